```python
import jax, jax.numpy as jnp
from jax import lax
import numpy as np

D_MODEL = 1024
BATCH = 2
SEQ = 8192
DEPTH = 1

CHUNK = 64
N_MEM = 256
EPS = 1e-6

SWA_HEADS = 16
SWA_KV_HEADS = 2
SWA_HEAD_DIM = 64
SWA_GROUP = SWA_HEADS // SWA_KV_HEADS
SWA_WINDOW = 128
SWA_WIN_CHUNKS = SWA_WINDOW // CHUNK
SWA_BLOCK = 128

GLA_HEADS = 4
GLA_KEY_DIM = D_MODEL // 2
GLA_VAL_DIM = D_MODEL
GLA_DK = GLA_KEY_DIM // GLA_HEADS
GLA_DV = GLA_VAL_DIM // GLA_HEADS
GLA_GATE_RANK = 16
GLA_GATE_NORM = 16.0

MEM_HEADS = 4
MEM_HEAD_DIM = 64
MEM_WIDTH = MEM_HEADS * MEM_HEAD_DIM

D_FF = -(-(8 * D_MODEL) // (3 * 256)) * 256

IN_SIZES = (SWA_HEADS * SWA_HEAD_DIM, SWA_KV_HEADS * SWA_HEAD_DIM, SWA_KV_HEADS * SWA_HEAD_DIM,
            GLA_KEY_DIM, GLA_KEY_DIM, GLA_VAL_DIM, GLA_VAL_DIM, GLA_GATE_RANK, 2 * D_MODEL)
IN_COLS = int(sum(IN_SIZES))
IN_OFFSETS = tuple(int(o) for o in np.cumsum(IN_SIZES)[:-1])

kernel_name = 'hybrid_swa_sinks_gla_memory_block'


def _rms(t, w):
    tf = t.astype(jnp.float32)
    y = tf * lax.rsqrt(jnp.mean(tf * tf, axis=-1, keepdims=True) + EPS)
    return y.astype(t.dtype) * w


def _swa_with_sinks(q, k, v, sinks):
    B, S = q.shape[0], q.shape[1]
    nb = S // SWA_BLOCK
    qb = q.reshape(B, nb, SWA_BLOCK, SWA_KV_HEADS, SWA_GROUP, SWA_HEAD_DIM)
    pad = ((0, 0), (SWA_BLOCK, 0), (0, 0), (0, 0))
    kp = jnp.pad(k, pad).reshape(B, nb + 1, SWA_BLOCK, SWA_KV_HEADS, SWA_HEAD_DIM)
    vp = jnp.pad(v, pad).reshape(B, nb + 1, SWA_BLOCK, SWA_KV_HEADS, SWA_HEAD_DIM)
    kb = jnp.concatenate([kp[:, :-1], kp[:, 1:]], axis=2)
    vb = jnp.concatenate([vp[:, :-1], vp[:, 1:]], axis=2)
    scale = SWA_HEAD_DIM ** -0.5
    s = jnp.einsum('bnqhgd,bnshd->bnhgqs', qb, kb).astype(jnp.float32) * scale
    n = jnp.arange(nb)[:, None, None]
    qpos = n * SWA_BLOCK + jnp.arange(SWA_BLOCK)[None, :, None]
    kpos = (n - 1) * SWA_BLOCK + jnp.arange(2 * SWA_BLOCK)[None, None, :]
    qch = qpos // CHUNK
    kch = kpos // CHUNK
    mask = (kpos >= 0) & (kch <= qch) & (kch >= qch - SWA_WIN_CHUNKS)
    s = jnp.where(mask[None, :, None, None], s, -1e30)
    sink = sinks.astype(jnp.float32).reshape(1, 1, SWA_KV_HEADS, SWA_GROUP, 1, 1)
    m = jnp.maximum(jnp.max(s, axis=-1, keepdims=True), sink)
    p = jnp.exp(s - m)
    denom = jnp.sum(p, axis=-1, keepdims=True) + jnp.exp(sink - m)
    probs = (p / denom).astype(v.dtype)
    o = jnp.einsum('bnhgqs,bnshd->bnqhgd', probs, vb)
    return o.reshape(B, S, SWA_HEADS * SWA_HEAD_DIM)


def _gla_chunk_causal(q, k, v, gk):
    B, S = q.shape[0], q.shape[1]
    nc = S // CHUNK
    f32 = jnp.float32
    r = lambda t: t.astype(f32).reshape(B, nc, CHUNK, GLA_HEADS, t.shape[-1])
    qc = r(q) * (GLA_DK ** -0.5)
    kc, vc, gc = r(k), r(v), r(gk)
    b = jnp.cumsum(gc, axis=2)
    b_end = b[:, :, -1:]
    k_dec = kc * jnp.exp(b_end - b)
    a = jnp.exp(b_end[:, :, 0])

    def step(state, xs):
        q_c, k_c, v_c, a_c = xs
        state = a_c[..., None] * state + jnp.einsum('bchk,bchv->bhkv', k_c, v_c)
        o_c = jnp.einsum('bchk,bhkv->bchv', q_c, state)
        return state, o_c

    xs = (jnp.moveaxis(qc, 1, 0), jnp.moveaxis(k_dec, 1, 0), jnp.moveaxis(vc, 1, 0), jnp.moveaxis(a, 1, 0))
    s0 = jnp.zeros((B, GLA_HEADS, GLA_DK, GLA_DV), f32)
    _, o = lax.scan(step, s0, xs)
    return jnp.moveaxis(o, 0, 1).reshape(B, S, GLA_HEADS, GLA_DV).astype(q.dtype)


def _mixer_block(h, w_in, b_gate, attn_sinks, gla_gate_w2, gla_gate_b, gla_norm_w, w_attn_o, w_gla_o, w_mix_o):
    B, S, _ = h.shape
    proj = h @ w_in
    q_a, k_a, v_a, q_g, k_g, v_g, g_g, a_lr, gates = jnp.split(proj, IN_OFFSETS, axis=-1)
    o_a = _swa_with_sinks(q_a.reshape(B, S, SWA_HEADS, SWA_HEAD_DIM),
                          k_a.reshape(B, S, SWA_KV_HEADS, SWA_HEAD_DIM),
                          v_a.reshape(B, S, SWA_KV_HEADS, SWA_HEAD_DIM), attn_sinks)
    gk = jax.nn.log_sigmoid((a_lr @ gla_gate_w2 + gla_gate_b).astype(jnp.float32)) / GLA_GATE_NORM
    o_g = _gla_chunk_causal(q_g.reshape(B, S, GLA_HEADS, GLA_DK),
                            k_g.reshape(B, S, GLA_HEADS, GLA_DK),
                            v_g.reshape(B, S, GLA_HEADS, GLA_DV),
                            gk.reshape(B, S, GLA_HEADS, GLA_DK))
    o_g = _rms(o_g, gla_norm_w) * jax.nn.silu(g_g.reshape(B, S, GLA_HEADS, GLA_DV))
    o_g = o_g.reshape(B, S, GLA_VAL_DIM)
    g_a, g_b = jnp.split(jax.nn.sigmoid(gates + b_gate), 2, axis=-1)
    merged = g_a * (o_a @ w_attn_o) + g_b * (o_g @ w_gla_o)
    return merged @ w_mix_o


def _memory_xattn(h, m, w_mem_q, w_mem_kv, w_mem_o):
    B, S, _ = h.shape
    q = (h @ w_mem_q).reshape(B, S, MEM_HEADS, MEM_HEAD_DIM)
    k, v = jnp.split(m @ w_mem_kv, 2, axis=-1)
    k = k.reshape(B, m.shape[1], MEM_HEADS, MEM_HEAD_DIM)
    v = v.reshape(B, m.shape[1], MEM_HEADS, MEM_HEAD_DIM)
    s = jnp.einsum('bshd,bmhd->bhsm', q, k).astype(jnp.float32) * (MEM_HEAD_DIM ** -0.5)
    p = jax.nn.softmax(s, axis=-1).astype(v.dtype)
    o = jnp.einsum('bhsm,bmhd->bshd', p, v).reshape(B, S, MEM_WIDTH)
    return o @ w_mem_o


def _swiglu(h, w_gate, w_up, w_down):
    return (jax.nn.silu(h @ w_gate) * (h @ w_up)) @ w_down


def setup_inputs(seed: int = 0) -> dict:
    key = jax.random.key(seed)
    ks = jax.random.split(key, 24)
    f32 = jnp.float32
    L = DEPTH

    def nrm(k, shape, fan_in):
        return jax.random.normal(k, shape, f32) * (fan_in ** -0.5)

    def gain(k, shape):
        return 1.0 + 0.02 * jax.random.normal(k, shape, f32)

    return {
        'x': jax.random.normal(ks[0], (BATCH, SEQ, D_MODEL), f32),
        'mem': jax.random.normal(ks[1], (BATCH, N_MEM, D_MODEL), f32),
        'norm_mix_w': gain(ks[2], (L, D_MODEL)),
        'w_in': nrm(ks[3], (L, D_MODEL, IN_COLS), D_MODEL),
        'b_gate': 0.02 * jax.random.normal(ks[4], (L, 2 * D_MODEL), f32),
        'attn_sinks': 0.5 * jax.random.normal(ks[5], (L, SWA_HEADS), f32),
        'gla_gate_w2': nrm(ks[6], (L, GLA_GATE_RANK, GLA_KEY_DIM), GLA_GATE_RANK),
        'gla_gate_b': 0.1 * jax.random.normal(ks[7], (L, GLA_KEY_DIM), f32),
        'gla_norm_w': gain(ks[8], (L, GLA_DV)),
        'w_attn_o': nrm(ks[9], (L, SWA_HEADS * SWA_HEAD_DIM, D_MODEL), SWA_HEADS * SWA_HEAD_DIM),
        'w_gla_o': nrm(ks[10], (L, GLA_VAL_DIM, D_MODEL), GLA_VAL_DIM),
        'w_mix_o': nrm(ks[11], (L, D_MODEL, D_MODEL), D_MODEL),
        'norm_mem_q_w': gain(ks[12], (L, D_MODEL)),
        'norm_mem_kv_w': gain(ks[13], (L, D_MODEL)),
        'w_mem_q': nrm(ks[14], (L, D_MODEL, MEM_WIDTH), D_MODEL),
        'w_mem_kv': nrm(ks[15], (L, D_MODEL, 2 * MEM_WIDTH), D_MODEL),
        'w_mem_o': nrm(ks[16], (L, MEM_WIDTH, D_MODEL), MEM_WIDTH),
        'norm_ffn_w': gain(ks[17], (L, D_MODEL)),
        'w_ffn_gate': nrm(ks[18], (L, D_MODEL, D_FF), D_MODEL),
        'w_ffn_up': nrm(ks[19], (L, D_MODEL, D_FF), D_MODEL),
        'w_ffn_down': nrm(ks[20], (L, D_FF, D_MODEL), D_FF),
        'norm_final_w': gain(ks[21], (D_MODEL,)),
    }


def reference(x, mem, norm_mix_w, w_in, b_gate, attn_sinks, gla_gate_w2, gla_gate_b, gla_norm_w,
              w_attn_o, w_gla_o, w_mix_o, norm_mem_q_w, norm_mem_kv_w, w_mem_q, w_mem_kv, w_mem_o,
              norm_ffn_w, w_ffn_gate, w_ffn_up, w_ffn_down, norm_final_w):
    for l in range(DEPTH):
        h = _rms(x, norm_mix_w[l])
        x = x + _mixer_block(h, w_in[l], b_gate[l], attn_sinks[l], gla_gate_w2[l], gla_gate_b[l],
                             gla_norm_w[l], w_attn_o[l], w_gla_o[l], w_mix_o[l])
        x = x + _memory_xattn(_rms(x, norm_mem_q_w[l]), _rms(mem, norm_mem_kv_w[l]),
                              w_mem_q[l], w_mem_kv[l], w_mem_o[l])
        x = x + _swiglu(_rms(x, norm_ffn_w[l]), w_ffn_gate[l], w_ffn_up[l], w_ffn_down[l])
    return _rms(x, norm_final_w)
```

```python
import functools

import jax
import jax.numpy as jnp
from jax import lax
from jax.experimental import pallas as pl
from jax.experimental.pallas import tpu as pltpu

D_MODEL = 1024
CHUNK = 64
N_MEM = 256
EPS = 1e-6

SWA_HEADS = 16
SWA_KV_HEADS = 2
SWA_HEAD_DIM = 64
SWA_BLOCK = 128

GLA_HEADS = 4
GLA_KEY_DIM = D_MODEL // 2
GLA_VAL_DIM = D_MODEL
GLA_DK = GLA_KEY_DIM // GLA_HEADS
GLA_DV = GLA_VAL_DIM // GLA_HEADS
GLA_GATE_RANK = 16
GLA_GATE_NORM = 16.0

MEM_HEADS = 4
MEM_HEAD_DIM = 64
MEM_WIDTH = MEM_HEADS * MEM_HEAD_DIM

D_FF = -(-(8 * D_MODEL) // (3 * 256)) * 256

IN_SIZES = (SWA_HEADS * SWA_HEAD_DIM, SWA_KV_HEADS * SWA_HEAD_DIM, SWA_KV_HEADS * SWA_HEAD_DIM,
            GLA_KEY_DIM, GLA_KEY_DIM, GLA_VAL_DIM, GLA_VAL_DIM, GLA_GATE_RANK, 2 * D_MODEL)

LANES = 128
VMEM_LIMIT = 56 * 1024 * 1024

TM_PROJ = 512
TC_GLA = 512
FF_CHUNK = 256

BF16 = jnp.bfloat16
F32 = jnp.float32


def _rms(x, w):
    return x * lax.rsqrt(jnp.mean(x * x, axis=-1, keepdims=True) + EPS) * w


def _dot(a, b):
    return jnp.dot(a, b, preferred_element_type=F32)


def _dot_nt(a, b):
    return lax.dot_general(a, b, (((1,), (1,)), ((), ())), preferred_element_type=F32)


def _const_spec(shape):
    zeros = (0,) * len(shape)
    return pl.BlockSpec(shape, lambda *_: zeros, pipeline_mode=pl.Buffered(1))


def _params(semantics):
    return pltpu.CompilerParams(dimension_semantics=semantics, vmem_limit_bytes=VMEM_LIMIT)


def _in_proj_kernel(x_ref, nw_ref, wqa_ref, wkva_ref, wqg_ref, wkg_ref, wvgt_ref, wgg_ref,
                    walr_ref, wgate_ref, w2_ref, gb_ref, bg_ref,
                    qa_ref, kva_ref, qg_ref, kg_ref, vgt_ref, gg_ref, gk_ref, gate_ref):
    h = _rms(x_ref[...], nw_ref[...]).astype(BF16)
    qa_ref[...] = (_dot(h, wqa_ref[...]) * (SWA_HEAD_DIM ** -0.5)).astype(BF16)
    kva_ref[...] = _dot(h, wkva_ref[...]).astype(BF16)
    qg_ref[...] = _dot(h, wqg_ref[...]).astype(BF16)
    kg_ref[...] = _dot(h, wkg_ref[...]).astype(BF16)
    vgt_ref[...] = _dot_nt(wvgt_ref[...], h).astype(BF16)
    g = _dot(h, wgg_ref[...])
    gg_ref[...] = (g * jax.nn.sigmoid(g)).astype(BF16)
    a_lr = _dot(h, walr_ref[...]).astype(BF16)
    z = _dot(a_lr, w2_ref[...]) + gb_ref[...]
    log_sig = jnp.minimum(z, 0.0) - jnp.log1p(jnp.exp(-jnp.abs(z)))
    gk_ref[...] = log_sig * (1.0 / GLA_GATE_NORM)
    gate_ref[...] = jax.nn.sigmoid(_dot(h, wgate_ref[...]) + bg_ref[...]).astype(BF16)


def _in_proj(x2d, nw, wqa, wkva, wqg, wkg, wvgt, wgg, walr, wgate, w2, gb, bg):
    t = x2d.shape[0]
    tm = TM_PROJ
    row = lambda n: pl.BlockSpec((tm, n), lambda i: (i, 0))
    consts = (nw, wqa, wkva, wqg, wkg, wvgt, wgg, walr, wgate, w2, gb, bg)
    out_shape = (
        jax.ShapeDtypeStruct((t, wqa.shape[1]), BF16),
        jax.ShapeDtypeStruct((t, wkva.shape[1]), BF16),
        jax.ShapeDtypeStruct((t, GLA_KEY_DIM), BF16),
        jax.ShapeDtypeStruct((t, GLA_KEY_DIM), BF16),
        jax.ShapeDtypeStruct((GLA_VAL_DIM, t), BF16),
        jax.ShapeDtypeStruct((t, GLA_VAL_DIM), BF16),
        jax.ShapeDtypeStruct((t, GLA_KEY_DIM), F32),
        jax.ShapeDtypeStruct((t, 2 * D_MODEL), BF16),
    )
    out_specs = (
        row(wqa.shape[1]), row(wkva.shape[1]), row(GLA_KEY_DIM), row(GLA_KEY_DIM),
        pl.BlockSpec((GLA_VAL_DIM, tm), lambda i: (0, i)),
        row(GLA_VAL_DIM), row(GLA_KEY_DIM), row(2 * D_MODEL),
    )
    return pl.pallas_call(
        _in_proj_kernel,
        grid=(t // tm,),
        in_specs=[row(D_MODEL)] + [_const_spec(c.shape) for c in consts],
        out_specs=out_specs,
        out_shape=out_shape,
        compiler_params=_params(("arbitrary",)),
        name="in_proj",
    )(x2d, *consts)


def _swa_kernel(sink_ref, q_ref, kv_ref, kvp_ref, o_ref):
    n = pl.program_id(1)
    blk = SWA_BLOCK
    lane = lax.broadcasted_iota(jnp.int32, (blk, LANES), 1)
    low = lane < SWA_HEAD_DIM
    row = lax.broadcasted_iota(jnp.int32, (2 * blk, 2 * blk), 0)
    col = lax.broadcasted_iota(jnp.int32, (2 * blk, 2 * blk), 1)
    q_chunk = (row % blk) // CHUNK
    k_chunk = col // CHUNK
    visible = (k_chunk >= q_chunk) & (k_chunk <= q_chunk + 2) & ((col >= blk) | (n > 0))
    first_head_rows = lax.broadcasted_iota(jnp.int32, (2 * blk, 1), 0) < blk
    pairs_per_kv = SWA_HEADS // SWA_KV_HEADS // 2
    for p in range(SWA_HEADS // 2):
        j = p // pairs_per_kv
        k2 = jnp.concatenate([kvp_ref[:, j * LANES:(j + 1) * LANES],
                              kv_ref[:, j * LANES:(j + 1) * LANES]], axis=0)
        v_off = SWA_KV_HEADS * LANES
        v2 = jnp.concatenate([kvp_ref[:, v_off + j * LANES:v_off + (j + 1) * LANES],
                              kv_ref[:, v_off + j * LANES:v_off + (j + 1) * LANES]], axis=0)
        qp = q_ref[:, p * LANES:(p + 1) * LANES]
        zero = jnp.zeros_like(qp)
        q2 = jnp.concatenate([jnp.where(low, qp, zero), jnp.where(low, zero, qp)], axis=0)
        s = _dot_nt(q2, k2)
        s = jnp.where(visible, s, -1e30)
        sink = jnp.where(first_head_rows, sink_ref[2 * p], sink_ref[2 * p + 1])
        m = jnp.maximum(jnp.max(s, axis=-1, keepdims=True), sink)
        e = jnp.exp(s - m)
        denom = jnp.sum(e, axis=-1, keepdims=True) + jnp.exp(sink - m)
        o2 = _dot(e.astype(BF16), v2) * (1.0 / denom)
        o_ref[:, p * LANES:(p + 1) * LANES] = jnp.where(low, o2[:blk], o2[blk:]).astype(BF16)


def _swa(sinks, qa, kva, batch, seq):
    nb = seq // SWA_BLOCK
    t = batch * seq
    kvw = kva.shape[1]
    return pl.pallas_call(
        _swa_kernel,
        grid=(batch, nb),
        in_specs=[
            pl.BlockSpec(memory_space=pltpu.SMEM),
            pl.BlockSpec((SWA_BLOCK, qa.shape[1]), lambda b, n: (b * nb + n, 0)),
            pl.BlockSpec((SWA_BLOCK, kvw), lambda b, n: (b * nb + n, 0)),
            pl.BlockSpec((SWA_BLOCK, kvw), lambda b, n: (b * nb + jnp.maximum(n - 1, 0), 0)),
        ],
        out_specs=pl.BlockSpec((SWA_BLOCK, qa.shape[1]), lambda b, n: (b * nb + n, 0)),
        out_shape=jax.ShapeDtypeStruct((t, qa.shape[1]), BF16),
        compiler_params=_params(("arbitrary", "arbitrary")),
        name="swa",
    )(sinks, qa, kva, kva)


def _gla_kernel(q_ref, k_ref, gk_ref, vt_ref, g_ref, nw_ref, o_ref, st_ref):
    @pl.when(pl.program_id(2) == 0)
    def _():
        st_ref[...] = jnp.zeros_like(st_ref)

    pair = 2 * CHUNK
    r = lax.broadcasted_iota(jnp.int32, (pair, pair), 0)
    c = lax.broadcasted_iota(jnp.int32, (pair, pair), 1)
    tri = ((c <= r) & ((r // CHUNK) == (c // CHUNK))).astype(BF16)
    first = lax.broadcasted_iota(jnp.int32, (pair, GLA_DK), 0) < CHUNK
    for i in range(TC_GLA // pair):
        sl = slice(i * pair, (i + 1) * pair)
        gk = gk_ref[sl, :]
        hi = gk.astype(BF16)
        lo = (gk - hi.astype(F32)).astype(BF16)
        b = _dot(tri, hi) + _dot(tri, lo)
        b_end0 = b[CHUNK - 1:CHUNK, :]
        b_end1 = b[pair - 1:pair, :]
        b_end = jnp.where(first, b_end0, b_end1)
        k_dec = (k_ref[sl, :].astype(F32) * jnp.exp(b_end - b)).astype(BF16)
        zero = jnp.zeros_like(k_dec)
        vt = vt_ref[:, sl]
        q = q_ref[sl, :]
        outs = []
        for half, (b_e, kd) in enumerate(((b_end0, jnp.where(first, k_dec, zero)),
                                          (b_end1, jnp.where(first, zero, k_dec)))):
            st = st_ref[...] * jnp.exp(b_e) + _dot(vt, kd)
            st_ref[...] = st
            qh = q[half * CHUNK:(half + 1) * CHUNK, :]
            outs.append(_dot_nt(qh, st.astype(BF16)))
        o = jnp.concatenate(outs, axis=0) * (GLA_DK ** -0.5)
        y = _rms(o, nw_ref[...])
        o_ref[sl, :] = (y * g_ref[sl, :].astype(F32)).astype(BF16)


def _gla(qg, kg, gk, vgt, gg, nw, batch, seq):
    ns = seq // TC_GLA
    t = batch * seq
    tok = lambda n: pl.BlockSpec((TC_GLA, n), lambda b, h, s: (b * ns + s, h))
    return pl.pallas_call(
        _gla_kernel,
        grid=(batch, GLA_HEADS, ns),
        in_specs=[
            tok(GLA_DK), tok(GLA_DK), tok(GLA_DK),
            pl.BlockSpec((GLA_DV, TC_GLA), lambda b, h, s: (h, b * ns + s)),
            tok(GLA_DV),
            pl.BlockSpec((1, GLA_DV), lambda b, h, s: (0, 0)),
        ],
        out_specs=tok(GLA_DV),
        out_shape=jax.ShapeDtypeStruct((t, GLA_VAL_DIM), BF16),
        scratch_shapes=[pltpu.VMEM((GLA_DV, GLA_DK), F32)],
        compiler_params=_params(("arbitrary", "arbitrary", "arbitrary")),
        name="gla",
    )(qg, kg, gk, vgt, gg, nw)


def _mix_kernel(x_ref, oa_ref, og_ref, gate_ref, wa_ref, wg_ref, wm_ref, o_ref):
    ya = _dot(oa_ref[...], wa_ref[...])
    yg = _dot(og_ref[...], wg_ref[...])
    merged = (gate_ref[:, :D_MODEL].astype(F32) * ya
              + gate_ref[:, D_MODEL:].astype(F32) * yg).astype(BF16)
    o_ref[...] = x_ref[...] + _dot(merged, wm_ref[...])


def _mix_out(x2d, oa, og, gates, wa, wg, wm):
    t = x2d.shape[0]
    tm = TM_PROJ
    row = lambda n: pl.BlockSpec((tm, n), lambda i: (i, 0))
    return pl.pallas_call(
        _mix_kernel,
        grid=(t // tm,),
        in_specs=[row(D_MODEL), row(oa.shape[1]), row(og.shape[1]), row(2 * D_MODEL),
                  _const_spec(wa.shape), _const_spec(wg.shape), _const_spec(wm.shape)],
        out_specs=row(D_MODEL),
        out_shape=jax.ShapeDtypeStruct((t, D_MODEL), F32),
        compiler_params=_params(("arbitrary",)),
        name="mix_out",
    )(x2d, oa, og, gates, wa, wg, wm)


def _mem_kv_kernel(m_ref, nw_ref, w_ref, k_ref, v_ref):
    mn = _rms(m_ref[0], nw_ref[...]).astype(BF16)
    kv = _dot(mn, w_ref[...])
    k = kv[:, :MEM_WIDTH].astype(BF16)
    v = kv[:, MEM_WIDTH:].astype(BF16)
    head = lax.broadcasted_iota(jnp.int32, (N_MEM, MEM_WIDTH), 1) // MEM_HEAD_DIM
    zero = jnp.zeros_like(k)
    for h in range(MEM_HEADS):
        k_ref[0, h] = jnp.where(head == h, k, zero)
        v_ref[0, h] = jnp.where(head == h, v, zero)


def _mem_kv(mem, nw, w):
    batch = mem.shape[0]
    out = jax.ShapeDtypeStruct((batch, MEM_HEADS, N_MEM, MEM_WIDTH), BF16)
    spec = pl.BlockSpec((1, MEM_HEADS, N_MEM, MEM_WIDTH), lambda b: (b, 0, 0, 0))
    return pl.pallas_call(
        _mem_kv_kernel,
        grid=(batch,),
        in_specs=[pl.BlockSpec((1, N_MEM, D_MODEL), lambda b: (b, 0, 0)),
                  _const_spec(nw.shape), _const_spec(w.shape)],
        out_specs=(spec, spec),
        out_shape=(out, out),
        compiler_params=_params(("arbitrary",)),
        name="mem_kv",
    )(mem, nw, w)


def _tail_kernel(x_ref, k_ref, v_ref, nq_ref, wq_ref, wo_ref, nf_ref, wg_ref, wu_ref, wd_ref,
                 nfin_ref, o_ref, act_ref):
    x = x_ref[...]
    hq = _rms(x, nq_ref[...]).astype(BF16)
    q = (_dot(hq, wq_ref[...]) * (MEM_HEAD_DIM ** -0.5)).astype(BF16)
    o = jnp.zeros((x.shape[0], MEM_WIDTH), F32)
    for h in range(MEM_HEADS):
        s = _dot_nt(q, k_ref[0, h])
        e = jnp.exp(s - jnp.max(s, axis=-1, keepdims=True))
        inv = 1.0 / jnp.sum(e, axis=-1, keepdims=True)
        o = o + _dot(e.astype(BF16), v_ref[0, h]) * inv
    x = x + _dot(o.astype(BF16), wo_ref[...])
    hf = _rms(x, nf_ref[...]).astype(BF16)
    for c in range(D_FF // FF_CHUNK):
        cs = slice(c * FF_CHUNK, (c + 1) * FF_CHUNK)
        g = _dot(hf, wg_ref[:, cs])
        u = _dot(hf, wu_ref[:, cs])
        act_ref[:, cs] = (g * jax.nn.sigmoid(g) * u).astype(BF16)
    x = x + _dot(act_ref[...], wd_ref[...])
    o_ref[...] = _rms(x, nfin_ref[...])


def _tail(x2d, kext, vext, nq, wq, wo, nf, wg, wu, wd, nfin, batch, seq):
    tm = TM_PROJ
    ns = seq // tm
    t = batch * seq
    row = pl.BlockSpec((tm, D_MODEL), lambda b, s: (b * ns + s, 0))
    mem_spec = pl.BlockSpec((1, MEM_HEADS, N_MEM, MEM_WIDTH), lambda b, s: (b, 0, 0, 0))
    consts = (nq, wq, wo, nf, wg, wu, wd, nfin)
    return pl.pallas_call(
        _tail_kernel,
        grid=(batch, ns),
        in_specs=[row, mem_spec, mem_spec] + [_const_spec(c.shape) for c in consts],
        out_specs=row,
        out_shape=jax.ShapeDtypeStruct((t, D_MODEL), F32),
        scratch_shapes=[pltpu.VMEM((tm, D_FF), BF16)],
        compiler_params=_params(("arbitrary", "arbitrary")),
        name="tail",
    )(x2d, kext, vext, *consts)


def _split_w_in(w):
    offs = [0]
    for n in IN_SIZES:
        offs.append(offs[-1] + n)
    return [w[:, offs[i]:offs[i + 1]] for i in range(len(IN_SIZES))]


def _layer(x2d, mem, batch, seq, norm_mix_w, w_in, b_gate, attn_sinks, gla_gate_w2, gla_gate_b,
           gla_norm_w, w_attn_o, w_gla_o, w_mix_o, norm_mem_q_w, norm_mem_kv_w, w_mem_q,
           w_mem_kv, w_mem_o, norm_ffn_w, w_ffn_gate, w_ffn_up, w_ffn_down, out_norm_w):
    wqa, wka, wva, wqg, wkg, wvg, wgg, walr, wgate = _split_w_in(w_in)
    hd = SWA_HEAD_DIM
    dup = lambda w: jnp.concatenate(
        [w[:, j * hd:(j + 1) * hd] for j in range(SWA_KV_HEADS) for _ in range(2)], axis=1)
    wkva = jnp.concatenate([dup(wka), dup(wva)], axis=1)
    walr_p = jnp.pad(walr, ((0, 0), (0, LANES - GLA_GATE_RANK)))
    w2_p = jnp.pad(gla_gate_w2, ((0, LANES - GLA_GATE_RANK), (0, 0)))
    bf = lambda a: a.astype(BF16)
    r2 = lambda a: a.reshape(1, -1)

    qa, kva, qg, kg, vgt, gg, gk, gates = _in_proj(
        x2d, r2(norm_mix_w), bf(wqa), bf(wkva), bf(wqg), bf(wkg), bf(wvg.T), bf(wgg),
        bf(walr_p), bf(wgate), bf(w2_p), r2(gla_gate_b), r2(b_gate))
    oa = _swa(attn_sinks, qa, kva, batch, seq)
    og = _gla(qg, kg, gk, vgt, gg, r2(gla_norm_w), batch, seq)
    x1 = _mix_out(x2d, oa, og, gates, bf(w_attn_o), bf(w_gla_o), bf(w_mix_o))
    kext, vext = _mem_kv(mem, r2(norm_mem_kv_w), bf(w_mem_kv))
    return _tail(x1, kext, vext, r2(norm_mem_q_w), bf(w_mem_q), bf(w_mem_o), r2(norm_ffn_w),
                 bf(w_ffn_gate), bf(w_ffn_up), bf(w_ffn_down), r2(out_norm_w), batch, seq)


def kernel(x, mem, norm_mix_w, w_in, b_gate, attn_sinks, gla_gate_w2, gla_gate_b, gla_norm_w,
           w_attn_o, w_gla_o, w_mix_o, norm_mem_q_w, norm_mem_kv_w, w_mem_q, w_mem_kv, w_mem_o,
           norm_ffn_w, w_ffn_gate, w_ffn_up, w_ffn_down, norm_final_w):
    batch, seq, d = x.shape
    depth = w_in.shape[0]
    assert depth == 1 and d == D_MODEL and seq % TM_PROJ == 0 and seq % TC_GLA == 0
    out = _layer(x.reshape(batch * seq, d), mem, batch, seq, norm_mix_w[0], w_in[0], b_gate[0],
                 attn_sinks[0], gla_gate_w2[0], gla_gate_b[0], gla_norm_w[0], w_attn_o[0],
                 w_gla_o[0], w_mix_o[0], norm_mem_q_w[0], norm_mem_kv_w[0], w_mem_q[0],
                 w_mem_kv[0], w_mem_o[0], norm_ffn_w[0], w_ffn_gate[0], w_ffn_up[0],
                 w_ffn_down[0], norm_final_w)
    return out.reshape(batch, seq, d)
```

```python
import functools

import jax
import jax.numpy as jnp
from jax import lax
from jax.experimental import pallas as pl
from jax.experimental.pallas import tpu as pltpu

D_MODEL = 1024
CHUNK = 64
N_MEM = 256
EPS = 1e-6

SWA_HEADS = 16
SWA_KV_HEADS = 2
SWA_HEAD_DIM = 64
SWA_BLOCK = 128

GLA_HEADS = 4
GLA_KEY_DIM = D_MODEL // 2
GLA_VAL_DIM = D_MODEL
GLA_DK = GLA_KEY_DIM // GLA_HEADS
GLA_DV = GLA_VAL_DIM // GLA_HEADS
GLA_GATE_RANK = 16
GLA_GATE_NORM = 16.0

MEM_HEADS = 4
MEM_HEAD_DIM = 64
MEM_WIDTH = MEM_HEADS * MEM_HEAD_DIM

D_FF = -(-(8 * D_MODEL) // (3 * 256)) * 256

IN_SIZES = (SWA_HEADS * SWA_HEAD_DIM, SWA_KV_HEADS * SWA_HEAD_DIM, SWA_KV_HEADS * SWA_HEAD_DIM,
            GLA_KEY_DIM, GLA_KEY_DIM, GLA_VAL_DIM, GLA_VAL_DIM, GLA_GATE_RANK, 2 * D_MODEL)

LANES = 128
VMEM_LIMIT = 56 * 1024 * 1024

LOG2E = 1.4426950408889634

TM_PROJ = 512
SWA_NB = 2
TC_GLA = 512
FF_CHUNK = 256

BF16 = jnp.bfloat16
F32 = jnp.float32


def _rms(x, w):
    return x * lax.rsqrt(jnp.mean(x * x, axis=-1, keepdims=True) + EPS) * w


def _dot(a, b):
    return jnp.dot(a, b, preferred_element_type=F32)


def _dot_nt(a, b):
    return lax.dot_general(a, b, (((1,), (1,)), ((), ())), preferred_element_type=F32)


def _const_spec(shape):
    zeros = (0,) * len(shape)
    return pl.BlockSpec(shape, lambda *_: zeros, pipeline_mode=pl.Buffered(1))


def _params(semantics):
    return pltpu.CompilerParams(dimension_semantics=semantics, vmem_limit_bytes=VMEM_LIMIT)


def _in_proj_kernel(x_ref, nw_ref, wqa_ref, wka_ref, wvat_ref, wqg_ref, wkg_ref, wvgt_ref,
                    wgg_ref, walr_ref, wgate_ref, w2_ref, gb_ref, bg_ref, gn_ref,
                    qa_ref, ka_ref, vat_ref, qg_ref, kg_ref, vgt_ref, gg_ref, gk_ref, gate_ref):
    h = _rms(x_ref[...], nw_ref[...]).astype(BF16)
    qa_ref[...] = (_dot(h, wqa_ref[...]) * (SWA_HEAD_DIM ** -0.5 * LOG2E)).astype(BF16)
    ka_ref[...] = _dot(h, wka_ref[...]).astype(BF16)
    vat_ref[...] = _dot_nt(wvat_ref[...], h).astype(BF16)
    qg_ref[...] = _dot(h, wqg_ref[...]).astype(BF16)
    kg_ref[...] = _dot(h, wkg_ref[...]).astype(BF16)
    vgt_ref[...] = _dot_nt(wvgt_ref[...], h).astype(BF16)
    g = _dot(h, wgg_ref[...])
    gg_ref[...] = (g * jax.nn.sigmoid(g) * gn_ref[...]).astype(BF16)
    a_lr = _dot(h, walr_ref[...]).astype(BF16)
    z = _dot(a_lr, w2_ref[...]) + gb_ref[...]
    log_sig = jnp.minimum(z, 0.0) - jnp.log1p(jnp.exp(-jnp.abs(z)))
    gk_ref[...] = log_sig * (1.0 / GLA_GATE_NORM)
    gate_ref[...] = jax.nn.sigmoid(_dot(h, wgate_ref[...]) + bg_ref[...]).astype(BF16)


def _in_proj(x2d, nw, wqa, wka, wvat, wqg, wkg, wvgt, wgg, walr, wgate, w2, gb, bg, gn):
    t = x2d.shape[0]
    tm = TM_PROJ
    row = lambda n: pl.BlockSpec((tm, n), lambda i: (i, 0))
    consts = (nw, wqa, wka, wvat, wqg, wkg, wvgt, wgg, walr, wgate, w2, gb, bg, gn)
    out_shape = (
        jax.ShapeDtypeStruct((t, wqa.shape[1]), BF16),
        jax.ShapeDtypeStruct((t, wka.shape[1]), BF16),
        jax.ShapeDtypeStruct((wvat.shape[0], t), BF16),
        jax.ShapeDtypeStruct((t, GLA_KEY_DIM), BF16),
        jax.ShapeDtypeStruct((t, GLA_KEY_DIM), BF16),
        jax.ShapeDtypeStruct((GLA_VAL_DIM, t), BF16),
        jax.ShapeDtypeStruct((t, GLA_VAL_DIM), BF16),
        jax.ShapeDtypeStruct((t, GLA_KEY_DIM), F32),
        jax.ShapeDtypeStruct((t, 2 * D_MODEL), BF16),
    )
    out_specs = (
        row(wqa.shape[1]), row(wka.shape[1]),
        pl.BlockSpec((wvat.shape[0], tm), lambda i: (0, i)),
        row(GLA_KEY_DIM), row(GLA_KEY_DIM),
        pl.BlockSpec((GLA_VAL_DIM, tm), lambda i: (0, i)),
        row(GLA_VAL_DIM), row(GLA_KEY_DIM), row(2 * D_MODEL),
    )
    return pl.pallas_call(
        _in_proj_kernel,
        grid=(t // tm,),
        in_specs=[row(D_MODEL)] + [_const_spec(c.shape) for c in consts],
        out_specs=out_specs,
        out_shape=out_shape,
        compiler_params=_params(("arbitrary",)),
        name="in_proj",
    )(x2d, *consts)


def _swa_kernel(sink_ref, q_ref, k_ref, kp_ref, vt_ref, vtp_ref, o_ref):
    blk = SWA_BLOCK
    win = blk + CHUNK
    low = lax.broadcasted_iota(jnp.int32, (CHUNK, LANES), 1) < SWA_HEAD_DIM
    low_row = lax.broadcasted_iota(jnp.int32, (1, LANES), 1) < SWA_HEAD_DIM
    prev_bias = jnp.where(pl.program_id(1) == 0, -1e30, 0.0).astype(F32)
    k_all = jnp.concatenate([kp_ref[...], k_ref[...]], axis=0)
    vt_all = jnp.concatenate([vtp_ref[...], vt_ref[...]], axis=1)
    zeros_chunk = jnp.zeros((CHUNK, LANES), BF16)
    pairs_per_kv = SWA_HEADS // SWA_KV_HEADS // 2
    n_pairs = SWA_HEADS // 2
    chunks = [(u, qh) for u in range(SWA_NB) for qh in range(blk // CHUNK)]

    def scores(p):
        kj = k_all[:, (p // pairs_per_kv) * LANES:(p // pairs_per_kv + 1) * LANES]
        res = []
        for u, qh in chunks:
            r0 = u * blk + qh * CHUNK
            qp = q_ref[r0:r0 + CHUNK, p * LANES:(p + 1) * LANES]
            zero = jnp.zeros_like(qp)
            qsel = jnp.concatenate([jnp.where(low, qp, zero), jnp.where(low, zero, qp)], axis=0)
            res.append(_dot_nt(kj[r0:r0 + win], qsel))
        return res

    def softmax(p, sts):
        sink_row = jnp.where(low_row, sink_ref[2 * p], sink_ref[2 * p + 1]) * LOG2E
        res = []
        for (u, qh), st in zip(chunks, sts):
            if u == 0:
                n_prev = blk - qh * CHUNK
                st = jnp.concatenate([st[:n_prev] + prev_bias, st[n_prev:]], axis=0)
            m = jnp.maximum(jnp.max(st, axis=0, keepdims=True), sink_row)
            e = jnp.exp2(st - m)
            denom = jnp.sum(e, axis=0, keepdims=True) + jnp.exp2(sink_row - m)
            e_full = jnp.concatenate([zeros_chunk] * qh + [e.astype(BF16)]
                                     + [zeros_chunk] * (1 - qh), axis=0)
            res.append((e_full, 1.0 / denom))
        return res

    def weighted_values(p, probs):
        vtj = vt_all[(p // pairs_per_kv) * LANES:(p // pairs_per_kv + 1) * LANES, :]
        return [_dot(vtj[:, u * blk:(u + 2) * blk], e_full) * inv
                for (u, qh), (e_full, inv) in zip(chunks, probs)]

    def store(p, ots):
        for (u, qh), ot in zip(chunks, ots):
            r0 = u * blk + qh * CHUNK
            o2 = ot.T
            o_ref[r0:r0 + CHUNK, p * LANES:(p + 1) * LANES] = jnp.where(
                low, o2[:CHUNK], o2[CHUNK:]).astype(BF16)

    sts = scores(0)
    ots_prev = None
    for p in range(n_pairs):
        sts_next = scores(p + 1) if p + 1 < n_pairs else None
        ots = weighted_values(p, softmax(p, sts))
        if ots_prev is not None:
            store(p - 1, ots_prev)
        sts, ots_prev = sts_next, ots
    store(n_pairs - 1, ots_prev)


def _swa(sinks, qa, ka, vat, batch, seq):
    nb = seq // SWA_BLOCK
    ns = nb // SWA_NB
    t = batch * seq
    rows = SWA_NB * SWA_BLOCK
    kw = ka.shape[1]
    prev = lambda b, s: b * nb + jnp.maximum(s * SWA_NB - 1, 0)
    return pl.pallas_call(
        _swa_kernel,
        grid=(batch, ns),
        in_specs=[
            pl.BlockSpec(memory_space=pltpu.SMEM),
            pl.BlockSpec((rows, qa.shape[1]), lambda b, s: (b * ns + s, 0)),
            pl.BlockSpec((rows, kw), lambda b, s: (b * ns + s, 0)),
            pl.BlockSpec((SWA_BLOCK, kw), lambda b, s: (prev(b, s), 0)),
            pl.BlockSpec((kw, rows), lambda b, s: (0, b * ns + s)),
            pl.BlockSpec((kw, SWA_BLOCK), lambda b, s: (0, prev(b, s))),
        ],
        out_specs=pl.BlockSpec((rows, qa.shape[1]), lambda b, s: (b * ns + s, 0)),
        out_shape=jax.ShapeDtypeStruct((t, qa.shape[1]), BF16),
        compiler_params=_params(("arbitrary", "arbitrary")),
        name="swa",
    )(sinks, qa, ka, ka, vat, vat)


def _gla_kernel(q_ref, k_ref, gk_ref, vt_ref, g_ref, o_ref, st_ref):
    @pl.when(pl.program_id(1) == 0)
    def _():
        st_ref[...] = jnp.zeros_like(st_ref)

    pair = 2 * CHUNK
    heads = range(GLA_HEADS)
    ks = [slice(h * GLA_DK, (h + 1) * GLA_DK) for h in heads]
    vs = [slice(h * GLA_DV, (h + 1) * GLA_DV) for h in heads]
    r = lax.broadcasted_iota(jnp.int32, (pair, pair), 0)
    c = lax.broadcasted_iota(jnp.int32, (pair, pair), 1)
    tri = ((c <= r) & ((r // CHUNK) == (c // CHUNK))).astype(BF16)
    tri2 = jnp.concatenate([tri, tri], axis=1)
    first = lax.broadcasted_iota(jnp.int32, (pair, GLA_DK), 0) < CHUNK
    eps_scaled = EPS * GLA_DK

    def decay_and_kv(i):
        sl = slice(i * pair, (i + 1) * pair)
        gk = gk_ref[sl, :]
        hi = gk.astype(BF16)
        lo = (gk - hi.astype(F32)).astype(BF16)
        b_all = _dot(tri2, jnp.concatenate([hi, lo], axis=0))
        res = []
        for h in heads:
            b = b_all[:, ks[h]]
            b_end0 = b[CHUNK - 1:CHUNK, :]
            b_end1 = b[pair - 1:pair, :]
            k_dec = (k_ref[sl, ks[h]].astype(F32)
                     * jnp.exp(jnp.where(first, b_end0, b_end1) - b)).astype(BF16)
            zero = jnp.zeros_like(k_dec)
            kd = jnp.concatenate([jnp.where(first, k_dec, zero), jnp.where(first, zero, k_dec)],
                                 axis=1)
            kv = _dot(vt_ref[vs[h], sl], kd)
            res.append((jnp.exp(b_end0), jnp.exp(b_end1), kv))
        return res

    def state_and_out(i, sts, dk):
        sl = slice(i * pair, (i + 1) * pair)
        outs = []
        for h in heads:
            a0, a1, kv = dk[h]
            st0 = sts[h] * a0 + kv[:, :GLA_DK]
            st1 = st0 * a1 + kv[:, GLA_DK:]
            sts[h] = st1
            q = q_ref[sl, ks[h]]
            outs.append(jnp.concatenate([_dot_nt(q[:CHUNK], st0.astype(BF16)),
                                         _dot_nt(q[CHUNK:], st1.astype(BF16))], axis=0))
        for h in heads:
            o = outs[h]
            inv = lax.rsqrt(jnp.mean(o * o, axis=-1, keepdims=True) + eps_scaled)
            o_ref[sl, vs[h]] = (o * inv * g_ref[sl, vs[h]].astype(F32)).astype(BF16)

    sts = [st_ref[h] for h in heads]
    n_pairs = TC_GLA // pair
    cur = decay_and_kv(0)
    for i in range(n_pairs):
        nxt = decay_and_kv(i + 1) if i + 1 < n_pairs else None
        state_and_out(i, sts, cur)
        cur = nxt
    for h in heads:
        st_ref[h] = sts[h]


def _gla(qg, kg, gk, vgt, gg, batch, seq):
    ns = seq // TC_GLA
    t = batch * seq
    tok = lambda n: pl.BlockSpec((TC_GLA, n), lambda b, s: (b * ns + s, 0))
    return pl.pallas_call(
        _gla_kernel,
        grid=(batch, ns),
        in_specs=[
            tok(GLA_KEY_DIM), tok(GLA_KEY_DIM), tok(GLA_KEY_DIM),
            pl.BlockSpec((GLA_VAL_DIM, TC_GLA), lambda b, s: (0, b * ns + s)),
            tok(GLA_VAL_DIM),
        ],
        out_specs=tok(GLA_VAL_DIM),
        out_shape=jax.ShapeDtypeStruct((t, GLA_VAL_DIM), BF16),
        scratch_shapes=[pltpu.VMEM((GLA_HEADS, GLA_DV, GLA_DK), F32)],
        compiler_params=_params(("arbitrary", "arbitrary")),
        name="gla",
    )(qg, kg, gk, vgt, gg)


def _mix_kernel(x_ref, oa_ref, og_ref, gate_ref, wa_ref, wg_ref, wm_ref, o_ref):
    ya = _dot(oa_ref[...], wa_ref[...])
    yg = _dot(og_ref[...], wg_ref[...])
    merged = (gate_ref[:, :D_MODEL].astype(F32) * ya
              + gate_ref[:, D_MODEL:].astype(F32) * yg).astype(BF16)
    o_ref[...] = x_ref[...] + _dot(merged, wm_ref[...])


def _mix_out(x2d, oa, og, gates, wa, wg, wm):
    t = x2d.shape[0]
    tm = TM_PROJ
    row = lambda n: pl.BlockSpec((tm, n), lambda i: (i, 0))
    return pl.pallas_call(
        _mix_kernel,
        grid=(t // tm,),
        in_specs=[row(D_MODEL), row(oa.shape[1]), row(og.shape[1]), row(2 * D_MODEL),
                  _const_spec(wa.shape), _const_spec(wg.shape), _const_spec(wm.shape)],
        out_specs=row(D_MODEL),
        out_shape=jax.ShapeDtypeStruct((t, D_MODEL), F32),
        compiler_params=_params(("arbitrary",)),
        name="mix_out",
    )(x2d, oa, og, gates, wa, wg, wm)


def _mem_kv_kernel(m_ref, nw_ref, w_ref, k_ref, v_ref):
    mn = _rms(m_ref[0], nw_ref[...]).astype(BF16)
    kv = _dot(mn, w_ref[...])
    k = kv[:, :MEM_WIDTH].astype(BF16)
    v = kv[:, MEM_WIDTH:].astype(BF16)
    head = lax.broadcasted_iota(jnp.int32, (N_MEM, MEM_WIDTH), 1) // MEM_HEAD_DIM
    zero = jnp.zeros_like(k)
    for h in range(MEM_HEADS):
        k_ref[0, h] = jnp.where(head == h, k, zero)
        v_ref[0, h] = jnp.where(head == h, v, zero)


def _mem_kv(mem, nw, w):
    batch = mem.shape[0]
    out = jax.ShapeDtypeStruct((batch, MEM_HEADS, N_MEM, MEM_WIDTH), BF16)
    spec = pl.BlockSpec((1, MEM_HEADS, N_MEM, MEM_WIDTH), lambda b: (b, 0, 0, 0))
    return pl.pallas_call(
        _mem_kv_kernel,
        grid=(batch,),
        in_specs=[pl.BlockSpec((1, N_MEM, D_MODEL), lambda b: (b, 0, 0)),
                  _const_spec(nw.shape), _const_spec(w.shape)],
        out_specs=(spec, spec),
        out_shape=(out, out),
        compiler_params=_params(("arbitrary",)),
        name="mem_kv",
    )(mem, nw, w)


def _tail_kernel(x_ref, k_ref, v_ref, nq_ref, wq_ref, wo_ref, nf_ref, wg_ref, wu_ref, wd_ref,
                 nfin_ref, o_ref, act_ref):
    x = x_ref[...]
    hq = _rms(x, nq_ref[...]).astype(BF16)
    q = (_dot(hq, wq_ref[...]) * (MEM_HEAD_DIM ** -0.5)).astype(BF16)
    o = jnp.zeros((x.shape[0], MEM_WIDTH), F32)
    for h in range(MEM_HEADS):
        s = _dot_nt(q, k_ref[0, h])
        e = jnp.exp(s - jnp.max(s, axis=-1, keepdims=True))
        inv = 1.0 / jnp.sum(e, axis=-1, keepdims=True)
        o = o + _dot(e.astype(BF16), v_ref[0, h]) * inv
    x = x + _dot(o.astype(BF16), wo_ref[...])
    hf = _rms(x, nf_ref[...]).astype(BF16)
    for c in range(D_FF // FF_CHUNK):
        cs = slice(c * FF_CHUNK, (c + 1) * FF_CHUNK)
        g = _dot(hf, wg_ref[:, cs])
        u = _dot(hf, wu_ref[:, cs])
        act_ref[:, cs] = (g * jax.nn.sigmoid(g) * u).astype(BF16)
    x = x + _dot(act_ref[...], wd_ref[...])
    o_ref[...] = _rms(x, nfin_ref[...])


def _tail(x2d, kext, vext, nq, wq, wo, nf, wg, wu, wd, nfin, batch, seq):
    tm = TM_PROJ
    ns = seq // tm
    t = batch * seq
    row = pl.BlockSpec((tm, D_MODEL), lambda b, s: (b * ns + s, 0))
    mem_spec = pl.BlockSpec((1, MEM_HEADS, N_MEM, MEM_WIDTH), lambda b, s: (b, 0, 0, 0))
    consts = (nq, wq, wo, nf, wg, wu, wd, nfin)
    return pl.pallas_call(
        _tail_kernel,
        grid=(batch, ns),
        in_specs=[row, mem_spec, mem_spec] + [_const_spec(c.shape) for c in consts],
        out_specs=row,
        out_shape=jax.ShapeDtypeStruct((t, D_MODEL), F32),
        scratch_shapes=[pltpu.VMEM((tm, D_FF), BF16)],
        compiler_params=_params(("arbitrary", "arbitrary")),
        name="tail",
    )(x2d, kext, vext, *consts)


def _split_w_in(w):
    offs = [0]
    for n in IN_SIZES:
        offs.append(offs[-1] + n)
    return [w[:, offs[i]:offs[i + 1]] for i in range(len(IN_SIZES))]


def _layer(x2d, mem, batch, seq, norm_mix_w, w_in, b_gate, attn_sinks, gla_gate_w2, gla_gate_b,
           gla_norm_w, w_attn_o, w_gla_o, w_mix_o, norm_mem_q_w, norm_mem_kv_w, w_mem_q,
           w_mem_kv, w_mem_o, norm_ffn_w, w_ffn_gate, w_ffn_up, w_ffn_down, out_norm_w):
    wqa, wka, wva, wqg, wkg, wvg, wgg, walr, wgate = _split_w_in(w_in)
    hd = SWA_HEAD_DIM
    dup = lambda w: jnp.concatenate(
        [w[:, j * hd:(j + 1) * hd] for j in range(SWA_KV_HEADS) for _ in range(2)], axis=1)
    walr_p = jnp.pad(walr, ((0, 0), (0, LANES - GLA_GATE_RANK)))
    w2_p = jnp.pad(gla_gate_w2, ((0, LANES - GLA_GATE_RANK), (0, 0)))
    bf = lambda a: a.astype(BF16)
    r2 = lambda a: a.reshape(1, -1)

    qa, ka, vat, qg, kg, vgt, gg, gk, gates = _in_proj(
        x2d, r2(norm_mix_w), bf(wqa), bf(dup(wka)), bf(dup(wva).T), bf(wqg), bf(wkg), bf(wvg.T),
        bf(wgg), bf(walr_p), bf(wgate), bf(w2_p), r2(gla_gate_b), r2(b_gate),
        r2(jnp.tile(gla_norm_w, GLA_HEADS)))
    oa = _swa(attn_sinks, qa, ka, vat, batch, seq)
    og = _gla(qg, kg, gk, vgt, gg, batch, seq)
    x1 = _mix_out(x2d, oa, og, gates, bf(w_attn_o), bf(w_gla_o), bf(w_mix_o))
    kext, vext = _mem_kv(mem, r2(norm_mem_kv_w), bf(w_mem_kv))
    return _tail(x1, kext, vext, r2(norm_mem_q_w), bf(w_mem_q), bf(w_mem_o), r2(norm_ffn_w),
                 bf(w_ffn_gate), bf(w_ffn_up), bf(w_ffn_down), r2(out_norm_w), batch, seq)


def kernel(x, mem, norm_mix_w, w_in, b_gate, attn_sinks, gla_gate_w2, gla_gate_b, gla_norm_w,
           w_attn_o, w_gla_o, w_mix_o, norm_mem_q_w, norm_mem_kv_w, w_mem_q, w_mem_kv, w_mem_o,
           norm_ffn_w, w_ffn_gate, w_ffn_up, w_ffn_down, norm_final_w):
    batch, seq, d = x.shape
    depth = w_in.shape[0]
    assert depth == 1 and d == D_MODEL and seq % TM_PROJ == 0 and seq % TC_GLA == 0
    out = _layer(x.reshape(batch * seq, d), mem, batch, seq, norm_mix_w[0], w_in[0], b_gate[0],
                 attn_sinks[0], gla_gate_w2[0], gla_gate_b[0], gla_norm_w[0], w_attn_o[0],
                 w_gla_o[0], w_mix_o[0], norm_mem_q_w[0], norm_mem_kv_w[0], w_mem_q[0],
                 w_mem_kv[0], w_mem_o[0], norm_ffn_w[0], w_ffn_gate[0], w_ffn_up[0],
                 w_ffn_down[0], norm_final_w)
    return out.reshape(batch, seq, d)
```

```python
import functools

import jax
import jax.numpy as jnp
from jax import lax
from jax.experimental import pallas as pl
from jax.experimental.pallas import tpu as pltpu

D_MODEL = 1024
CHUNK = 64
N_MEM = 256
EPS = 1e-6

SWA_HEADS = 16
SWA_KV_HEADS = 2
SWA_HEAD_DIM = 64
SWA_BLOCK = 128

GLA_HEADS = 4
GLA_KEY_DIM = D_MODEL // 2
GLA_VAL_DIM = D_MODEL
GLA_DK = GLA_KEY_DIM // GLA_HEADS
GLA_DV = GLA_VAL_DIM // GLA_HEADS
GLA_GATE_RANK = 16
GLA_GATE_NORM = 16.0

MEM_HEADS = 4
MEM_HEAD_DIM = 64
MEM_WIDTH = MEM_HEADS * MEM_HEAD_DIM

D_FF = -(-(8 * D_MODEL) // (3 * 256)) * 256

IN_SIZES = (SWA_HEADS * SWA_HEAD_DIM, SWA_KV_HEADS * SWA_HEAD_DIM, SWA_KV_HEADS * SWA_HEAD_DIM,
            GLA_KEY_DIM, GLA_KEY_DIM, GLA_VAL_DIM, GLA_VAL_DIM, GLA_GATE_RANK, 2 * D_MODEL)

LANES = 128
VMEM_LIMIT = 56 * 1024 * 1024

LOG2E = 1.4426950408889634

TM_PROJ = 1024
SWA_NB = 2
TC_GLA = 512
FF_CHUNK = 256

BF16 = jnp.bfloat16
F32 = jnp.float32


def _rms(x, w):
    return x * lax.rsqrt(jnp.mean(x * x, axis=-1, keepdims=True) + EPS) * w


def _dot(a, b):
    return jnp.dot(a, b, preferred_element_type=F32)


def _dot_nt(a, b):
    return lax.dot_general(a, b, (((1,), (1,)), ((), ())), preferred_element_type=F32)


def _const_spec(shape):
    zeros = (0,) * len(shape)
    return pl.BlockSpec(shape, lambda *_: zeros, pipeline_mode=pl.Buffered(1))


def _params(semantics):
    return pltpu.CompilerParams(dimension_semantics=semantics, vmem_limit_bytes=VMEM_LIMIT)


def _in_proj_kernel(x_ref, nw_ref, wqa_ref, wka_ref, wvat_ref, wqg_ref, wkg_ref, wvgt_ref,
                    wgg_ref, walr_ref, wgate_ref, w2_ref, gb_ref, bg_ref, gn_ref,
                    qa_ref, ka_ref, vat_ref, qg_ref, kg_ref, vgt_ref, gg_ref, gk_ref, gate_ref):
    h = _rms(x_ref[...], nw_ref[...]).astype(BF16)
    qa_ref[...] = (_dot(h, wqa_ref[...]) * (SWA_HEAD_DIM ** -0.5 * LOG2E)).astype(BF16)
    ka_ref[...] = _dot(h, wka_ref[...]).astype(BF16)
    vat_ref[...] = _dot_nt(wvat_ref[...], h).astype(BF16)
    qg_ref[...] = _dot(h, wqg_ref[...]).astype(BF16)
    kg_ref[...] = _dot(h, wkg_ref[...]).astype(BF16)
    vgt_ref[...] = _dot_nt(wvgt_ref[...], h).astype(BF16)
    g = _dot(h, wgg_ref[...])
    gg_ref[...] = (g * jax.nn.sigmoid(g) * gn_ref[...]).astype(BF16)
    a_lr = _dot(h, walr_ref[...]).astype(BF16)
    z = _dot(a_lr, w2_ref[...]) + gb_ref[...]
    log_sig = jnp.minimum(z, 0.0) - jnp.log1p(jnp.exp(-jnp.abs(z)))
    gk_ref[...] = log_sig * (1.0 / GLA_GATE_NORM)
    gate_ref[...] = jax.nn.sigmoid(_dot(h, wgate_ref[...]) + bg_ref[...]).astype(BF16)


def _in_proj(x2d, nw, wqa, wka, wvat, wqg, wkg, wvgt, wgg, walr, wgate, w2, gb, bg, gn):
    t = x2d.shape[0]
    tm = TM_PROJ
    row = lambda n: pl.BlockSpec((tm, n), lambda i: (i, 0))
    consts = (nw, wqa, wka, wvat, wqg, wkg, wvgt, wgg, walr, wgate, w2, gb, bg, gn)
    out_shape = (
        jax.ShapeDtypeStruct((t, wqa.shape[1]), BF16),
        jax.ShapeDtypeStruct((t, wka.shape[1]), BF16),
        jax.ShapeDtypeStruct((wvat.shape[0], t), BF16),
        jax.ShapeDtypeStruct((t, GLA_KEY_DIM), BF16),
        jax.ShapeDtypeStruct((t, GLA_KEY_DIM), BF16),
        jax.ShapeDtypeStruct((GLA_VAL_DIM, t), BF16),
        jax.ShapeDtypeStruct((t, GLA_VAL_DIM), BF16),
        jax.ShapeDtypeStruct((t, GLA_KEY_DIM), F32),
        jax.ShapeDtypeStruct((t, 2 * D_MODEL), BF16),
    )
    out_specs = (
        row(wqa.shape[1]), row(wka.shape[1]),
        pl.BlockSpec((wvat.shape[0], tm), lambda i: (0, i)),
        row(GLA_KEY_DIM), row(GLA_KEY_DIM),
        pl.BlockSpec((GLA_VAL_DIM, tm), lambda i: (0, i)),
        row(GLA_VAL_DIM), row(GLA_KEY_DIM), row(2 * D_MODEL),
    )
    return pl.pallas_call(
        _in_proj_kernel,
        grid=(t // tm,),
        in_specs=[row(D_MODEL)] + [_const_spec(c.shape) for c in consts],
        out_specs=out_specs,
        out_shape=out_shape,
        compiler_params=_params(("arbitrary",)),
        name="in_proj",
    )(x2d, *consts)


def _swa_kernel(sink_ref, q_ref, k_ref, kp_ref, vt_ref, vtp_ref, o_ref):
    blk = SWA_BLOCK
    win = blk + CHUNK
    low = lax.broadcasted_iota(jnp.int32, (CHUNK, LANES), 1) < SWA_HEAD_DIM
    low_row = lax.broadcasted_iota(jnp.int32, (1, LANES), 1) < SWA_HEAD_DIM
    prev_bias = jnp.where(pl.program_id(1) == 0, -1e30, 0.0).astype(F32)
    k_all = jnp.concatenate([kp_ref[...], k_ref[...]], axis=0)
    vt_all = jnp.concatenate([vtp_ref[...], vt_ref[...]], axis=1)
    zeros_chunk = jnp.zeros((CHUNK, LANES), BF16)
    pairs_per_kv = SWA_HEADS // SWA_KV_HEADS // 2
    n_pairs = SWA_HEADS // 2
    chunks = [(u, qh) for u in range(SWA_NB) for qh in range(blk // CHUNK)]

    def scores(p):
        kj = k_all[:, (p // pairs_per_kv) * LANES:(p // pairs_per_kv + 1) * LANES]
        res = []
        for u, qh in chunks:
            r0 = u * blk + qh * CHUNK
            qp = q_ref[r0:r0 + CHUNK, p * LANES:(p + 1) * LANES]
            zero = jnp.zeros_like(qp)
            qsel = jnp.concatenate([jnp.where(low, qp, zero), jnp.where(low, zero, qp)], axis=0)
            res.append(_dot_nt(kj[r0:r0 + win], qsel))
        return res

    def softmax(p, sts):
        sink_row = jnp.where(low_row, sink_ref[2 * p], sink_ref[2 * p + 1]) * LOG2E
        res = []
        for (u, qh), st in zip(chunks, sts):
            if u == 0:
                n_prev = blk - qh * CHUNK
                st = jnp.concatenate([st[:n_prev] + prev_bias, st[n_prev:]], axis=0)
            m = jnp.maximum(jnp.max(st, axis=0, keepdims=True), sink_row)
            e = jnp.exp2(st - m)
            denom = jnp.sum(e, axis=0, keepdims=True) + jnp.exp2(sink_row - m)
            e_full = jnp.concatenate([zeros_chunk] * qh + [e.astype(BF16)]
                                     + [zeros_chunk] * (1 - qh), axis=0)
            res.append((e_full, 1.0 / denom))
        return res

    def weighted_values(p, probs):
        vtj = vt_all[(p // pairs_per_kv) * LANES:(p // pairs_per_kv + 1) * LANES, :]
        return [_dot(vtj[:, u * blk:(u + 2) * blk], e_full) * inv
                for (u, qh), (e_full, inv) in zip(chunks, probs)]

    def store(p, ots):
        for (u, qh), ot in zip(chunks, ots):
            r0 = u * blk + qh * CHUNK
            o2 = ot.T
            o_ref[r0:r0 + CHUNK, p * LANES:(p + 1) * LANES] = jnp.where(
                low, o2[:CHUNK], o2[CHUNK:]).astype(BF16)

    sts = scores(0)
    ots_prev = None
    for p in range(n_pairs):
        sts_next = scores(p + 1) if p + 1 < n_pairs else None
        ots = weighted_values(p, softmax(p, sts))
        if ots_prev is not None:
            store(p - 1, ots_prev)
        sts, ots_prev = sts_next, ots
    store(n_pairs - 1, ots_prev)


def _swa(sinks, qa, ka, vat, batch, seq):
    nb = seq // SWA_BLOCK
    ns = nb // SWA_NB
    t = batch * seq
    rows = SWA_NB * SWA_BLOCK
    kw = ka.shape[1]
    prev = lambda b, s: b * nb + jnp.maximum(s * SWA_NB - 1, 0)
    return pl.pallas_call(
        _swa_kernel,
        grid=(batch, ns),
        in_specs=[
            pl.BlockSpec(memory_space=pltpu.SMEM),
            pl.BlockSpec((rows, qa.shape[1]), lambda b, s: (b * ns + s, 0)),
            pl.BlockSpec((rows, kw), lambda b, s: (b * ns + s, 0)),
            pl.BlockSpec((SWA_BLOCK, kw), lambda b, s: (prev(b, s), 0)),
            pl.BlockSpec((kw, rows), lambda b, s: (0, b * ns + s)),
            pl.BlockSpec((kw, SWA_BLOCK), lambda b, s: (0, prev(b, s))),
        ],
        out_specs=pl.BlockSpec((rows, qa.shape[1]), lambda b, s: (b * ns + s, 0)),
        out_shape=jax.ShapeDtypeStruct((t, qa.shape[1]), BF16),
        compiler_params=_params(("arbitrary", "arbitrary")),
        name="swa",
    )(sinks, qa, ka, ka, vat, vat)


def _gla_kernel(q_ref, k_ref, gk_ref, vt_ref, g_ref, o_ref, st_ref):
    @pl.when(pl.program_id(1) == 0)
    def _():
        st_ref[...] = jnp.zeros_like(st_ref)

    pair = 2 * CHUNK
    heads = range(GLA_HEADS)
    ks = [slice(h * GLA_DK, (h + 1) * GLA_DK) for h in heads]
    vs = [slice(h * GLA_DV, (h + 1) * GLA_DV) for h in heads]
    r = lax.broadcasted_iota(jnp.int32, (pair, pair), 0)
    c = lax.broadcasted_iota(jnp.int32, (pair, pair), 1)
    tri = ((c <= r) & ((r // CHUNK) == (c // CHUNK))).astype(BF16)
    tri2 = jnp.concatenate([tri, tri], axis=1)
    first = lax.broadcasted_iota(jnp.int32, (pair, GLA_DK), 0) < CHUNK
    eps_scaled = EPS * GLA_DK

    def decay_and_kv(i):
        sl = slice(i * pair, (i + 1) * pair)
        gk = gk_ref[sl, :]
        hi = gk.astype(BF16)
        lo = (gk - hi.astype(F32)).astype(BF16)
        b_all = _dot(tri2, jnp.concatenate([hi, lo], axis=0))
        res = []
        for h in heads:
            b = b_all[:, ks[h]]
            b_end0 = b[CHUNK - 1:CHUNK, :]
            b_end1 = b[pair - 1:pair, :]
            k_dec = (k_ref[sl, ks[h]].astype(F32)
                     * jnp.exp(jnp.where(first, b_end0, b_end1) - b)).astype(BF16)
            zero = jnp.zeros_like(k_dec)
            kd = jnp.concatenate([jnp.where(first, k_dec, zero), jnp.where(first, zero, k_dec)],
                                 axis=1)
            kv = _dot(vt_ref[vs[h], sl], kd)
            res.append((jnp.exp(b_end0), jnp.exp(b_end1), kv))
        return res

    def state_and_out(i, sts, dk):
        sl = slice(i * pair, (i + 1) * pair)
        outs = []
        for h in heads:
            a0, a1, kv = dk[h]
            st0 = sts[h] * a0 + kv[:, :GLA_DK]
            st1 = st0 * a1 + kv[:, GLA_DK:]
            sts[h] = st1
            q = q_ref[sl, ks[h]]
            outs.append(jnp.concatenate([_dot_nt(q[:CHUNK], st0.astype(BF16)),
                                         _dot_nt(q[CHUNK:], st1.astype(BF16))], axis=0))
        for h in heads:
            o = outs[h]
            inv = lax.rsqrt(jnp.mean(o * o, axis=-1, keepdims=True) + eps_scaled)
            o_ref[sl, vs[h]] = (o * inv * g_ref[sl, vs[h]].astype(F32)).astype(BF16)

    sts = [st_ref[h] for h in heads]
    n_pairs = TC_GLA // pair
    cur = decay_and_kv(0)
    for i in range(n_pairs):
        nxt = decay_and_kv(i + 1) if i + 1 < n_pairs else None
        state_and_out(i, sts, cur)
        cur = nxt
    for h in heads:
        st_ref[h] = sts[h]


def _gla(qg, kg, gk, vgt, gg, batch, seq):
    ns = seq // TC_GLA
    t = batch * seq
    tok = lambda n: pl.BlockSpec((TC_GLA, n), lambda b, s: (b * ns + s, 0))
    return pl.pallas_call(
        _gla_kernel,
        grid=(batch, ns),
        in_specs=[
            tok(GLA_KEY_DIM), tok(GLA_KEY_DIM), tok(GLA_KEY_DIM),
            pl.BlockSpec((GLA_VAL_DIM, TC_GLA), lambda b, s: (0, b * ns + s)),
            tok(GLA_VAL_DIM),
        ],
        out_specs=tok(GLA_VAL_DIM),
        out_shape=jax.ShapeDtypeStruct((t, GLA_VAL_DIM), BF16),
        scratch_shapes=[pltpu.VMEM((GLA_HEADS, GLA_DV, GLA_DK), F32)],
        compiler_params=_params(("arbitrary", "arbitrary")),
        name="gla",
    )(qg, kg, gk, vgt, gg)


def _mix_kernel(x_ref, oa_ref, og_ref, gate_ref, wa_ref, wg_ref, wm_ref, o_ref):
    ya = _dot(oa_ref[...], wa_ref[...])
    yg = _dot(og_ref[...], wg_ref[...])
    merged = (gate_ref[:, :D_MODEL].astype(F32) * ya
              + gate_ref[:, D_MODEL:].astype(F32) * yg).astype(BF16)
    o_ref[...] = x_ref[...] + _dot(merged, wm_ref[...])


def _mix_out(x2d, oa, og, gates, wa, wg, wm):
    t = x2d.shape[0]
    tm = TM_PROJ
    row = lambda n: pl.BlockSpec((tm, n), lambda i: (i, 0))
    return pl.pallas_call(
        _mix_kernel,
        grid=(t // tm,),
        in_specs=[row(D_MODEL), row(oa.shape[1]), row(og.shape[1]), row(2 * D_MODEL),
                  _const_spec(wa.shape), _const_spec(wg.shape), _const_spec(wm.shape)],
        out_specs=row(D_MODEL),
        out_shape=jax.ShapeDtypeStruct((t, D_MODEL), F32),
        compiler_params=_params(("arbitrary",)),
        name="mix_out",
    )(x2d, oa, og, gates, wa, wg, wm)


def _mem_kv_kernel(m_ref, nw_ref, w_ref, k_ref, v_ref):
    mn = _rms(m_ref[0], nw_ref[...]).astype(BF16)
    kv = _dot(mn, w_ref[...])
    k = kv[:, :MEM_WIDTH].astype(BF16)
    v = kv[:, MEM_WIDTH:].astype(BF16)
    head = lax.broadcasted_iota(jnp.int32, (N_MEM, MEM_WIDTH), 1) // MEM_HEAD_DIM
    zero = jnp.zeros_like(k)
    for h in range(MEM_HEADS):
        k_ref[0, h] = jnp.where(head == h, k, zero)
        v_ref[0, h] = jnp.where(head == h, v, zero)


def _mem_kv(mem, nw, w):
    batch = mem.shape[0]
    out = jax.ShapeDtypeStruct((batch, MEM_HEADS, N_MEM, MEM_WIDTH), BF16)
    spec = pl.BlockSpec((1, MEM_HEADS, N_MEM, MEM_WIDTH), lambda b: (b, 0, 0, 0))
    return pl.pallas_call(
        _mem_kv_kernel,
        grid=(batch,),
        in_specs=[pl.BlockSpec((1, N_MEM, D_MODEL), lambda b: (b, 0, 0)),
                  _const_spec(nw.shape), _const_spec(w.shape)],
        out_specs=(spec, spec),
        out_shape=(out, out),
        compiler_params=_params(("arbitrary",)),
        name="mem_kv",
    )(mem, nw, w)


def _tail_kernel(x_ref, k_ref, v_ref, nq_ref, wq_ref, wo_ref, nf_ref, wg_ref, wu_ref, wd_ref,
                 nfin_ref, o_ref, act_ref):
    x = x_ref[...]
    hq = _rms(x, nq_ref[...]).astype(BF16)
    q = (_dot(hq, wq_ref[...]) * (MEM_HEAD_DIM ** -0.5)).astype(BF16)
    o = jnp.zeros((x.shape[0], MEM_WIDTH), F32)
    for h in range(MEM_HEADS):
        s = _dot_nt(q, k_ref[0, h])
        e = jnp.exp(s - jnp.max(s, axis=-1, keepdims=True))
        inv = 1.0 / jnp.sum(e, axis=-1, keepdims=True)
        o = o + _dot(e.astype(BF16), v_ref[0, h]) * inv
    x = x + _dot(o.astype(BF16), wo_ref[...])
    hf = _rms(x, nf_ref[...]).astype(BF16)
    for c in range(D_FF // FF_CHUNK):
        cs = slice(c * FF_CHUNK, (c + 1) * FF_CHUNK)
        g = _dot(hf, wg_ref[:, cs])
        u = _dot(hf, wu_ref[:, cs])
        act_ref[:, cs] = (g * jax.nn.sigmoid(g) * u).astype(BF16)
    x = x + _dot(act_ref[...], wd_ref[...])
    o_ref[...] = _rms(x, nfin_ref[...])


def _tail(x2d, kext, vext, nq, wq, wo, nf, wg, wu, wd, nfin, batch, seq):
    tm = TM_PROJ
    ns = seq // tm
    t = batch * seq
    row = pl.BlockSpec((tm, D_MODEL), lambda b, s: (b * ns + s, 0))
    mem_spec = pl.BlockSpec((1, MEM_HEADS, N_MEM, MEM_WIDTH), lambda b, s: (b, 0, 0, 0))
    consts = (nq, wq, wo, nf, wg, wu, wd, nfin)
    return pl.pallas_call(
        _tail_kernel,
        grid=(batch, ns),
        in_specs=[row, mem_spec, mem_spec] + [_const_spec(c.shape) for c in consts],
        out_specs=row,
        out_shape=jax.ShapeDtypeStruct((t, D_MODEL), F32),
        scratch_shapes=[pltpu.VMEM((tm, D_FF), BF16)],
        compiler_params=_params(("arbitrary", "arbitrary")),
        name="tail",
    )(x2d, kext, vext, *consts)


def _split_w_in(w):
    offs = [0]
    for n in IN_SIZES:
        offs.append(offs[-1] + n)
    return [w[:, offs[i]:offs[i + 1]] for i in range(len(IN_SIZES))]


def _layer(x2d, mem, batch, seq, norm_mix_w, w_in, b_gate, attn_sinks, gla_gate_w2, gla_gate_b,
           gla_norm_w, w_attn_o, w_gla_o, w_mix_o, norm_mem_q_w, norm_mem_kv_w, w_mem_q,
           w_mem_kv, w_mem_o, norm_ffn_w, w_ffn_gate, w_ffn_up, w_ffn_down, out_norm_w):
    wqa, wka, wva, wqg, wkg, wvg, wgg, walr, wgate = _split_w_in(w_in)
    hd = SWA_HEAD_DIM
    dup = lambda w: jnp.concatenate(
        [w[:, j * hd:(j + 1) * hd] for j in range(SWA_KV_HEADS) for _ in range(2)], axis=1)
    walr_p = jnp.pad(walr, ((0, 0), (0, LANES - GLA_GATE_RANK)))
    w2_p = jnp.pad(gla_gate_w2, ((0, LANES - GLA_GATE_RANK), (0, 0)))
    bf = lambda a: a.astype(BF16)
    r2 = lambda a: a.reshape(1, -1)

    qa, ka, vat, qg, kg, vgt, gg, gk, gates = _in_proj(
        x2d, r2(norm_mix_w), bf(wqa), bf(dup(wka)), bf(dup(wva).T), bf(wqg), bf(wkg), bf(wvg.T),
        bf(wgg), bf(walr_p), bf(wgate), bf(w2_p), r2(gla_gate_b), r2(b_gate),
        r2(jnp.tile(gla_norm_w, GLA_HEADS)))
    oa = _swa(attn_sinks, qa, ka, vat, batch, seq)
    og = _gla(qg, kg, gk, vgt, gg, batch, seq)
    x1 = _mix_out(x2d, oa, og, gates, bf(w_attn_o), bf(w_gla_o), bf(w_mix_o))
    kext, vext = _mem_kv(mem, r2(norm_mem_kv_w), bf(w_mem_kv))
    return _tail(x1, kext, vext, r2(norm_mem_q_w), bf(w_mem_q), bf(w_mem_o), r2(norm_ffn_w),
                 bf(w_ffn_gate), bf(w_ffn_up), bf(w_ffn_down), r2(out_norm_w), batch, seq)


def kernel(x, mem, norm_mix_w, w_in, b_gate, attn_sinks, gla_gate_w2, gla_gate_b, gla_norm_w,
           w_attn_o, w_gla_o, w_mix_o, norm_mem_q_w, norm_mem_kv_w, w_mem_q, w_mem_kv, w_mem_o,
           norm_ffn_w, w_ffn_gate, w_ffn_up, w_ffn_down, norm_final_w):
    batch, seq, d = x.shape
    depth = w_in.shape[0]
    assert depth == 1 and d == D_MODEL and seq % TM_PROJ == 0 and seq % TC_GLA == 0
    out = _layer(x.reshape(batch * seq, d), mem, batch, seq, norm_mix_w[0], w_in[0], b_gate[0],
                 attn_sinks[0], gla_gate_w2[0], gla_gate_b[0], gla_norm_w[0], w_attn_o[0],
                 w_gla_o[0], w_mix_o[0], norm_mem_q_w[0], norm_mem_kv_w[0], w_mem_q[0],
                 w_mem_kv[0], w_mem_o[0], norm_ffn_w[0], w_ffn_gate[0], w_ffn_up[0],
                 w_ffn_down[0], norm_final_w)
    return out.reshape(batch, seq, d)
```

```python
import functools

import jax
import jax.numpy as jnp
from jax import lax
from jax.experimental import pallas as pl
from jax.experimental.pallas import tpu as pltpu

D_MODEL = 1024
CHUNK = 64
N_MEM = 256
EPS = 1e-6

SWA_HEADS = 16
SWA_KV_HEADS = 2
SWA_HEAD_DIM = 64
SWA_BLOCK = 128

GLA_HEADS = 4
GLA_KEY_DIM = D_MODEL // 2
GLA_VAL_DIM = D_MODEL
GLA_DK = GLA_KEY_DIM // GLA_HEADS
GLA_DV = GLA_VAL_DIM // GLA_HEADS
GLA_GATE_RANK = 16
GLA_GATE_NORM = 16.0

MEM_HEADS = 4
MEM_HEAD_DIM = 64
MEM_WIDTH = MEM_HEADS * MEM_HEAD_DIM

D_FF = -(-(8 * D_MODEL) // (3 * 256)) * 256

IN_SIZES = (SWA_HEADS * SWA_HEAD_DIM, SWA_KV_HEADS * SWA_HEAD_DIM, SWA_KV_HEADS * SWA_HEAD_DIM,
            GLA_KEY_DIM, GLA_KEY_DIM, GLA_VAL_DIM, GLA_VAL_DIM, GLA_GATE_RANK, 2 * D_MODEL)
IN_OFFSETS = tuple(sum(IN_SIZES[:i]) for i in range(len(IN_SIZES) + 1))

LANES = 128
VMEM_LIMIT = 56 * 1024 * 1024

LOG2E = 1.4426950408889634

TM_PROJ = 1024
SWA_NB = 2
TC_GLA = 512
FF_CHUNK = 256

BF16 = jnp.bfloat16
F32 = jnp.float32


def _rms(x, w):
    return x * lax.rsqrt(jnp.mean(x * x, axis=-1, keepdims=True) + EPS) * w


def _dot(a, b):
    return jnp.dot(a, b, preferred_element_type=F32)


def _dot_nt(a, b):
    return lax.dot_general(a, b, (((1,), (1,)), ((), ())), preferred_element_type=F32)


def _const_spec(shape):
    zeros = (0,) * len(shape)
    return pl.BlockSpec(shape, lambda *_: zeros, pipeline_mode=pl.Buffered(1))


def _params(semantics):
    return pltpu.CompilerParams(dimension_semantics=semantics, vmem_limit_bytes=VMEM_LIMIT)


def _in_proj_kernel(x_ref, nw_ref, wt_ref, w2_ref, gb_ref, bg_ref, gn_ref,
                    qa_ref, ka_ref, vat_ref, qg_ref, kg_ref, vgt_ref, gg_ref, gk_ref, gate_ref):
    o_qa, o_ka, o_va, o_qg, o_kg, o_vg, o_gg, o_alr, o_gate, o_end = IN_OFFSETS
    hd = SWA_HEAD_DIM
    h = _rms(x_ref[...], nw_ref[...]).astype(BF16)
    qa_ref[...] = (_dot_nt(h, wt_ref[o_qa:o_ka, :]) * (hd ** -0.5 * LOG2E)).astype(BF16)
    twice = lambda o: jnp.concatenate(
        [wt_ref[o + j * hd:o + (j + 1) * hd, :] for j in range(SWA_KV_HEADS) for _ in range(2)],
        axis=0)
    ka_ref[...] = _dot_nt(h, twice(o_ka)).astype(BF16)
    vat_ref[...] = _dot_nt(twice(o_va), h).astype(BF16)
    qg_ref[...] = _dot_nt(h, wt_ref[o_qg:o_kg, :]).astype(BF16)
    kg_ref[...] = _dot_nt(h, wt_ref[o_kg:o_vg, :]).astype(BF16)
    vgt_ref[...] = _dot_nt(wt_ref[o_vg:o_gg, :], h).astype(BF16)
    g = _dot_nt(h, wt_ref[o_gg:o_alr, :])
    gg_ref[...] = (g * jax.nn.sigmoid(g) * gn_ref[...]).astype(BF16)
    a_lr = _dot_nt(h, wt_ref[o_alr:o_alr + LANES, :]).astype(BF16)
    z = _dot(a_lr, w2_ref[...]) + gb_ref[...]
    log_sig = jnp.minimum(z, 0.0) - jnp.log1p(jnp.exp(-jnp.abs(z)))
    gk_ref[...] = log_sig * (1.0 / GLA_GATE_NORM)
    gate_ref[...] = jax.nn.sigmoid(_dot_nt(h, wt_ref[o_gate:o_end, :]) + bg_ref[...]).astype(BF16)


def _in_proj(x2d, nw, wt, w2, gb, bg, gn):
    t = x2d.shape[0]
    tm = TM_PROJ
    row = lambda n: pl.BlockSpec((tm, n), lambda i: (i, 0))
    col = lambda n: pl.BlockSpec((n, tm), lambda i: (0, i))
    consts = (nw, wt, w2, gb, bg, gn)
    kw = 2 * SWA_KV_HEADS * SWA_HEAD_DIM
    tok = lambda n, dt=BF16: jax.ShapeDtypeStruct((t, n), dt)
    out_shape = (
        tok(SWA_HEADS * SWA_HEAD_DIM), tok(kw), jax.ShapeDtypeStruct((kw, t), BF16),
        tok(GLA_KEY_DIM), tok(GLA_KEY_DIM), jax.ShapeDtypeStruct((GLA_VAL_DIM, t), BF16),
        tok(GLA_VAL_DIM), tok(GLA_KEY_DIM, F32), tok(2 * D_MODEL),
    )
    out_specs = (
        row(SWA_HEADS * SWA_HEAD_DIM), row(kw), col(kw),
        row(GLA_KEY_DIM), row(GLA_KEY_DIM), col(GLA_VAL_DIM),
        row(GLA_VAL_DIM), row(GLA_KEY_DIM), row(2 * D_MODEL),
    )
    return pl.pallas_call(
        _in_proj_kernel,
        grid=(t // tm,),
        in_specs=[row(D_MODEL)] + [_const_spec(c.shape) for c in consts],
        out_specs=out_specs,
        out_shape=out_shape,
        compiler_params=_params(("arbitrary",)),
        name="in_proj",
    )(x2d, *consts)


def _swa_kernel(sink_ref, q_ref, k_ref, kp_ref, vt_ref, vtp_ref, o_ref):
    blk = SWA_BLOCK
    win = blk + CHUNK
    low = lax.broadcasted_iota(jnp.int32, (CHUNK, LANES), 1) < SWA_HEAD_DIM
    low_row = lax.broadcasted_iota(jnp.int32, (1, LANES), 1) < SWA_HEAD_DIM
    prev_bias = jnp.where(pl.program_id(1) == 0, -1e30, 0.0).astype(F32)
    k_all = jnp.concatenate([kp_ref[...], k_ref[...]], axis=0)
    vt_all = jnp.concatenate([vtp_ref[...], vt_ref[...]], axis=1)
    zeros_chunk = jnp.zeros((CHUNK, LANES), BF16)
    pairs_per_kv = SWA_HEADS // SWA_KV_HEADS // 2
    n_pairs = SWA_HEADS // 2
    chunks = [(u, qh) for u in range(SWA_NB) for qh in range(blk // CHUNK)]

    def scores(p):
        kj = k_all[:, (p // pairs_per_kv) * LANES:(p // pairs_per_kv + 1) * LANES]
        res = []
        for u, qh in chunks:
            r0 = u * blk + qh * CHUNK
            qp = q_ref[r0:r0 + CHUNK, p * LANES:(p + 1) * LANES]
            zero = jnp.zeros_like(qp)
            qsel = jnp.concatenate([jnp.where(low, qp, zero), jnp.where(low, zero, qp)], axis=0)
            res.append(_dot_nt(kj[r0:r0 + win], qsel))
        return res

    def softmax(p, sts):
        sink_row = jnp.where(low_row, sink_ref[2 * p], sink_ref[2 * p + 1]) * LOG2E
        res = []
        for (u, qh), st in zip(chunks, sts):
            if u == 0:
                n_prev = blk - qh * CHUNK
                st = jnp.concatenate([st[:n_prev] + prev_bias, st[n_prev:]], axis=0)
            m = jnp.maximum(jnp.max(st, axis=0, keepdims=True), sink_row)
            e = jnp.exp2(st - m)
            denom = jnp.sum(e, axis=0, keepdims=True) + jnp.exp2(sink_row - m)
            e_full = jnp.concatenate([zeros_chunk] * qh + [e.astype(BF16)]
                                     + [zeros_chunk] * (1 - qh), axis=0)
            res.append((e_full, 1.0 / denom))
        return res

    def weighted_values(p, probs):
        vtj = vt_all[(p // pairs_per_kv) * LANES:(p // pairs_per_kv + 1) * LANES, :]
        return [_dot(vtj[:, u * blk:(u + 2) * blk], e_full) * inv
                for (u, qh), (e_full, inv) in zip(chunks, probs)]

    def store(p, ots):
        for (u, qh), ot in zip(chunks, ots):
            r0 = u * blk + qh * CHUNK
            o2 = ot.T
            o_ref[r0:r0 + CHUNK, p * LANES:(p + 1) * LANES] = jnp.where(
                low, o2[:CHUNK], o2[CHUNK:]).astype(BF16)

    sts = scores(0)
    ots_prev = None
    for p in range(n_pairs):
        sts_next = scores(p + 1) if p + 1 < n_pairs else None
        ots = weighted_values(p, softmax(p, sts))
        if ots_prev is not None:
            store(p - 1, ots_prev)
        sts, ots_prev = sts_next, ots
    store(n_pairs - 1, ots_prev)


def _swa(sinks, qa, ka, vat, batch, seq):
    nb = seq // SWA_BLOCK
    ns = nb // SWA_NB
    t = batch * seq
    rows = SWA_NB * SWA_BLOCK
    kw = ka.shape[1]
    prev = lambda b, s: b * nb + jnp.maximum(s * SWA_NB - 1, 0)
    return pl.pallas_call(
        _swa_kernel,
        grid=(batch, ns),
        in_specs=[
            pl.BlockSpec(memory_space=pltpu.SMEM),
            pl.BlockSpec((rows, qa.shape[1]), lambda b, s: (b * ns + s, 0)),
            pl.BlockSpec((rows, kw), lambda b, s: (b * ns + s, 0)),
            pl.BlockSpec((SWA_BLOCK, kw), lambda b, s: (prev(b, s), 0)),
            pl.BlockSpec((kw, rows), lambda b, s: (0, b * ns + s)),
            pl.BlockSpec((kw, SWA_BLOCK), lambda b, s: (0, prev(b, s))),
        ],
        out_specs=pl.BlockSpec((rows, qa.shape[1]), lambda b, s: (b * ns + s, 0)),
        out_shape=jax.ShapeDtypeStruct((t, qa.shape[1]), BF16),
        compiler_params=_params(("arbitrary", "arbitrary")),
        name="swa",
    )(sinks, qa, ka, ka, vat, vat)


def _gla_kernel(q_ref, k_ref, gk_ref, vt_ref, g_ref, o_ref, st_ref):
    @pl.when(pl.program_id(1) == 0)
    def _():
        st_ref[...] = jnp.zeros_like(st_ref)

    pair = 2 * CHUNK
    heads = range(GLA_HEADS)
    ks = [slice(h * GLA_DK, (h + 1) * GLA_DK) for h in heads]
    vs = [slice(h * GLA_DV, (h + 1) * GLA_DV) for h in heads]
    r = lax.broadcasted_iota(jnp.int32, (pair, pair), 0)
    c = lax.broadcasted_iota(jnp.int32, (pair, pair), 1)
    tri = ((c <= r) & ((r // CHUNK) == (c // CHUNK))).astype(BF16)
    tri2 = jnp.concatenate([tri, tri], axis=1)
    first = lax.broadcasted_iota(jnp.int32, (pair, GLA_DK), 0) < CHUNK
    eps_scaled = EPS * GLA_DK

    def decay_and_kv(i):
        sl = slice(i * pair, (i + 1) * pair)
        gk = gk_ref[sl, :]
        hi = gk.astype(BF16)
        lo = (gk - hi.astype(F32)).astype(BF16)
        b_all = _dot(tri2, jnp.concatenate([hi, lo], axis=0))
        res = []
        for h in heads:
            b = b_all[:, ks[h]]
            b_end0 = b[CHUNK - 1:CHUNK, :]
            b_end1 = b[pair - 1:pair, :]
            k_dec = (k_ref[sl, ks[h]].astype(F32)
                     * jnp.exp(jnp.where(first, b_end0, b_end1) - b)).astype(BF16)
            zero = jnp.zeros_like(k_dec)
            kd = jnp.concatenate([jnp.where(first, k_dec, zero), jnp.where(first, zero, k_dec)],
                                 axis=1)
            kv = _dot(vt_ref[vs[h], sl], kd)
            res.append((jnp.exp(b_end0), jnp.exp(b_end1), kv))
        return res

    def state_and_out(i, sts, dk):
        sl = slice(i * pair, (i + 1) * pair)
        outs = []
        for h in heads:
            a0, a1, kv = dk[h]
            st0 = sts[h] * a0 + kv[:, :GLA_DK]
            st1 = st0 * a1 + kv[:, GLA_DK:]
            sts[h] = st1
            q = q_ref[sl, ks[h]]
            outs.append(jnp.concatenate([_dot_nt(q[:CHUNK], st0.astype(BF16)),
                                         _dot_nt(q[CHUNK:], st1.astype(BF16))], axis=0))
        for h in heads:
            o = outs[h]
            inv = lax.rsqrt(jnp.mean(o * o, axis=-1, keepdims=True) + eps_scaled)
            o_ref[sl, vs[h]] = (o * inv * g_ref[sl, vs[h]].astype(F32)).astype(BF16)

    sts = [st_ref[h] for h in heads]
    n_pairs = TC_GLA // pair
    cur = decay_and_kv(0)
    for i in range(n_pairs):
        nxt = decay_and_kv(i + 1) if i + 1 < n_pairs else None
        state_and_out(i, sts, cur)
        cur = nxt
    for h in heads:
        st_ref[h] = sts[h]


def _gla(qg, kg, gk, vgt, gg, batch, seq):
    ns = seq // TC_GLA
    t = batch * seq
    tok = lambda n: pl.BlockSpec((TC_GLA, n), lambda b, s: (b * ns + s, 0))
    return pl.pallas_call(
        _gla_kernel,
        grid=(batch, ns),
        in_specs=[
            tok(GLA_KEY_DIM), tok(GLA_KEY_DIM), tok(GLA_KEY_DIM),
            pl.BlockSpec((GLA_VAL_DIM, TC_GLA), lambda b, s: (0, b * ns + s)),
            tok(GLA_VAL_DIM),
        ],
        out_specs=tok(GLA_VAL_DIM),
        out_shape=jax.ShapeDtypeStruct((t, GLA_VAL_DIM), BF16),
        scratch_shapes=[pltpu.VMEM((GLA_HEADS, GLA_DV, GLA_DK), F32)],
        compiler_params=_params(("arbitrary", "arbitrary")),
        name="gla",
    )(qg, kg, gk, vgt, gg)


def _mix_kernel(x_ref, oa_ref, og_ref, gate_ref, wa_ref, wg_ref, wm_ref, o_ref):
    ya = _dot(oa_ref[...], wa_ref[...])
    yg = _dot(og_ref[...], wg_ref[...])
    merged = (gate_ref[:, :D_MODEL].astype(F32) * ya
              + gate_ref[:, D_MODEL:].astype(F32) * yg).astype(BF16)
    o_ref[...] = x_ref[...] + _dot(merged, wm_ref[...])


def _mix_out(x2d, oa, og, gates, wa, wg, wm):
    t = x2d.shape[0]
    tm = TM_PROJ
    row = lambda n: pl.BlockSpec((tm, n), lambda i: (i, 0))
    return pl.pallas_call(
        _mix_kernel,
        grid=(t // tm,),
        in_specs=[row(D_MODEL), row(oa.shape[1]), row(og.shape[1]), row(2 * D_MODEL),
                  _const_spec(wa.shape), _const_spec(wg.shape), _const_spec(wm.shape)],
        out_specs=row(D_MODEL),
        out_shape=jax.ShapeDtypeStruct((t, D_MODEL), F32),
        compiler_params=_params(("arbitrary",)),
        name="mix_out",
    )(x2d, oa, og, gates, wa, wg, wm)


def _mem_kv_kernel(m_ref, nw_ref, w_ref, k_ref, v_ref):
    mn = _rms(m_ref[0], nw_ref[...]).astype(BF16)
    kv = _dot(mn, w_ref[...])
    k = kv[:, :MEM_WIDTH].astype(BF16)
    v = kv[:, MEM_WIDTH:].astype(BF16)
    head = lax.broadcasted_iota(jnp.int32, (N_MEM, MEM_WIDTH), 1) // MEM_HEAD_DIM
    zero = jnp.zeros_like(k)
    for h in range(MEM_HEADS):
        k_ref[0, h] = jnp.where(head == h, k, zero)
        v_ref[0, h] = jnp.where(head == h, v, zero)


def _mem_kv(mem, nw, w):
    batch = mem.shape[0]
    out = jax.ShapeDtypeStruct((batch, MEM_HEADS, N_MEM, MEM_WIDTH), BF16)
    spec = pl.BlockSpec((1, MEM_HEADS, N_MEM, MEM_WIDTH), lambda b: (b, 0, 0, 0))
    return pl.pallas_call(
        _mem_kv_kernel,
        grid=(batch,),
        in_specs=[pl.BlockSpec((1, N_MEM, D_MODEL), lambda b: (b, 0, 0)),
                  _const_spec(nw.shape), _const_spec(w.shape)],
        out_specs=(spec, spec),
        out_shape=(out, out),
        compiler_params=_params(("arbitrary",)),
        name="mem_kv",
    )(mem, nw, w)


def _tail_kernel(x_ref, k_ref, v_ref, nq_ref, wq_ref, wo_ref, nf_ref, wg_ref, wu_ref, wd_ref,
                 nfin_ref, o_ref, act_ref):
    x = x_ref[...]
    hq = _rms(x, nq_ref[...]).astype(BF16)
    q = (_dot(hq, wq_ref[...]) * (MEM_HEAD_DIM ** -0.5)).astype(BF16)
    o = jnp.zeros((x.shape[0], MEM_WIDTH), F32)
    for h in range(MEM_HEADS):
        s = _dot_nt(q, k_ref[0, h])
        e = jnp.exp(s - jnp.max(s, axis=-1, keepdims=True))
        inv = 1.0 / jnp.sum(e, axis=-1, keepdims=True)
        o = o + _dot(e.astype(BF16), v_ref[0, h]) * inv
    x = x + _dot(o.astype(BF16), wo_ref[...])
    hf = _rms(x, nf_ref[...]).astype(BF16)
    for c in range(D_FF // FF_CHUNK):
        cs = slice(c * FF_CHUNK, (c + 1) * FF_CHUNK)
        g = _dot(hf, wg_ref[:, cs])
        u = _dot(hf, wu_ref[:, cs])
        act_ref[:, cs] = (g * jax.nn.sigmoid(g) * u).astype(BF16)
    x = x + _dot(act_ref[...], wd_ref[...])
    o_ref[...] = _rms(x, nfin_ref[...])


def _tail(x2d, kext, vext, nq, wq, wo, nf, wg, wu, wd, nfin, batch, seq):
    tm = TM_PROJ
    ns = seq // tm
    t = batch * seq
    row = pl.BlockSpec((tm, D_MODEL), lambda b, s: (b * ns + s, 0))
    mem_spec = pl.BlockSpec((1, MEM_HEADS, N_MEM, MEM_WIDTH), lambda b, s: (b, 0, 0, 0))
    consts = (nq, wq, wo, nf, wg, wu, wd, nfin)
    return pl.pallas_call(
        _tail_kernel,
        grid=(batch, ns),
        in_specs=[row, mem_spec, mem_spec] + [_const_spec(c.shape) for c in consts],
        out_specs=row,
        out_shape=jax.ShapeDtypeStruct((t, D_MODEL), F32),
        scratch_shapes=[pltpu.VMEM((tm, D_FF), BF16)],
        compiler_params=_params(("arbitrary", "arbitrary")),
        name="tail",
    )(x2d, kext, vext, *consts)


def _layer(x2d, mem, batch, seq, norm_mix_w, w_in, b_gate, attn_sinks, gla_gate_w2, gla_gate_b,
           gla_norm_w, w_attn_o, w_gla_o, w_mix_o, norm_mem_q_w, norm_mem_kv_w, w_mem_q,
           w_mem_kv, w_mem_o, norm_ffn_w, w_ffn_gate, w_ffn_up, w_ffn_down, out_norm_w):
    w2_p = jnp.pad(gla_gate_w2, ((0, LANES - GLA_GATE_RANK), (0, 0)))
    bf = lambda a: a.astype(BF16)
    r2 = lambda a: a.reshape(1, -1)

    qa, ka, vat, qg, kg, vgt, gg, gk, gates = _in_proj(
        x2d, r2(norm_mix_w), bf(w_in.T), bf(w2_p), r2(gla_gate_b), r2(b_gate),
        r2(jnp.tile(gla_norm_w, GLA_HEADS)))
    oa = _swa(attn_sinks, qa, ka, vat, batch, seq)
    og = _gla(qg, kg, gk, vgt, gg, batch, seq)
    x1 = _mix_out(x2d, oa, og, gates, bf(w_attn_o), bf(w_gla_o), bf(w_mix_o))
    kext, vext = _mem_kv(mem, r2(norm_mem_kv_w), bf(w_mem_kv))
    return _tail(x1, kext, vext, r2(norm_mem_q_w), bf(w_mem_q), bf(w_mem_o), r2(norm_ffn_w),
                 bf(w_ffn_gate), bf(w_ffn_up), bf(w_ffn_down), r2(out_norm_w), batch, seq)


def kernel(x, mem, norm_mix_w, w_in, b_gate, attn_sinks, gla_gate_w2, gla_gate_b, gla_norm_w,
           w_attn_o, w_gla_o, w_mix_o, norm_mem_q_w, norm_mem_kv_w, w_mem_q, w_mem_kv, w_mem_o,
           norm_ffn_w, w_ffn_gate, w_ffn_up, w_ffn_down, norm_final_w):
    batch, seq, d = x.shape
    depth = w_in.shape[0]
    assert depth == 1 and d == D_MODEL and seq % TM_PROJ == 0 and seq % TC_GLA == 0
    out = _layer(x.reshape(batch * seq, d), mem, batch, seq, norm_mix_w[0], w_in[0], b_gate[0],
                 attn_sinks[0], gla_gate_w2[0], gla_gate_b[0], gla_norm_w[0], w_attn_o[0],
                 w_gla_o[0], w_mix_o[0], norm_mem_q_w[0], norm_mem_kv_w[0], w_mem_q[0],
                 w_mem_kv[0], w_mem_o[0], norm_ffn_w[0], w_ffn_gate[0], w_ffn_up[0],
                 w_ffn_down[0], norm_final_w)
    return out.reshape(batch, seq, d)
```

```python
import functools

import jax
import jax.numpy as jnp
from jax import lax
from jax.experimental import pallas as pl
from jax.experimental.pallas import tpu as pltpu

D_MODEL = 1024
CHUNK = 64
N_MEM = 256
EPS = 1e-6

SWA_HEADS = 16
SWA_KV_HEADS = 2
SWA_HEAD_DIM = 64
SWA_BLOCK = 128

GLA_HEADS = 4
GLA_KEY_DIM = D_MODEL // 2
GLA_VAL_DIM = D_MODEL
GLA_DK = GLA_KEY_DIM // GLA_HEADS
GLA_DV = GLA_VAL_DIM // GLA_HEADS
GLA_GATE_RANK = 16
GLA_GATE_NORM = 16.0

MEM_HEADS = 4
MEM_HEAD_DIM = 64
MEM_WIDTH = MEM_HEADS * MEM_HEAD_DIM

D_FF = -(-(8 * D_MODEL) // (3 * 256)) * 256

IN_SIZES = (SWA_HEADS * SWA_HEAD_DIM, SWA_KV_HEADS * SWA_HEAD_DIM, SWA_KV_HEADS * SWA_HEAD_DIM,
            GLA_KEY_DIM, GLA_KEY_DIM, GLA_VAL_DIM, GLA_VAL_DIM, GLA_GATE_RANK, 2 * D_MODEL)
IN_OFFSETS = tuple(sum(IN_SIZES[:i]) for i in range(len(IN_SIZES) + 1))

LANES = 128
VMEM_LIMIT = 56 * 1024 * 1024

LOG2E = 1.4426950408889634

TM_PROJ = 1024
TM_ATTN = 512
MIX_CHUNK = 256
FF_CHUNK = 256

BF16 = jnp.bfloat16
F32 = jnp.float32


def _rms(x, w):
    return x * lax.rsqrt(jnp.mean(x * x, axis=-1, keepdims=True) + EPS) * w


def _dot(a, b):
    return jnp.dot(a, b, preferred_element_type=F32)


def _dot_nt(a, b):
    return lax.dot_general(a, b, (((1,), (1,)), ((), ())), preferred_element_type=F32)


def _const_spec(shape):
    zeros = (0,) * len(shape)
    return pl.BlockSpec(shape, lambda *_: zeros, pipeline_mode=pl.Buffered(1))


def _params(semantics):
    return pltpu.CompilerParams(dimension_semantics=semantics, vmem_limit_bytes=VMEM_LIMIT)


def _in_proj_kernel(x_ref, nw_ref, wt_ref, w2_ref, gb_ref, bg_ref, gn_ref,
                    qa_ref, ka_ref, vat_ref, qg_ref, kg_ref, vgt_ref, gg_ref, gk_ref, gate_ref):
    o_qa, o_ka, o_va, o_qg, o_kg, o_vg, o_gg, o_alr, o_gate, o_end = IN_OFFSETS
    hd = SWA_HEAD_DIM
    h = _rms(x_ref[...], nw_ref[...]).astype(BF16)
    qa_ref[...] = (_dot_nt(h, wt_ref[o_qa:o_ka, :]) * (hd ** -0.5 * LOG2E)).astype(BF16)
    twice = lambda o: jnp.concatenate(
        [wt_ref[o + j * hd:o + (j + 1) * hd, :] for j in range(SWA_KV_HEADS) for _ in range(2)],
        axis=0)
    ka_ref[...] = _dot_nt(h, twice(o_ka)).astype(BF16)
    vat_ref[...] = _dot_nt(twice(o_va), h).astype(BF16)
    qg_ref[...] = _dot_nt(h, wt_ref[o_qg:o_kg, :]).astype(BF16)
    kg_ref[...] = _dot_nt(h, wt_ref[o_kg:o_vg, :]).astype(BF16)
    vgt_ref[...] = _dot_nt(wt_ref[o_vg:o_gg, :], h).astype(BF16)
    g = _dot_nt(h, wt_ref[o_gg:o_alr, :])
    gg_ref[...] = (g * jax.nn.sigmoid(g) * gn_ref[...]).astype(BF16)
    a_lr = _dot_nt(h, wt_ref[o_alr:o_alr + LANES, :]).astype(BF16)
    z = _dot(a_lr, w2_ref[...]) + gb_ref[...]
    log_sig = jnp.minimum(z, 0.0) - jnp.log1p(jnp.exp(-jnp.abs(z)))
    gk_ref[...] = log_sig * (1.0 / GLA_GATE_NORM)
    gate_ref[...] = jax.nn.sigmoid(_dot_nt(h, wt_ref[o_gate:o_end, :]) + bg_ref[...]).astype(BF16)


def _in_proj(x2d, nw, wt, w2, gb, bg, gn):
    t = x2d.shape[0]
    tm = TM_PROJ
    row = lambda n: pl.BlockSpec((tm, n), lambda i: (i, 0))
    col = lambda n: pl.BlockSpec((n, tm), lambda i: (0, i))
    consts = (nw, wt, w2, gb, bg, gn)
    kw = 2 * SWA_KV_HEADS * SWA_HEAD_DIM
    tok = lambda n, dt=BF16: jax.ShapeDtypeStruct((t, n), dt)
    out_shape = (
        tok(SWA_HEADS * SWA_HEAD_DIM), tok(kw), jax.ShapeDtypeStruct((kw, t), BF16),
        tok(GLA_KEY_DIM), tok(GLA_KEY_DIM), jax.ShapeDtypeStruct((GLA_VAL_DIM, t), BF16),
        tok(GLA_VAL_DIM), tok(GLA_KEY_DIM, F32), tok(2 * D_MODEL),
    )
    out_specs = (
        row(SWA_HEADS * SWA_HEAD_DIM), row(kw), col(kw),
        row(GLA_KEY_DIM), row(GLA_KEY_DIM), col(GLA_VAL_DIM),
        row(GLA_VAL_DIM), row(GLA_KEY_DIM), row(2 * D_MODEL),
    )
    return pl.pallas_call(
        _in_proj_kernel,
        grid=(t // tm,),
        in_specs=[row(D_MODEL)] + [_const_spec(c.shape) for c in consts],
        out_specs=out_specs,
        out_shape=out_shape,
        compiler_params=_params(("arbitrary",)),
        name="in_proj",
    )(x2d, *consts)


def _swa_stages(sink_ref, q_ref, k_ref, kp_ref, vt_ref, vtp_ref, seq_start, write):
    blk = SWA_BLOCK
    win = blk + CHUNK
    low = lax.broadcasted_iota(jnp.int32, (CHUNK, LANES), 1) < SWA_HEAD_DIM
    low_row = lax.broadcasted_iota(jnp.int32, (1, LANES), 1) < SWA_HEAD_DIM
    prev_bias = jnp.where(seq_start, -1e30, 0.0).astype(F32)
    k_all = jnp.concatenate([kp_ref[...], k_ref[...]], axis=0)
    vt_all = jnp.concatenate([vtp_ref[...], vt_ref[...]], axis=1)
    zeros_chunk = jnp.zeros((CHUNK, LANES), BF16)
    pairs_per_kv = SWA_HEADS // SWA_KV_HEADS // 2
    n_pairs = SWA_HEADS // 2
    chunks = [(u, qh) for u in range(q_ref.shape[0] // blk) for qh in range(blk // CHUNK)]
    live = {}

    def scores(p):
        kj = k_all[:, (p // pairs_per_kv) * LANES:(p // pairs_per_kv + 1) * LANES]
        res = []
        for u, qh in chunks:
            r0 = u * blk + qh * CHUNK
            qp = q_ref[r0:r0 + CHUNK, p * LANES:(p + 1) * LANES]
            zero = jnp.zeros_like(qp)
            qsel = jnp.concatenate([jnp.where(low, qp, zero), jnp.where(low, zero, qp)], axis=0)
            res.append(_dot_nt(kj[r0:r0 + win], qsel))
        live["s", p] = res

    def softmax_values(p):
        sink_row = jnp.where(low_row, sink_ref[2 * p], sink_ref[2 * p + 1]) * LOG2E
        vtj = vt_all[(p // pairs_per_kv) * LANES:(p // pairs_per_kv + 1) * LANES, :]
        res = []
        for (u, qh), st in zip(chunks, live.pop(("s", p))):
            if u == 0:
                n_prev = blk - qh * CHUNK
                st = jnp.concatenate([st[:n_prev] + prev_bias, st[n_prev:]], axis=0)
            m = jnp.maximum(jnp.max(st, axis=0, keepdims=True), sink_row)
            e = jnp.exp2(st - m)
            denom = jnp.sum(e, axis=0, keepdims=True) + jnp.exp2(sink_row - m)
            e_full = jnp.concatenate([zeros_chunk] * qh + [e.astype(BF16)]
                                     + [zeros_chunk] * (1 - qh), axis=0)
            res.append(_dot(vtj[:, u * blk:(u + 2) * blk], e_full) * (1.0 / denom))
        live["o", p] = res

    def store(p):
        for (u, qh), ot in zip(chunks, live.pop(("o", p))):
            r0 = u * blk + qh * CHUNK
            o2 = ot.T
            write(slice(r0, r0 + CHUNK), slice(p * LANES, (p + 1) * LANES),
                  jnp.where(low, o2[:CHUNK], o2[CHUNK:]).astype(BF16))

    stages = [functools.partial(scores, 0)]
    for p in range(n_pairs):
        if p + 1 < n_pairs:
            stages.append(functools.partial(scores, p + 1))
        stages.append(functools.partial(softmax_values, p))
        if p > 0:
            stages.append(functools.partial(store, p - 1))
    stages.append(functools.partial(store, n_pairs - 1))
    return stages


def _gla_stages(q_ref, k_ref, gk_ref, vt_ref, g_ref, st_ref, seq_start, write):
    pair = 2 * CHUNK
    heads = range(GLA_HEADS)
    ks = [slice(h * GLA_DK, (h + 1) * GLA_DK) for h in heads]
    vs = [slice(h * GLA_DV, (h + 1) * GLA_DV) for h in heads]
    r = lax.broadcasted_iota(jnp.int32, (pair, pair), 0)
    c = lax.broadcasted_iota(jnp.int32, (pair, pair), 1)
    tri = ((c <= r) & ((r // CHUNK) == (c // CHUNK))).astype(BF16)
    tri2 = jnp.concatenate([tri, tri], axis=1)
    first = lax.broadcasted_iota(jnp.int32, (pair, GLA_DK), 0) < CHUNK
    eps_scaled = EPS * GLA_DK
    n_pairs = q_ref.shape[0] // pair
    keep = jnp.where(seq_start, 0.0, 1.0).astype(F32)
    sts = [st_ref[h] * keep for h in heads]
    live = {}

    def decay_and_kv(i):
        sl = slice(i * pair, (i + 1) * pair)
        gk = gk_ref[sl, :]
        hi = gk.astype(BF16)
        lo = (gk - hi.astype(F32)).astype(BF16)
        b_all = _dot(tri2, jnp.concatenate([hi, lo], axis=0))
        res = []
        for h in heads:
            b = b_all[:, ks[h]]
            b_end0 = b[CHUNK - 1:CHUNK, :]
            b_end1 = b[pair - 1:pair, :]
            k_dec = (k_ref[sl, ks[h]].astype(F32)
                     * jnp.exp(jnp.where(first, b_end0, b_end1) - b)).astype(BF16)
            zero = jnp.zeros_like(k_dec)
            kd = jnp.concatenate([jnp.where(first, k_dec, zero), jnp.where(first, zero, k_dec)],
                                 axis=1)
            kv = _dot(vt_ref[vs[h], sl], kd)
            res.append((jnp.exp(b_end0), jnp.exp(b_end1), kv))
        live[i] = res

    def state_and_out(i):
        sl = slice(i * pair, (i + 1) * pair)
        dk = live.pop(i)
        outs = []
        for h in heads:
            a0, a1, kv = dk[h]
            st0 = sts[h] * a0 + kv[:, :GLA_DK]
            st1 = st0 * a1 + kv[:, GLA_DK:]
            sts[h] = st1
            q = q_ref[sl, ks[h]]
            outs.append(jnp.concatenate([_dot_nt(q[:CHUNK], st0.astype(BF16)),
                                         _dot_nt(q[CHUNK:], st1.astype(BF16))], axis=0))
        for h in heads:
            o = outs[h]
            inv = lax.rsqrt(jnp.mean(o * o, axis=-1, keepdims=True) + eps_scaled)
            write(sl, vs[h], (o * inv * g_ref[sl, vs[h]].astype(F32)).astype(BF16))

    def save_state():
        for h in heads:
            st_ref[h] = sts[h]

    stages = [functools.partial(decay_and_kv, 0)]
    for i in range(n_pairs):
        if i + 1 < n_pairs:
            stages.append(functools.partial(decay_and_kv, i + 1))
        stages.append(functools.partial(state_and_out, i))
    stages.append(save_state)
    return stages


def _mix_stages(x_ref, gate_ref, wa_ref, wg_ref, wm_ref, oa_ref, og_ref, mg_ref, o_ref):
    pieces = [slice(c * MIX_CHUNK, (c + 1) * MIX_CHUNK) for c in range(D_MODEL // MIX_CHUNK)]

    def merge(cs):
        ya = _dot(oa_ref[...], wa_ref[:, cs])
        yg = _dot(og_ref[...], wg_ref[:, cs])
        cs_b = slice(D_MODEL + cs.start, D_MODEL + cs.stop)
        mg_ref[:, cs] = (gate_ref[:, cs].astype(F32) * ya
                         + gate_ref[:, cs_b].astype(F32) * yg).astype(BF16)

    def project(cs):
        o_ref[:, cs] = x_ref[:, cs] + _dot(mg_ref[...], wm_ref[:, cs])

    return ([functools.partial(merge, cs) for cs in pieces]
            + [functools.partial(project, cs) for cs in pieces])


def _interleave(major, minor):
    out, done = [], 0
    for n, stage in enumerate(major):
        out.append(stage)
        want = (n + 1) * len(minor) // len(major)
        out.extend(minor[done:want])
        done = want
    return out


def _attn_mix_kernel(tiles_per_seq,
                     sink_ref, q_ref, k_ref, kp_ref, vt_ref, vtp_ref,
                     qg_ref, kg_ref, gk_ref, vgt_ref, gg_ref,
                     x_ref, gate_ref, wa_ref, wg_ref, wm_ref,
                     o_ref, oa_scr, og_scr, mg_scr, st_ref):
    i = pl.program_id(0)
    n_tiles = pl.num_programs(0) - 1

    @pl.when(i == 0)
    def _():
        oa_scr[...] = jnp.zeros_like(oa_scr)
        og_scr[...] = jnp.zeros_like(og_scr)
        st_ref[...] = jnp.zeros_like(st_ref)

    seq_start = (jnp.minimum(i, n_tiles - 1) % tiles_per_seq) == 0
    slot_w = i % 2
    slot_r = 1 - slot_w

    def write_oa(rows, cols, val):
        oa_scr[slot_w, rows, cols] = val

    def write_og(rows, cols, val):
        og_scr[slot_w, rows, cols] = val

    swa = _swa_stages(sink_ref, q_ref, k_ref, kp_ref, vt_ref, vtp_ref, seq_start, write_oa)
    gla = _gla_stages(qg_ref, kg_ref, gk_ref, vgt_ref, gg_ref, st_ref, seq_start, write_og)
    mix = _mix_stages(x_ref, gate_ref, wa_ref, wg_ref, wm_ref,
                      oa_scr.at[slot_r], og_scr.at[slot_r], mg_scr, o_ref)
    for stage in _interleave(_interleave(swa, gla), mix):
        stage()


def _attn_mix(sinks, qa, ka, vat, qg, kg, gk, vgt, gg, x2d, gates, wa, wg, wm, seq):
    t = x2d.shape[0]
    tm = TM_ATTN
    n_tiles = t // tm
    blocks = tm // SWA_BLOCK
    kw = ka.shape[1]
    cur = lambda i: jnp.minimum(i, n_tiles - 1)
    prv = lambda i: jnp.maximum(i - 1, 0)
    prev_block = lambda i: jnp.maximum(cur(i) * blocks - 1, 0)
    tok = lambda n: pl.BlockSpec((tm, n), lambda i: (cur(i), 0))
    col = lambda n: pl.BlockSpec((n, tm), lambda i: (0, cur(i)))
    old = lambda n: pl.BlockSpec((tm, n), lambda i: (prv(i), 0))
    return pl.pallas_call(
        functools.partial(_attn_mix_kernel, seq // tm),
        grid=(n_tiles + 1,),
        in_specs=[
            pl.BlockSpec(memory_space=pltpu.SMEM),
            tok(qa.shape[1]), tok(kw),
            pl.BlockSpec((SWA_BLOCK, kw), lambda i: (prev_block(i), 0)),
            col(kw),
            pl.BlockSpec((kw, SWA_BLOCK), lambda i: (0, prev_block(i))),
            tok(GLA_KEY_DIM), tok(GLA_KEY_DIM), tok(GLA_KEY_DIM), col(GLA_VAL_DIM),
            tok(GLA_VAL_DIM),
            old(D_MODEL), old(2 * D_MODEL),
            _const_spec(wa.shape), _const_spec(wg.shape), _const_spec(wm.shape),
        ],
        out_specs=old(D_MODEL),
        out_shape=jax.ShapeDtypeStruct((t, D_MODEL), F32),
        scratch_shapes=[
            pltpu.VMEM((2, tm, SWA_HEADS * SWA_HEAD_DIM), BF16),
            pltpu.VMEM((2, tm, GLA_VAL_DIM), BF16),
            pltpu.VMEM((tm, D_MODEL), BF16),
            pltpu.VMEM((GLA_HEADS, GLA_DV, GLA_DK), F32),
        ],
        compiler_params=_params(("arbitrary",)),
        name="attn_mix",
    )(sinks, qa, ka, ka, vat, vat, qg, kg, gk, vgt, gg, x2d, gates, wa, wg, wm)


def _mem_kv_kernel(m_ref, nw_ref, w_ref, k_ref, v_ref):
    mn = _rms(m_ref[0], nw_ref[...]).astype(BF16)
    kv = _dot(mn, w_ref[...])
    k = kv[:, :MEM_WIDTH].astype(BF16)
    v = kv[:, MEM_WIDTH:].astype(BF16)
    head = lax.broadcasted_iota(jnp.int32, (N_MEM, MEM_WIDTH), 1) // MEM_HEAD_DIM
    zero = jnp.zeros_like(k)
    for h in range(MEM_HEADS):
        k_ref[0, h] = jnp.where(head == h, k, zero)
        v_ref[0, h] = jnp.where(head == h, v, zero)


def _mem_kv(mem, nw, w):
    batch = mem.shape[0]
    out = jax.ShapeDtypeStruct((batch, MEM_HEADS, N_MEM, MEM_WIDTH), BF16)
    spec = pl.BlockSpec((1, MEM_HEADS, N_MEM, MEM_WIDTH), lambda b: (b, 0, 0, 0))
    return pl.pallas_call(
        _mem_kv_kernel,
        grid=(batch,),
        in_specs=[pl.BlockSpec((1, N_MEM, D_MODEL), lambda b: (b, 0, 0)),
                  _const_spec(nw.shape), _const_spec(w.shape)],
        out_specs=(spec, spec),
        out_shape=(out, out),
        compiler_params=_params(("arbitrary",)),
        name="mem_kv",
    )(mem, nw, w)


def _tail_kernel(x_ref, k_ref, v_ref, nq_ref, wq_ref, wo_ref, nf_ref, wg_ref, wu_ref, wd_ref,
                 nfin_ref, o_ref, act_ref):
    x = x_ref[...]
    hq = _rms(x, nq_ref[...]).astype(BF16)
    q = (_dot(hq, wq_ref[...]) * (MEM_HEAD_DIM ** -0.5)).astype(BF16)
    o = jnp.zeros((x.shape[0], MEM_WIDTH), F32)
    for h in range(MEM_HEADS):
        s = _dot_nt(q, k_ref[0, h])
        e = jnp.exp(s - jnp.max(s, axis=-1, keepdims=True))
        inv = 1.0 / jnp.sum(e, axis=-1, keepdims=True)
        o = o + _dot(e.astype(BF16), v_ref[0, h]) * inv
    x = x + _dot(o.astype(BF16), wo_ref[...])
    hf = _rms(x, nf_ref[...]).astype(BF16)
    for c in range(D_FF // FF_CHUNK):
        cs = slice(c * FF_CHUNK, (c + 1) * FF_CHUNK)
        g = _dot(hf, wg_ref[:, cs])
        u = _dot(hf, wu_ref[:, cs])
        act_ref[:, cs] = (g * jax.nn.sigmoid(g) * u).astype(BF16)
    x = x + _dot(act_ref[...], wd_ref[...])
    o_ref[...] = _rms(x, nfin_ref[...])


def _tail(x2d, kext, vext, nq, wq, wo, nf, wg, wu, wd, nfin, batch, seq):
    tm = TM_PROJ
    ns = seq // tm
    t = batch * seq
    row = pl.BlockSpec((tm, D_MODEL), lambda b, s: (b * ns + s, 0))
    mem_spec = pl.BlockSpec((1, MEM_HEADS, N_MEM, MEM_WIDTH), lambda b, s: (b, 0, 0, 0))
    consts = (nq, wq, wo, nf, wg, wu, wd, nfin)
    return pl.pallas_call(
        _tail_kernel,
        grid=(batch, ns),
        in_specs=[row, mem_spec, mem_spec] + [_const_spec(c.shape) for c in consts],
        out_specs=row,
        out_shape=jax.ShapeDtypeStruct((t, D_MODEL), F32),
        scratch_shapes=[pltpu.VMEM((tm, D_FF), BF16)],
        compiler_params=_params(("arbitrary", "arbitrary")),
        name="tail",
    )(x2d, kext, vext, *consts)


def _layer(x2d, mem, batch, seq, norm_mix_w, w_in, b_gate, attn_sinks, gla_gate_w2, gla_gate_b,
           gla_norm_w, w_attn_o, w_gla_o, w_mix_o, norm_mem_q_w, norm_mem_kv_w, w_mem_q,
           w_mem_kv, w_mem_o, norm_ffn_w, w_ffn_gate, w_ffn_up, w_ffn_down, out_norm_w):
    w2_p = jnp.pad(gla_gate_w2, ((0, LANES - GLA_GATE_RANK), (0, 0)))
    bf = lambda a: a.astype(BF16)
    r2 = lambda a: a.reshape(1, -1)

    qa, ka, vat, qg, kg, vgt, gg, gk, gates = _in_proj(
        x2d, r2(norm_mix_w), bf(w_in.T), bf(w2_p), r2(gla_gate_b), r2(b_gate),
        r2(jnp.tile(gla_norm_w, GLA_HEADS)))
    x1 = _attn_mix(attn_sinks, qa, ka, vat, qg, kg, gk, vgt, gg, x2d, gates,
                   bf(w_attn_o), bf(w_gla_o), bf(w_mix_o), seq)
    kext, vext = _mem_kv(mem, r2(norm_mem_kv_w), bf(w_mem_kv))
    return _tail(x1, kext, vext, r2(norm_mem_q_w), bf(w_mem_q), bf(w_mem_o), r2(norm_ffn_w),
                 bf(w_ffn_gate), bf(w_ffn_up), bf(w_ffn_down), r2(out_norm_w), batch, seq)


def kernel(x, mem, norm_mix_w, w_in, b_gate, attn_sinks, gla_gate_w2, gla_gate_b, gla_norm_w,
           w_attn_o, w_gla_o, w_mix_o, norm_mem_q_w, norm_mem_kv_w, w_mem_q, w_mem_kv, w_mem_o,
           norm_ffn_w, w_ffn_gate, w_ffn_up, w_ffn_down, norm_final_w):
    batch, seq, d = x.shape
    depth = w_in.shape[0]
    assert depth == 1 and d == D_MODEL and seq % TM_PROJ == 0 and seq % TM_ATTN == 0
    out = _layer(x.reshape(batch * seq, d), mem, batch, seq, norm_mix_w[0], w_in[0], b_gate[0],
                 attn_sinks[0], gla_gate_w2[0], gla_gate_b[0], gla_norm_w[0], w_attn_o[0],
                 w_gla_o[0], w_mix_o[0], norm_mem_q_w[0], norm_mem_kv_w[0], w_mem_q[0],
                 w_mem_kv[0], w_mem_o[0], norm_ffn_w[0], w_ffn_gate[0], w_ffn_up[0],
                 w_ffn_down[0], norm_final_w)
    return out.reshape(batch, seq, d)
```

```python
import functools

import jax
import jax.numpy as jnp
from jax import lax
from jax.experimental import pallas as pl
from jax.experimental.pallas import tpu as pltpu

D_MODEL = 1024
CHUNK = 64
N_MEM = 256
EPS = 1e-6

SWA_HEADS = 16
SWA_KV_HEADS = 2
SWA_HEAD_DIM = 64
SWA_BLOCK = 128

GLA_HEADS = 4
GLA_KEY_DIM = D_MODEL // 2
GLA_VAL_DIM = D_MODEL
GLA_DK = GLA_KEY_DIM // GLA_HEADS
GLA_DV = GLA_VAL_DIM // GLA_HEADS
GLA_GATE_RANK = 16
GLA_GATE_NORM = 16.0

MEM_HEADS = 4
MEM_HEAD_DIM = 64
MEM_WIDTH = MEM_HEADS * MEM_HEAD_DIM

D_FF = -(-(8 * D_MODEL) // (3 * 256)) * 256

IN_SIZES = (SWA_HEADS * SWA_HEAD_DIM, SWA_KV_HEADS * SWA_HEAD_DIM, SWA_KV_HEADS * SWA_HEAD_DIM,
            GLA_KEY_DIM, GLA_KEY_DIM, GLA_VAL_DIM, GLA_VAL_DIM, GLA_GATE_RANK, 2 * D_MODEL)
IN_OFFSETS = tuple(sum(IN_SIZES[:i]) for i in range(len(IN_SIZES) + 1))

LANES = 128
VMEM_LIMIT = 56 * 1024 * 1024

LOG2E = 1.4426950408889634

TM_PROJ = 1024
TM_ATTN = 512
TM_TAIL = 512
MIX_CHUNK = 256
FF_CHUNK = 256

BF16 = jnp.bfloat16
F32 = jnp.float32


def _rms(x, w):
    return x * lax.rsqrt(jnp.mean(x * x, axis=-1, keepdims=True) + EPS) * w


def _dot(a, b):
    return jnp.dot(a, b, preferred_element_type=F32)


def _dot_nt(a, b):
    return lax.dot_general(a, b, (((1,), (1,)), ((), ())), preferred_element_type=F32)


def _const_spec(shape):
    zeros = (0,) * len(shape)
    return pl.BlockSpec(shape, lambda *_: zeros, pipeline_mode=pl.Buffered(1))


def _params(semantics):
    return pltpu.CompilerParams(dimension_semantics=semantics, vmem_limit_bytes=VMEM_LIMIT)


def _in_proj_kernel(x_ref, nw_ref, wt_ref, w2_ref, gb_ref, bg_ref, gn_ref,
                    qa_ref, ka_ref, vat_ref, qg_ref, kg_ref, vgt_ref, gg_ref, gk_ref, gate_ref):
    o_qa, o_ka, o_va, o_qg, o_kg, o_vg, o_gg, o_alr, o_gate, o_end = IN_OFFSETS
    hd = SWA_HEAD_DIM
    h = _rms(x_ref[...], nw_ref[...]).astype(BF16)
    qa_ref[...] = (_dot_nt(h, wt_ref[o_qa:o_ka, :]) * (hd ** -0.5 * LOG2E)).astype(BF16)
    twice = lambda o: jnp.concatenate(
        [wt_ref[o + j * hd:o + (j + 1) * hd, :] for j in range(SWA_KV_HEADS) for _ in range(2)],
        axis=0)
    ka_ref[...] = _dot_nt(h, twice(o_ka)).astype(BF16)
    vat_ref[...] = _dot_nt(twice(o_va), h).astype(BF16)
    qg_ref[...] = _dot_nt(h, wt_ref[o_qg:o_kg, :]).astype(BF16)
    kg_ref[...] = _dot_nt(h, wt_ref[o_kg:o_vg, :]).astype(BF16)
    vgt_ref[...] = _dot_nt(wt_ref[o_vg:o_gg, :], h).astype(BF16)
    g = _dot_nt(h, wt_ref[o_gg:o_alr, :])
    gg_ref[...] = (g * jax.nn.sigmoid(g) * gn_ref[...]).astype(BF16)
    a_lr = _dot_nt(h, wt_ref[o_alr:o_alr + LANES, :]).astype(BF16)
    z = _dot(a_lr, w2_ref[...]) + gb_ref[...]
    log_sig = jnp.minimum(z, 0.0) - jnp.log1p(jnp.exp(-jnp.abs(z)))
    gk_ref[...] = log_sig * (1.0 / GLA_GATE_NORM)
    gate_ref[...] = jax.nn.sigmoid(_dot_nt(h, wt_ref[o_gate:o_end, :]) + bg_ref[...]).astype(BF16)


def _in_proj(x2d, nw, wt, w2, gb, bg, gn):
    t = x2d.shape[0]
    tm = TM_PROJ
    row = lambda n: pl.BlockSpec((tm, n), lambda i: (i, 0))
    col = lambda n: pl.BlockSpec((n, tm), lambda i: (0, i))
    consts = (nw, wt, w2, gb, bg, gn)
    kw = 2 * SWA_KV_HEADS * SWA_HEAD_DIM
    tok = lambda n, dt=BF16: jax.ShapeDtypeStruct((t, n), dt)
    out_shape = (
        tok(SWA_HEADS * SWA_HEAD_DIM), tok(kw), jax.ShapeDtypeStruct((kw, t), BF16),
        tok(GLA_KEY_DIM), tok(GLA_KEY_DIM), jax.ShapeDtypeStruct((GLA_VAL_DIM, t), BF16),
        tok(GLA_VAL_DIM), tok(GLA_KEY_DIM, F32), tok(2 * D_MODEL),
    )
    out_specs = (
        row(SWA_HEADS * SWA_HEAD_DIM), row(kw), col(kw),
        row(GLA_KEY_DIM), row(GLA_KEY_DIM), col(GLA_VAL_DIM),
        row(GLA_VAL_DIM), row(GLA_KEY_DIM), row(2 * D_MODEL),
    )
    return pl.pallas_call(
        _in_proj_kernel,
        grid=(t // tm,),
        in_specs=[row(D_MODEL)] + [_const_spec(c.shape) for c in consts],
        out_specs=out_specs,
        out_shape=out_shape,
        compiler_params=_params(("arbitrary",)),
        name="in_proj",
    )(x2d, *consts)


def _swa_stages(sink_ref, q_ref, k_ref, kp_ref, vt_ref, vtp_ref, seq_start, write):
    blk = SWA_BLOCK
    win = blk + CHUNK
    low = lax.broadcasted_iota(jnp.int32, (CHUNK, LANES), 1) < SWA_HEAD_DIM
    low_row = lax.broadcasted_iota(jnp.int32, (1, LANES), 1) < SWA_HEAD_DIM
    prev_bias = jnp.where(seq_start, -1e30, 0.0).astype(F32)
    k_all = jnp.concatenate([kp_ref[...], k_ref[...]], axis=0)
    vt_all = jnp.concatenate([vtp_ref[...], vt_ref[...]], axis=1)
    zeros_chunk = jnp.zeros((CHUNK, LANES), BF16)
    pairs_per_kv = SWA_HEADS // SWA_KV_HEADS // 2
    n_pairs = SWA_HEADS // 2
    chunks = [(u, qh) for u in range(q_ref.shape[0] // blk) for qh in range(blk // CHUNK)]
    live = {}

    def scores(p):
        kj = k_all[:, (p // pairs_per_kv) * LANES:(p // pairs_per_kv + 1) * LANES]
        res = []
        for u, qh in chunks:
            r0 = u * blk + qh * CHUNK
            qp = q_ref[r0:r0 + CHUNK, p * LANES:(p + 1) * LANES]
            zero = jnp.zeros_like(qp)
            qsel = jnp.concatenate([jnp.where(low, qp, zero), jnp.where(low, zero, qp)], axis=0)
            res.append(_dot_nt(kj[r0:r0 + win], qsel))
        live["s", p] = res

    def softmax_values(p):
        sink_row = jnp.where(low_row, sink_ref[2 * p], sink_ref[2 * p + 1]) * LOG2E
        vtj = vt_all[(p // pairs_per_kv) * LANES:(p // pairs_per_kv + 1) * LANES, :]
        res = []
        for (u, qh), st in zip(chunks, live.pop(("s", p))):
            if u == 0:
                n_prev = blk - qh * CHUNK
                st = jnp.concatenate([st[:n_prev] + prev_bias, st[n_prev:]], axis=0)
            m = jnp.maximum(jnp.max(st, axis=0, keepdims=True), sink_row)
            e = jnp.exp2(st - m)
            denom = jnp.sum(e, axis=0, keepdims=True) + jnp.exp2(sink_row - m)
            e_full = jnp.concatenate([zeros_chunk] * qh + [e.astype(BF16)]
                                     + [zeros_chunk] * (1 - qh), axis=0)
            res.append(_dot(vtj[:, u * blk:(u + 2) * blk], e_full) * (1.0 / denom))
        live["o", p] = res

    def store(p):
        for (u, qh), ot in zip(chunks, live.pop(("o", p))):
            r0 = u * blk + qh * CHUNK
            o2 = ot.T
            write(slice(r0, r0 + CHUNK), slice(p * LANES, (p + 1) * LANES),
                  jnp.where(low, o2[:CHUNK], o2[CHUNK:]).astype(BF16))

    stages = [functools.partial(scores, 0)]
    for p in range(n_pairs):
        if p + 1 < n_pairs:
            stages.append(functools.partial(scores, p + 1))
        stages.append(functools.partial(softmax_values, p))
        if p > 0:
            stages.append(functools.partial(store, p - 1))
    stages.append(functools.partial(store, n_pairs - 1))
    return stages


def _gla_stages(q_ref, k_ref, gk_ref, vt_ref, g_ref, st_ref, seq_start, write):
    pair = 2 * CHUNK
    heads = range(GLA_HEADS)
    ks = [slice(h * GLA_DK, (h + 1) * GLA_DK) for h in heads]
    vs = [slice(h * GLA_DV, (h + 1) * GLA_DV) for h in heads]
    r = lax.broadcasted_iota(jnp.int32, (pair, pair), 0)
    c = lax.broadcasted_iota(jnp.int32, (pair, pair), 1)
    tri = ((c <= r) & ((r // CHUNK) == (c // CHUNK))).astype(BF16)
    tri2 = jnp.concatenate([tri, tri], axis=1)
    first = lax.broadcasted_iota(jnp.int32, (pair, GLA_DK), 0) < CHUNK
    eps_scaled = EPS * GLA_DK
    n_pairs = q_ref.shape[0] // pair
    sts = [jnp.where(seq_start, jnp.zeros((GLA_DV, GLA_DK), F32), st_ref[h]) for h in heads]
    live = {}

    def decay_and_kv(i):
        sl = slice(i * pair, (i + 1) * pair)
        gk = gk_ref[sl, :]
        hi = gk.astype(BF16)
        lo = (gk - hi.astype(F32)).astype(BF16)
        b_all = _dot(tri2, jnp.concatenate([hi, lo], axis=0))
        res = []
        for h in heads:
            b = b_all[:, ks[h]]
            b_end0 = b[CHUNK - 1:CHUNK, :]
            b_end1 = b[pair - 1:pair, :]
            k_dec = (k_ref[sl, ks[h]].astype(F32)
                     * jnp.exp(jnp.where(first, b_end0, b_end1) - b)).astype(BF16)
            zero = jnp.zeros_like(k_dec)
            kd = jnp.concatenate([jnp.where(first, k_dec, zero), jnp.where(first, zero, k_dec)],
                                 axis=1)
            kv = _dot(vt_ref[vs[h], sl], kd)
            res.append((jnp.exp(b_end0), jnp.exp(b_end1), kv))
        live[i] = res

    def state_and_out(i):
        sl = slice(i * pair, (i + 1) * pair)
        dk = live.pop(i)
        outs = []
        for h in heads:
            a0, a1, kv = dk[h]
            st0 = sts[h] * a0 + kv[:, :GLA_DK]
            st1 = st0 * a1 + kv[:, GLA_DK:]
            sts[h] = st1
            q = q_ref[sl, ks[h]]
            outs.append(jnp.concatenate([_dot_nt(q[:CHUNK], st0.astype(BF16)),
                                         _dot_nt(q[CHUNK:], st1.astype(BF16))], axis=0))
        for h in heads:
            o = outs[h]
            inv = lax.rsqrt(jnp.mean(o * o, axis=-1, keepdims=True) + eps_scaled)
            write(sl, vs[h], (o * inv * g_ref[sl, vs[h]].astype(F32)).astype(BF16))

    def save_state():
        for h in heads:
            st_ref[h] = sts[h]

    stages = [functools.partial(decay_and_kv, 0)]
    for i in range(n_pairs):
        if i + 1 < n_pairs:
            stages.append(functools.partial(decay_and_kv, i + 1))
        stages.append(functools.partial(state_and_out, i))
    stages.append(save_state)
    return stages


def _mix_stages(x_ref, gate_ref, wa_ref, wg_ref, wm_ref, oa_ref, og_ref, mg_ref, o_ref):
    pieces = [slice(c * MIX_CHUNK, (c + 1) * MIX_CHUNK) for c in range(D_MODEL // MIX_CHUNK)]

    def merge(cs):
        ya = _dot(oa_ref[...], wa_ref[:, cs])
        yg = _dot(og_ref[...], wg_ref[:, cs])
        cs_b = slice(D_MODEL + cs.start, D_MODEL + cs.stop)
        mg_ref[:, cs] = (gate_ref[:, cs].astype(F32) * ya
                         + gate_ref[:, cs_b].astype(F32) * yg).astype(BF16)

    def project(cs):
        o_ref[:, cs] = x_ref[:, cs] + _dot(mg_ref[...], wm_ref[:, cs])

    return ([functools.partial(merge, cs) for cs in pieces]
            + [functools.partial(project, cs) for cs in pieces])


def _interleave(major, minor):
    out, done = [], 0
    for n, stage in enumerate(major):
        out.append(stage)
        want = (n + 1) * len(minor) // len(major)
        out.extend(minor[done:want])
        done = want
    return out


def _run_lookahead(i, n_tiles, lead, lag, merge, init=None):
    @pl.when(i == 0)
    def _():
        if init is not None:
            init()
        for stage in lead():
            stage()

    @pl.when(jnp.logical_and(i > 0, i < n_tiles))
    def _():
        for stage in merge(lead(), lag()):
            stage()

    @pl.when(i == n_tiles)
    def _():
        for stage in lag():
            stage()


def _attn_mix_kernel(tiles_per_seq,
                     sink_ref, q_ref, k_ref, kp_ref, vt_ref, vtp_ref,
                     qg_ref, kg_ref, gk_ref, vgt_ref, gg_ref,
                     x_ref, gate_ref, wa_ref, wg_ref, wm_ref,
                     o_ref, oa_scr, og_scr, mg_scr, st_ref):
    i = pl.program_id(0)
    n_tiles = pl.num_programs(0) - 1
    seq_start = (i % tiles_per_seq) == 0
    slot_w = i % 2
    slot_r = 1 - slot_w

    def write_oa(rows, cols, val):
        oa_scr[slot_w, rows, cols] = val

    def write_og(rows, cols, val):
        og_scr[slot_w, rows, cols] = val

    def mixers():
        swa = _swa_stages(sink_ref, q_ref, k_ref, kp_ref, vt_ref, vtp_ref, seq_start, write_oa)
        gla = _gla_stages(qg_ref, kg_ref, gk_ref, vgt_ref, gg_ref, st_ref, seq_start, write_og)
        return _interleave(swa, gla)

    def projections():
        return _mix_stages(x_ref, gate_ref, wa_ref, wg_ref, wm_ref,
                           oa_scr.at[slot_r], og_scr.at[slot_r], mg_scr, o_ref)

    def init():
        st_ref[...] = jnp.zeros_like(st_ref)

    _run_lookahead(i, n_tiles, mixers, projections, _interleave, init)


def _attn_mix(sinks, qa, ka, vat, qg, kg, gk, vgt, gg, x2d, gates, wa, wg, wm, seq):
    t = x2d.shape[0]
    tm = TM_ATTN
    n_tiles = t // tm
    blocks = tm // SWA_BLOCK
    kw = ka.shape[1]
    cur = lambda i: jnp.minimum(i, n_tiles - 1)
    prv = lambda i: jnp.maximum(i - 1, 0)
    prev_block = lambda i: jnp.maximum(cur(i) * blocks - 1, 0)
    tok = lambda n: pl.BlockSpec((tm, n), lambda i: (cur(i), 0))
    col = lambda n: pl.BlockSpec((n, tm), lambda i: (0, cur(i)))
    old = lambda n: pl.BlockSpec((tm, n), lambda i: (prv(i), 0))
    return pl.pallas_call(
        functools.partial(_attn_mix_kernel, seq // tm),
        grid=(n_tiles + 1,),
        in_specs=[
            pl.BlockSpec(memory_space=pltpu.SMEM),
            tok(qa.shape[1]), tok(kw),
            pl.BlockSpec((SWA_BLOCK, kw), lambda i: (prev_block(i), 0)),
            col(kw),
            pl.BlockSpec((kw, SWA_BLOCK), lambda i: (0, prev_block(i))),
            tok(GLA_KEY_DIM), tok(GLA_KEY_DIM), tok(GLA_KEY_DIM), col(GLA_VAL_DIM),
            tok(GLA_VAL_DIM),
            old(D_MODEL), old(2 * D_MODEL),
            _const_spec(wa.shape), _const_spec(wg.shape), _const_spec(wm.shape),
        ],
        out_specs=old(D_MODEL),
        out_shape=jax.ShapeDtypeStruct((t, D_MODEL), F32),
        scratch_shapes=[
            pltpu.VMEM((2, tm, SWA_HEADS * SWA_HEAD_DIM), BF16),
            pltpu.VMEM((2, tm, GLA_VAL_DIM), BF16),
            pltpu.VMEM((tm, D_MODEL), BF16),
            pltpu.VMEM((GLA_HEADS, GLA_DV, GLA_DK), F32),
        ],
        compiler_params=_params(("arbitrary",)),
        name="attn_mix",
    )(sinks, qa, ka, ka, vat, vat, qg, kg, gk, vgt, gg, x2d, gates, wa, wg, wm)


def _mem_kv_kernel(m_ref, nw_ref, w_ref, k_ref, v_ref):
    mn = _rms(m_ref[0], nw_ref[...]).astype(BF16)
    kv = _dot(mn, w_ref[...])
    k = kv[:, :MEM_WIDTH].astype(BF16)
    v = kv[:, MEM_WIDTH:].astype(BF16)
    head = lax.broadcasted_iota(jnp.int32, (N_MEM, MEM_WIDTH), 1) // MEM_HEAD_DIM
    zero = jnp.zeros_like(k)
    for h in range(MEM_HEADS):
        k_ref[0, h] = jnp.where(head == h, k, zero)
        v_ref[0, h] = jnp.where(head == h, v, zero)


def _mem_kv(mem, nw, w):
    batch = mem.shape[0]
    out = jax.ShapeDtypeStruct((batch, MEM_HEADS, N_MEM, MEM_WIDTH), BF16)
    spec = pl.BlockSpec((1, MEM_HEADS, N_MEM, MEM_WIDTH), lambda b: (b, 0, 0, 0))
    return pl.pallas_call(
        _mem_kv_kernel,
        grid=(batch,),
        in_specs=[pl.BlockSpec((1, N_MEM, D_MODEL), lambda b: (b, 0, 0)),
                  _const_spec(nw.shape), _const_spec(w.shape)],
        out_specs=(spec, spec),
        out_shape=(out, out),
        compiler_params=_params(("arbitrary",)),
        name="mem_kv",
    )(mem, nw, w)


def _xattn_stages(x_ref, k_ref, v_ref, nq_ref, wq_ref, wo_ref, nf_ref, x2_ref, hf_ref):
    live = {}

    def queries():
        hq = _rms(x_ref[...], nq_ref[...]).astype(BF16)
        live["q"] = (_dot(hq, wq_ref[...]) * (MEM_HEAD_DIM ** -0.5)).astype(BF16)
        live["o"] = jnp.zeros((x_ref.shape[0], MEM_WIDTH), F32)

    def head(h):
        s = _dot_nt(live["q"], k_ref[0, h])
        e = jnp.exp(s - jnp.max(s, axis=-1, keepdims=True))
        inv = 1.0 / jnp.sum(e, axis=-1, keepdims=True)
        live["o"] = live["o"] + _dot(e.astype(BF16), v_ref[0, h]) * inv

    def project():
        x2 = x_ref[...] + _dot(live.pop("o").astype(BF16), wo_ref[...])
        x2_ref[...] = x2
        hf_ref[...] = _rms(x2, nf_ref[...]).astype(BF16)

    return [queries] + [functools.partial(head, h) for h in range(MEM_HEADS)] + [project]


def _ffn_stages(x2_ref, hf_ref, wg_ref, wu_ref, wd_ref, nfin_ref, act_ref, o_ref):
    live = {"ss": jnp.zeros((o_ref.shape[0], 1), F32)}

    def hidden(cs):
        hf = hf_ref[...]
        g = _dot(hf, wg_ref[:, cs])
        u = _dot(hf, wu_ref[:, cs])
        act_ref[:, cs] = (g * jax.nn.sigmoid(g) * u).astype(BF16)

    def down(cs):
        y = x2_ref[:, cs] + _dot(act_ref[...], wd_ref[:, cs])
        live["ss"] = live["ss"] + jnp.sum(y * y, axis=-1, keepdims=True)
        o_ref[:, cs] = y

    def final_norm():
        inv = lax.rsqrt(live["ss"] * (1.0 / D_MODEL) + EPS)
        o_ref[...] = o_ref[...] * inv * nfin_ref[...]

    hid = [slice(c * FF_CHUNK, (c + 1) * FF_CHUNK) for c in range(D_FF // FF_CHUNK)]
    out = [slice(c * MIX_CHUNK, (c + 1) * MIX_CHUNK) for c in range(D_MODEL // MIX_CHUNK)]
    return ([functools.partial(hidden, cs) for cs in hid]
            + [functools.partial(down, cs) for cs in out] + [final_norm])


def _tail_kernel(x_ref, k_ref, v_ref, nq_ref, wq_ref, wo_ref, nf_ref, wg_ref, wu_ref, wd_ref,
                 nfin_ref, o_ref, x2_scr, hf_scr, act_ref):
    i = pl.program_id(0)
    n_tiles = pl.num_programs(0) - 1
    slot_w = i % 2
    slot_r = 1 - slot_w

    def xattn():
        return _xattn_stages(x_ref, k_ref, v_ref, nq_ref, wq_ref, wo_ref, nf_ref,
                             x2_scr.at[slot_w], hf_scr.at[slot_w])

    def ffn():
        return _ffn_stages(x2_scr.at[slot_r], hf_scr.at[slot_r], wg_ref, wu_ref, wd_ref,
                           nfin_ref, act_ref, o_ref)

    def merge(lead, lag):
        n_hidden = D_FF // FF_CHUNK
        n_first = n_hidden - len(lead)
        return lag[:n_first] + _interleave(lag[n_first:n_hidden], lead) + lag[n_hidden:]

    _run_lookahead(i, n_tiles, xattn, ffn, merge)


def _tail(x2d, kext, vext, nq, wq, wo, nf, wg, wu, wd, nfin, seq):
    tm = TM_TAIL
    t = x2d.shape[0]
    n_tiles = t // tm
    tiles_per_seq = seq // tm
    cur = lambda i: jnp.minimum(i, n_tiles - 1)
    prv = lambda i: jnp.maximum(i - 1, 0)
    mem_spec = pl.BlockSpec((1, MEM_HEADS, N_MEM, MEM_WIDTH),
                            lambda i: (cur(i) // tiles_per_seq, 0, 0, 0))
    consts = (nq, wq, wo, nf, wg, wu, wd, nfin)
    return pl.pallas_call(
        _tail_kernel,
        grid=(n_tiles + 1,),
        in_specs=[pl.BlockSpec((tm, D_MODEL), lambda i: (cur(i), 0)), mem_spec, mem_spec]
        + [_const_spec(c.shape) for c in consts],
        out_specs=pl.BlockSpec((tm, D_MODEL), lambda i: (prv(i), 0)),
        out_shape=jax.ShapeDtypeStruct((t, D_MODEL), F32),
        scratch_shapes=[pltpu.VMEM((2, tm, D_MODEL), F32), pltpu.VMEM((2, tm, D_MODEL), BF16),
                        pltpu.VMEM((tm, D_FF), BF16)],
        compiler_params=_params(("arbitrary",)),
        name="tail",
    )(x2d, kext, vext, *consts)


def _layer(x2d, mem, batch, seq, norm_mix_w, w_in, b_gate, attn_sinks, gla_gate_w2, gla_gate_b,
           gla_norm_w, w_attn_o, w_gla_o, w_mix_o, norm_mem_q_w, norm_mem_kv_w, w_mem_q,
           w_mem_kv, w_mem_o, norm_ffn_w, w_ffn_gate, w_ffn_up, w_ffn_down, out_norm_w):
    w2_p = jnp.pad(gla_gate_w2, ((0, LANES - GLA_GATE_RANK), (0, 0)))
    bf = lambda a: a.astype(BF16)
    r2 = lambda a: a.reshape(1, -1)

    qa, ka, vat, qg, kg, vgt, gg, gk, gates = _in_proj(
        x2d, r2(norm_mix_w), bf(w_in.T), bf(w2_p), r2(gla_gate_b), r2(b_gate),
        r2(jnp.tile(gla_norm_w, GLA_HEADS)))
    x1 = _attn_mix(attn_sinks, qa, ka, vat, qg, kg, gk, vgt, gg, x2d, gates,
                   bf(w_attn_o), bf(w_gla_o), bf(w_mix_o), seq)
    kext, vext = _mem_kv(mem, r2(norm_mem_kv_w), bf(w_mem_kv))
    return _tail(x1, kext, vext, r2(norm_mem_q_w), bf(w_mem_q), bf(w_mem_o), r2(norm_ffn_w),
                 bf(w_ffn_gate), bf(w_ffn_up), bf(w_ffn_down), r2(out_norm_w), seq)


def kernel(x, mem, norm_mix_w, w_in, b_gate, attn_sinks, gla_gate_w2, gla_gate_b, gla_norm_w,
           w_attn_o, w_gla_o, w_mix_o, norm_mem_q_w, norm_mem_kv_w, w_mem_q, w_mem_kv, w_mem_o,
           norm_ffn_w, w_ffn_gate, w_ffn_up, w_ffn_down, norm_final_w):
    batch, seq, d = x.shape
    depth = w_in.shape[0]
    assert depth == 1 and d == D_MODEL
    assert seq % TM_PROJ == 0 and seq % TM_ATTN == 0 and seq % TM_TAIL == 0
    out = _layer(x.reshape(batch * seq, d), mem, batch, seq, norm_mix_w[0], w_in[0], b_gate[0],
                 attn_sinks[0], gla_gate_w2[0], gla_gate_b[0], gla_norm_w[0], w_attn_o[0],
                 w_gla_o[0], w_mix_o[0], norm_mem_q_w[0], norm_mem_kv_w[0], w_mem_q[0],
                 w_mem_kv[0], w_mem_o[0], norm_ffn_w[0], w_ffn_gate[0], w_ffn_up[0],
                 w_ffn_down[0], norm_final_w)
    return out.reshape(batch, seq, d)
```

```python
import functools

import jax
import jax.numpy as jnp
from jax import lax
from jax.experimental import pallas as pl
from jax.experimental.pallas import tpu as pltpu

D_MODEL = 1024
CHUNK = 64
N_MEM = 256
EPS = 1e-6

SWA_HEADS = 16
SWA_KV_HEADS = 2
SWA_HEAD_DIM = 64
SWA_BLOCK = 128

GLA_HEADS = 4
GLA_KEY_DIM = D_MODEL // 2
GLA_VAL_DIM = D_MODEL
GLA_DK = GLA_KEY_DIM // GLA_HEADS
GLA_DV = GLA_VAL_DIM // GLA_HEADS
GLA_GATE_RANK = 16
GLA_GATE_NORM = 16.0

MEM_HEADS = 4
MEM_HEAD_DIM = 64
MEM_WIDTH = MEM_HEADS * MEM_HEAD_DIM

D_FF = -(-(8 * D_MODEL) // (3 * 256)) * 256

IN_SIZES = (SWA_HEADS * SWA_HEAD_DIM, SWA_KV_HEADS * SWA_HEAD_DIM, SWA_KV_HEADS * SWA_HEAD_DIM,
            GLA_KEY_DIM, GLA_KEY_DIM, GLA_VAL_DIM, GLA_VAL_DIM, GLA_GATE_RANK, 2 * D_MODEL)
IN_OFFSETS = tuple(sum(IN_SIZES[:i]) for i in range(len(IN_SIZES) + 1))

LANES = 128
VMEM_LIMIT = 56 * 1024 * 1024

LOG2E = 1.4426950408889634

TM_PROJ = 1024
TM_ATTN = 512
TM_TAIL = 512
MIX_CHUNK = 256
FF_CHUNK = 256

BF16 = jnp.bfloat16
F32 = jnp.float32


def _rms(x, w):
    return x * lax.rsqrt(jnp.mean(x * x, axis=-1, keepdims=True) + EPS) * w


def _dot(a, b):
    return jnp.dot(a, b, preferred_element_type=F32)


def _dot_nt(a, b):
    return lax.dot_general(a, b, (((1,), (1,)), ((), ())), preferred_element_type=F32)


def _const_spec(shape):
    zeros = (0,) * len(shape)
    return pl.BlockSpec(shape, lambda *_: zeros, pipeline_mode=pl.Buffered(1))


def _params(semantics):
    return pltpu.CompilerParams(dimension_semantics=semantics, vmem_limit_bytes=VMEM_LIMIT)


def _in_proj_kernel(x_ref, nw_ref, wt_ref, w2_ref, gb_ref, bg_ref, gn_ref,
                    qa_ref, ka_ref, vat_ref, qg_ref, kg_ref, vgt_ref, gg_ref, gk_ref, gate_ref):
    o_qa, o_ka, o_va, o_qg, o_kg, o_vg, o_gg, o_alr, o_gate, o_end = IN_OFFSETS
    hd = SWA_HEAD_DIM
    h = _rms(x_ref[...], nw_ref[...]).astype(BF16)
    qa_ref[...] = (_dot_nt(h, wt_ref[o_qa:o_ka, :]) * (hd ** -0.5 * LOG2E)).astype(BF16)
    twice = lambda o: jnp.concatenate(
        [wt_ref[o + j * hd:o + (j + 1) * hd, :] for j in range(SWA_KV_HEADS) for _ in range(2)],
        axis=0)
    ka_ref[...] = _dot_nt(h, twice(o_ka)).astype(BF16)
    vat_ref[...] = _dot_nt(twice(o_va), h).astype(BF16)
    qg_ref[...] = _dot_nt(h, wt_ref[o_qg:o_kg, :]).astype(BF16)
    kg_ref[...] = _dot_nt(h, wt_ref[o_kg:o_vg, :]).astype(BF16)
    vgt_ref[...] = _dot_nt(wt_ref[o_vg:o_gg, :], h).astype(BF16)
    g = _dot_nt(h, wt_ref[o_gg:o_alr, :])
    gg_ref[...] = (g * jax.nn.sigmoid(g) * gn_ref[...]).astype(BF16)
    a_lr = _dot_nt(h, wt_ref[o_alr:o_alr + LANES, :]).astype(BF16)
    z = _dot(a_lr, w2_ref[...]) + gb_ref[...]
    log_sig = jnp.minimum(z, 0.0) - jnp.log1p(jnp.exp(-jnp.abs(z)))
    gk_ref[...] = log_sig * (1.0 / GLA_GATE_NORM)
    gate_ref[...] = jax.nn.sigmoid(_dot_nt(h, wt_ref[o_gate:o_end, :]) + bg_ref[...]).astype(BF16)


def _in_proj(x2d, nw, wt, w2, gb, bg, gn):
    t = x2d.shape[0]
    tm = TM_PROJ
    row = lambda n: pl.BlockSpec((tm, n), lambda i: (i, 0))
    col = lambda n: pl.BlockSpec((n, tm), lambda i: (0, i))
    consts = (nw, wt, w2, gb, bg, gn)
    kw = 2 * SWA_KV_HEADS * SWA_HEAD_DIM
    tok = lambda n, dt=BF16: jax.ShapeDtypeStruct((t, n), dt)
    out_shape = (
        tok(SWA_HEADS * SWA_HEAD_DIM), tok(kw), jax.ShapeDtypeStruct((kw, t), BF16),
        tok(GLA_KEY_DIM), tok(GLA_KEY_DIM), jax.ShapeDtypeStruct((GLA_VAL_DIM, t), BF16),
        tok(GLA_VAL_DIM), tok(GLA_KEY_DIM, F32), tok(2 * D_MODEL),
    )
    out_specs = (
        row(SWA_HEADS * SWA_HEAD_DIM), row(kw), col(kw),
        row(GLA_KEY_DIM), row(GLA_KEY_DIM), col(GLA_VAL_DIM),
        row(GLA_VAL_DIM), row(GLA_KEY_DIM), row(2 * D_MODEL),
    )
    return pl.pallas_call(
        _in_proj_kernel,
        grid=(t // tm,),
        in_specs=[row(D_MODEL)] + [_const_spec(c.shape) for c in consts],
        out_specs=out_specs,
        out_shape=out_shape,
        compiler_params=_params(("arbitrary",)),
        name="in_proj",
    )(x2d, *consts)


def _swa_stages(sink_ref, q_ref, k_ref, kp_ref, vt_ref, vtp_ref, seq_start, write):
    blk = SWA_BLOCK
    win = blk + CHUNK
    low = lax.broadcasted_iota(jnp.int32, (CHUNK, LANES), 1) < SWA_HEAD_DIM
    low_row = lax.broadcasted_iota(jnp.int32, (1, LANES), 1) < SWA_HEAD_DIM
    prev_bias = jnp.where(seq_start, -1e30, 0.0).astype(F32)
    k_all = jnp.concatenate([kp_ref[...], k_ref[...]], axis=0)
    vt_all = jnp.concatenate([vtp_ref[...], vt_ref[...]], axis=1)
    zeros_chunk = jnp.zeros((CHUNK, LANES), BF16)
    pairs_per_kv = SWA_HEADS // SWA_KV_HEADS // 2
    n_pairs = SWA_HEADS // 2
    chunks = [(u, qh) for u in range(q_ref.shape[0] // blk) for qh in range(blk // CHUNK)]
    live = {}

    def scores(p):
        kj = k_all[:, (p // pairs_per_kv) * LANES:(p // pairs_per_kv + 1) * LANES]
        res = []
        for u, qh in chunks:
            r0 = u * blk + qh * CHUNK
            qp = q_ref[r0:r0 + CHUNK, p * LANES:(p + 1) * LANES]
            zero = jnp.zeros_like(qp)
            qsel = jnp.concatenate([jnp.where(low, qp, zero), jnp.where(low, zero, qp)], axis=0)
            res.append(_dot_nt(kj[r0:r0 + win], qsel))
        live["s", p] = res

    def softmax_values(p):
        sink_row = jnp.where(low_row, sink_ref[2 * p], sink_ref[2 * p + 1]) * LOG2E
        vtj = vt_all[(p // pairs_per_kv) * LANES:(p // pairs_per_kv + 1) * LANES, :]
        res = []
        for (u, qh), st in zip(chunks, live.pop(("s", p))):
            if u == 0:
                n_prev = blk - qh * CHUNK
                st = jnp.concatenate([st[:n_prev] + prev_bias, st[n_prev:]], axis=0)
            m = jnp.maximum(jnp.max(st, axis=0, keepdims=True), sink_row)
            e = jnp.exp2(st - m)
            denom = jnp.sum(e, axis=0, keepdims=True) + jnp.exp2(sink_row - m)
            e_full = jnp.concatenate([zeros_chunk] * qh + [e.astype(BF16)]
                                     + [zeros_chunk] * (1 - qh), axis=0)
            res.append(_dot(vtj[:, u * blk:(u + 2) * blk], e_full) * (1.0 / denom))
        live["o", p] = res

    def store(p):
        for (u, qh), ot in zip(chunks, live.pop(("o", p))):
            r0 = u * blk + qh * CHUNK
            o2 = ot.T
            write(slice(r0, r0 + CHUNK), slice(p * LANES, (p + 1) * LANES),
                  jnp.where(low, o2[:CHUNK], o2[CHUNK:]).astype(BF16))

    return [(functools.partial(scores, p), functools.partial(softmax_values, p),
             functools.partial(store, p)) for p in range(n_pairs)]


def _gla_stages(q_ref, k_ref, gk_ref, vt_ref, g_ref, st_ref, seq_start, write):
    pair = 2 * CHUNK
    heads = range(GLA_HEADS)
    ks = [slice(h * GLA_DK, (h + 1) * GLA_DK) for h in heads]
    vs = [slice(h * GLA_DV, (h + 1) * GLA_DV) for h in heads]
    r = lax.broadcasted_iota(jnp.int32, (pair, pair), 0)
    c = lax.broadcasted_iota(jnp.int32, (pair, pair), 1)
    tri = ((c <= r) & ((r // CHUNK) == (c // CHUNK))).astype(BF16)
    tri2 = jnp.concatenate([tri, tri], axis=1)
    first = lax.broadcasted_iota(jnp.int32, (pair, GLA_DK), 0) < CHUNK
    eps_scaled = EPS * GLA_DK
    n_pairs = q_ref.shape[0] // pair
    sts = [jnp.where(seq_start, jnp.zeros((GLA_DV, GLA_DK), F32), st_ref[h]) for h in heads]
    live = {}

    def decay_and_kv(i):
        sl = slice(i * pair, (i + 1) * pair)
        gk = gk_ref[sl, :]
        hi = gk.astype(BF16)
        lo = (gk - hi.astype(F32)).astype(BF16)
        b_all = _dot(tri2, jnp.concatenate([hi, lo], axis=0))
        res = []
        for h in heads:
            b = b_all[:, ks[h]]
            b_end0 = b[CHUNK - 1:CHUNK, :]
            b_end1 = b[pair - 1:pair, :]
            k_dec = (k_ref[sl, ks[h]].astype(F32)
                     * jnp.exp(jnp.where(first, b_end0, b_end1) - b)).astype(BF16)
            zero = jnp.zeros_like(k_dec)
            kd = jnp.concatenate([jnp.where(first, k_dec, zero), jnp.where(first, zero, k_dec)],
                                 axis=1)
            kv = _dot(vt_ref[vs[h], sl], kd)
            res.append((jnp.exp(b_end0), jnp.exp(b_end1), kv))
        live[i] = res

    def state_and_out(i):
        sl = slice(i * pair, (i + 1) * pair)
        dk = live.pop(i)
        outs = []
        for h in heads:
            a0, a1, kv = dk[h]
            st0 = sts[h] * a0 + kv[:, :GLA_DK]
            st1 = st0 * a1 + kv[:, GLA_DK:]
            sts[h] = st1
            q = q_ref[sl, ks[h]]
            outs.append(jnp.concatenate([_dot_nt(q[:CHUNK], st0.astype(BF16)),
                                         _dot_nt(q[CHUNK:], st1.astype(BF16))], axis=0))
        for h in heads:
            o = outs[h]
            inv = lax.rsqrt(jnp.mean(o * o, axis=-1, keepdims=True) + eps_scaled)
            write(sl, vs[h], (o * inv * g_ref[sl, vs[h]].astype(F32)).astype(BF16))

    def save_state():
        for h in heads:
            st_ref[h] = sts[h]

    stages = [functools.partial(decay_and_kv, 0)]
    for i in range(n_pairs):
        if i + 1 < n_pairs:
            stages.append(functools.partial(decay_and_kv, i + 1))
        stages.append(functools.partial(state_and_out, i))
    stages.append(save_state)
    return stages


def _mix_stages(x_ref, gate_ref, wa_ref, wg_ref, wm_ref, oa_ref, og_ref, mg_ref, o_ref):
    pieces = [slice(c * MIX_CHUNK, (c + 1) * MIX_CHUNK) for c in range(D_MODEL // MIX_CHUNK)]

    def merge(cs):
        ya = _dot(oa_ref[...], wa_ref[:, cs])
        yg = _dot(og_ref[...], wg_ref[:, cs])
        cs_b = slice(D_MODEL + cs.start, D_MODEL + cs.stop)
        mg_ref[:, cs] = (gate_ref[:, cs].astype(F32) * ya
                         + gate_ref[:, cs_b].astype(F32) * yg).astype(BF16)

    def project(cs):
        o_ref[:, cs] = x_ref[:, cs] + _dot(mg_ref[...], wm_ref[:, cs])

    return ([functools.partial(merge, cs) for cs in pieces]
            + [functools.partial(project, cs) for cs in pieces])


def _interleave(major, minor):
    out, done = [], 0
    for n, stage in enumerate(major):
        out.append(stage)
        want = (n + 1) * len(minor) // len(major)
        out.extend(minor[done:want])
        done = want
    return out


def _run_lookahead(i, n_tiles, lead, lag, merge, init=None):
    @pl.when(i == 0)
    def _():
        if init is not None:
            init()
        for stage in merge(lead(), []):
            stage()

    @pl.when(jnp.logical_and(i > 0, i < n_tiles))
    def _():
        for stage in merge(lead(), lag()):
            stage()

    @pl.when(i == n_tiles)
    def _():
        for stage in merge([], lag()):
            stage()


def _attn_mix_kernel(tiles_per_seq,
                     sink_ref, q_ref, k_ref, kp_ref, vt_ref, vtp_ref,
                     qg_ref, kg_ref, gk_ref, vgt_ref, gg_ref,
                     x_ref, gate_ref, wa_ref, wg_ref, wm_ref, fg_ref, fu_ref, fd_ref,
                     o_ref, fg16_ref, fu16_ref, fd16_ref, oa_scr, og_scr, mg_scr, st_ref):
    i = pl.program_id(0)
    n_tiles = pl.num_programs(0) - 1
    seq_start = (i % tiles_per_seq) == 0
    slot_w = i % 2
    slot_r = 1 - slot_w

    def write_oa(rows, cols, val):
        oa_scr[slot_w, rows, cols] = val

    def write_og(rows, cols, val):
        og_scr[slot_w, rows, cols] = val

    def mixers():
        swa = _swa_stages(sink_ref, q_ref, k_ref, kp_ref, vt_ref, vtp_ref, seq_start, write_oa)
        gla = _gla_stages(qg_ref, kg_ref, gk_ref, vgt_ref, gg_ref, st_ref, seq_start, write_og)
        n = len(swa)
        order = [swa[0][0]]
        for p in range(n):
            if p + 1 < n:
                order.append(swa[p + 1][0])
            order.append(swa[p][1])
            if p > 0:
                order.append(swa[p - 1][2])
        order.append(swa[n - 1][2])
        return _interleave(order, gla) + [convert_weights]

    def projections():
        return _mix_stages(x_ref, gate_ref, wa_ref, wg_ref, wm_ref,
                           oa_scr.at[slot_r], og_scr.at[slot_r], mg_scr, o_ref)

    def merge(lead, lag):
        return _interleave(lead, lag) if lead else lag

    def convert_weights():
        for src, dst in ((fg_ref, fg16_ref), (fu_ref, fu16_ref), (fd_ref, fd16_ref)):
            dst[...] = src[...].astype(BF16)

    def init():
        st_ref[...] = jnp.zeros_like(st_ref)

    _run_lookahead(i, n_tiles, mixers, projections, merge, init)


def _attn_mix(sinks, qa, ka, vat, qg, kg, gk, vgt, gg, x2d, gates, wa, wg, wm, ffn_w, seq):
    t = x2d.shape[0]
    tm = TM_ATTN
    n_tiles = t // tm
    blocks = tm // SWA_BLOCK
    kw = ka.shape[1]
    cur = lambda i: jnp.minimum(i, n_tiles - 1)
    prv = lambda i: jnp.maximum(i - 1, 0)
    prev_block = lambda i: jnp.maximum(cur(i) * blocks - 1, 0)
    tok = lambda n: pl.BlockSpec((tm, n), lambda i: (cur(i), 0))
    col = lambda n: pl.BlockSpec((n, tm), lambda i: (0, cur(i)))
    old = lambda n: pl.BlockSpec((tm, n), lambda i: (prv(i), 0))

    def slab(w):
        n_slabs = max(n for n in range(1, n_tiles + 1) if w.shape[0] % (16 * n) == 0)
        return pl.BlockSpec((w.shape[0] // n_slabs, w.shape[1]),
                            lambda i: (jnp.minimum(i, n_slabs - 1), 0))

    ffn_specs = [slab(w) for w in ffn_w]
    return pl.pallas_call(
        functools.partial(_attn_mix_kernel, seq // tm),
        grid=(n_tiles + 1,),
        in_specs=[
            pl.BlockSpec(memory_space=pltpu.SMEM),
            tok(qa.shape[1]), tok(kw),
            pl.BlockSpec((SWA_BLOCK, kw), lambda i: (prev_block(i), 0)),
            col(kw),
            pl.BlockSpec((kw, SWA_BLOCK), lambda i: (0, prev_block(i))),
            tok(GLA_KEY_DIM), tok(GLA_KEY_DIM), tok(GLA_KEY_DIM), col(GLA_VAL_DIM),
            tok(GLA_VAL_DIM),
            old(D_MODEL), old(2 * D_MODEL),
            _const_spec(wa.shape), _const_spec(wg.shape), _const_spec(wm.shape),
        ] + ffn_specs,
        out_specs=[old(D_MODEL)] + ffn_specs,
        out_shape=[jax.ShapeDtypeStruct((t, D_MODEL), F32)]
        + [jax.ShapeDtypeStruct(w.shape, BF16) for w in ffn_w],
        scratch_shapes=[
            pltpu.VMEM((2, tm, SWA_HEADS * SWA_HEAD_DIM), BF16),
            pltpu.VMEM((2, tm, GLA_VAL_DIM), BF16),
            pltpu.VMEM((tm, D_MODEL), BF16),
            pltpu.VMEM((GLA_HEADS, GLA_DV, GLA_DK), F32),
        ],
        compiler_params=_params(("arbitrary",)),
        name="attn_mix",
    )(sinks, qa, ka, ka, vat, vat, qg, kg, gk, vgt, gg, x2d, gates, wa, wg, wm, *ffn_w)


def _mem_kv_kernel(m_ref, nw_ref, w_ref, k_ref, v_ref):
    mn = _rms(m_ref[0], nw_ref[...]).astype(BF16)
    kv = _dot(mn, w_ref[...])
    k = kv[:, :MEM_WIDTH].astype(BF16)
    v = kv[:, MEM_WIDTH:].astype(BF16)
    head = lax.broadcasted_iota(jnp.int32, (N_MEM, MEM_WIDTH), 1) // MEM_HEAD_DIM
    zero = jnp.zeros_like(k)
    for h in range(MEM_HEADS):
        k_ref[0, h] = jnp.where(head == h, k, zero)
        v_ref[0, h] = jnp.where(head == h, v, zero)


def _mem_kv(mem, nw, w):
    batch = mem.shape[0]
    out = jax.ShapeDtypeStruct((batch, MEM_HEADS, N_MEM, MEM_WIDTH), BF16)
    spec = pl.BlockSpec((1, MEM_HEADS, N_MEM, MEM_WIDTH), lambda b: (b, 0, 0, 0))
    return pl.pallas_call(
        _mem_kv_kernel,
        grid=(batch,),
        in_specs=[pl.BlockSpec((1, N_MEM, D_MODEL), lambda b: (b, 0, 0)),
                  _const_spec(nw.shape), _const_spec(w.shape)],
        out_specs=(spec, spec),
        out_shape=(out, out),
        compiler_params=_params(("arbitrary",)),
        name="mem_kv",
    )(mem, nw, w)


def _xattn_stages(x_ref, k_ref, v_ref, nq_ref, wq_ref, wo_ref, nf_ref, x2_ref, hf_ref):
    live = {}

    def queries():
        hq = _rms(x_ref[...], nq_ref[...]).astype(BF16)
        live["q"] = (_dot(hq, wq_ref[...]) * (MEM_HEAD_DIM ** -0.5)).astype(BF16)
        live["o"] = jnp.zeros((x_ref.shape[0], MEM_WIDTH), F32)

    def head(h):
        s = _dot_nt(live["q"], k_ref[0, h])
        e = jnp.exp(s - jnp.max(s, axis=-1, keepdims=True))
        inv = 1.0 / jnp.sum(e, axis=-1, keepdims=True)
        live["o"] = live["o"] + _dot(e.astype(BF16), v_ref[0, h]) * inv

    def project():
        x2 = x_ref[...] + _dot(live.pop("o").astype(BF16), wo_ref[...])
        x2_ref[...] = x2
        hf_ref[...] = _rms(x2, nf_ref[...]).astype(BF16)

    return [queries] + [functools.partial(head, h) for h in range(MEM_HEADS)] + [project]


def _ffn_stages(x2_ref, hf_ref, wg_ref, wu_ref, wd_ref, nfin_ref, act_ref, o_ref):
    live = {"ss": jnp.zeros((o_ref.shape[0], 1), F32)}

    def hidden(cs):
        hf = hf_ref[...]
        g = _dot(hf, wg_ref[:, cs])
        u = _dot(hf, wu_ref[:, cs])
        act_ref[:, cs] = (g * jax.nn.sigmoid(g) * u).astype(BF16)

    def down(cs):
        y = x2_ref[:, cs] + _dot(act_ref[...], wd_ref[:, cs])
        live["ss"] = live["ss"] + jnp.sum(y * y, axis=-1, keepdims=True)
        o_ref[:, cs] = y

    def final_norm():
        inv = lax.rsqrt(live["ss"] * (1.0 / D_MODEL) + EPS)
        o_ref[...] = o_ref[...] * inv * nfin_ref[...]

    hid = [slice(c * FF_CHUNK, (c + 1) * FF_CHUNK) for c in range(D_FF // FF_CHUNK)]
    out = [slice(c * MIX_CHUNK, (c + 1) * MIX_CHUNK) for c in range(D_MODEL // MIX_CHUNK)]
    return ([functools.partial(hidden, cs) for cs in hid]
            + [functools.partial(down, cs) for cs in out] + [final_norm])


def _tail_kernel(x_ref, k_ref, v_ref, nq_ref, wq_ref, wo_ref, nf_ref, wg_ref, wu_ref, wd_ref,
                 nfin_ref, o_ref, x2_scr, hf_scr, act_ref):
    i = pl.program_id(0)
    n_tiles = pl.num_programs(0) - 1
    slot_w = i % 2
    slot_r = 1 - slot_w

    def xattn():
        return _xattn_stages(x_ref, k_ref, v_ref, nq_ref, wq_ref, wo_ref, nf_ref,
                             x2_scr.at[slot_w], hf_scr.at[slot_w])

    def ffn():
        return _ffn_stages(x2_scr.at[slot_r], hf_scr.at[slot_r], wg_ref, wu_ref, wd_ref,
                           nfin_ref, act_ref, o_ref)

    def merge(lead, lag):
        if not lag:
            return lead
        n_hidden = D_FF // FF_CHUNK
        n_first = n_hidden - len(lead)
        return lag[:n_first] + _interleave(lag[n_first:n_hidden], lead) + lag[n_hidden:]

    _run_lookahead(i, n_tiles, xattn, ffn, merge)


def _tail(x2d, kext, vext, nq, wq, wo, nf, wg, wu, wd, nfin, seq):
    tm = TM_TAIL
    t = x2d.shape[0]
    n_tiles = t // tm
    tiles_per_seq = seq // tm
    cur = lambda i: jnp.minimum(i, n_tiles - 1)
    prv = lambda i: jnp.maximum(i - 1, 0)
    mem_spec = pl.BlockSpec((1, MEM_HEADS, N_MEM, MEM_WIDTH),
                            lambda i: (cur(i) // tiles_per_seq, 0, 0, 0))
    consts = (nq, wq, wo, nf, wg, wu, wd, nfin)
    return pl.pallas_call(
        _tail_kernel,
        grid=(n_tiles + 1,),
        in_specs=[pl.BlockSpec((tm, D_MODEL), lambda i: (cur(i), 0)), mem_spec, mem_spec]
        + [_const_spec(c.shape) for c in consts],
        out_specs=pl.BlockSpec((tm, D_MODEL), lambda i: (prv(i), 0)),
        out_shape=jax.ShapeDtypeStruct((t, D_MODEL), F32),
        scratch_shapes=[pltpu.VMEM((2, tm, D_MODEL), F32), pltpu.VMEM((2, tm, D_MODEL), BF16),
                        pltpu.VMEM((tm, D_FF), BF16)],
        compiler_params=_params(("arbitrary",)),
        name="tail",
    )(x2d, kext, vext, *consts)


def _layer(x2d, mem, batch, seq, norm_mix_w, w_in, b_gate, attn_sinks, gla_gate_w2, gla_gate_b,
           gla_norm_w, w_attn_o, w_gla_o, w_mix_o, norm_mem_q_w, norm_mem_kv_w, w_mem_q,
           w_mem_kv, w_mem_o, norm_ffn_w, w_ffn_gate, w_ffn_up, w_ffn_down, out_norm_w):
    w2_p = jnp.pad(gla_gate_w2, ((0, LANES - GLA_GATE_RANK), (0, 0)))
    bf = lambda a: a.astype(BF16)
    r2 = lambda a: a.reshape(1, -1)

    qa, ka, vat, qg, kg, vgt, gg, gk, gates = _in_proj(
        x2d, r2(norm_mix_w), bf(w_in.T), bf(w2_p), r2(gla_gate_b), r2(b_gate),
        r2(jnp.tile(gla_norm_w, GLA_HEADS)))
    x1, wg16, wu16, wd16 = _attn_mix(
        attn_sinks, qa, ka, vat, qg, kg, gk, vgt, gg, x2d, gates,
        bf(w_attn_o), bf(w_gla_o), bf(w_mix_o), (w_ffn_gate, w_ffn_up, w_ffn_down), seq)
    kext, vext = _mem_kv(mem, r2(norm_mem_kv_w), bf(w_mem_kv))
    return _tail(x1, kext, vext, r2(norm_mem_q_w), bf(w_mem_q), bf(w_mem_o), r2(norm_ffn_w),
                 wg16, wu16, wd16, r2(out_norm_w), seq)


def kernel(x, mem, norm_mix_w, w_in, b_gate, attn_sinks, gla_gate_w2, gla_gate_b, gla_norm_w,
           w_attn_o, w_gla_o, w_mix_o, norm_mem_q_w, norm_mem_kv_w, w_mem_q, w_mem_kv, w_mem_o,
           norm_ffn_w, w_ffn_gate, w_ffn_up, w_ffn_down, norm_final_w):
    batch, seq, d = x.shape
    depth = w_in.shape[0]
    assert depth == 1 and d == D_MODEL
    assert seq % TM_PROJ == 0 and seq % TM_ATTN == 0 and seq % TM_TAIL == 0
    out = _layer(x.reshape(batch * seq, d), mem, batch, seq, norm_mix_w[0], w_in[0], b_gate[0],
                 attn_sinks[0], gla_gate_w2[0], gla_gate_b[0], gla_norm_w[0], w_attn_o[0],
                 w_gla_o[0], w_mix_o[0], norm_mem_q_w[0], norm_mem_kv_w[0], w_mem_q[0],
                 w_mem_kv[0], w_mem_o[0], norm_ffn_w[0], w_ffn_gate[0], w_ffn_up[0],
                 w_ffn_down[0], norm_final_w)
    return out.reshape(batch, seq, d)
```

```python
import functools

import jax
import jax.numpy as jnp
from jax import lax
from jax.experimental import pallas as pl
from jax.experimental.pallas import tpu as pltpu

D_MODEL = 1024
CHUNK = 64
N_MEM = 256
EPS = 1e-6

SWA_HEADS = 16
SWA_KV_HEADS = 2
SWA_HEAD_DIM = 64
SWA_BLOCK = 128

GLA_HEADS = 4
GLA_KEY_DIM = D_MODEL // 2
GLA_VAL_DIM = D_MODEL
GLA_DK = GLA_KEY_DIM // GLA_HEADS
GLA_DV = GLA_VAL_DIM // GLA_HEADS
GLA_GATE_RANK = 16
GLA_GATE_NORM = 16.0

MEM_HEADS = 4
MEM_HEAD_DIM = 64
MEM_WIDTH = MEM_HEADS * MEM_HEAD_DIM

D_FF = -(-(8 * D_MODEL) // (3 * 256)) * 256

IN_SIZES = (SWA_HEADS * SWA_HEAD_DIM, SWA_KV_HEADS * SWA_HEAD_DIM, SWA_KV_HEADS * SWA_HEAD_DIM,
            GLA_KEY_DIM, GLA_KEY_DIM, GLA_VAL_DIM, GLA_VAL_DIM, GLA_GATE_RANK, 2 * D_MODEL)
IN_OFFSETS = tuple(sum(IN_SIZES[:i]) for i in range(len(IN_SIZES) + 1))

LANES = 128
VMEM_LIMIT = 56 * 1024 * 1024

LOG2E = 1.4426950408889634

TM_PROJ = 1024
TM_ATTN = 512
TM_TAIL = 512
MIX_CHUNK = 256
FF_CHUNK = 256

BF16 = jnp.bfloat16
F32 = jnp.float32


def _rms(x, w):
    return x * lax.rsqrt(jnp.mean(x * x, axis=-1, keepdims=True) + EPS) * w


def _dot(a, b):
    return jnp.dot(a, b, preferred_element_type=F32)


def _dot_nt(a, b):
    return lax.dot_general(a, b, (((1,), (1,)), ((), ())), preferred_element_type=F32)


def _const_spec(shape):
    zeros = (0,) * len(shape)
    return pl.BlockSpec(shape, lambda *_: zeros, pipeline_mode=pl.Buffered(1))


def _params(semantics):
    return pltpu.CompilerParams(dimension_semantics=semantics, vmem_limit_bytes=VMEM_LIMIT)


def _in_proj_kernel(x_ref, nw_ref, wt_ref, w2_ref, gb_ref, bg_ref, gn_ref,
                    qa_ref, ka_ref, vat_ref, qg_ref, kg_ref, vgt_ref, gg_ref, gk_ref, gate_ref):
    o_qa, o_ka, o_va, o_qg, o_kg, o_vg, o_gg, o_alr, o_gate, o_end = IN_OFFSETS
    hd = SWA_HEAD_DIM
    h = _rms(x_ref[...], nw_ref[...]).astype(BF16)
    half = (o_end - o_gate) // 2

    def branch_gates(lo):
        pre = _dot_nt(h, wt_ref[o_gate + lo:o_gate + lo + half, :]) + bg_ref[:, lo:lo + half]
        gate_ref[:, lo:lo + half] = jax.nn.sigmoid(pre).astype(BF16)

    branch_gates(0)
    group = SWA_HEADS // SWA_KV_HEADS
    wq = jnp.concatenate(
        [wt_ref[o_qa + (p + j * group) * hd:o_qa + (p + j * group + 1) * hd, :]
         for p in range(group) for j in range(SWA_KV_HEADS)], axis=0)
    qa_ref[...] = (_dot_nt(h, wq) * (hd ** -0.5 * LOG2E)).astype(BF16)
    branch_gates(half)
    vgt_ref[...] = _dot_nt(wt_ref[o_vg:o_gg, :], h).astype(BF16)
    g = _dot_nt(h, wt_ref[o_gg:o_alr, :])
    gg_ref[...] = (g * jax.nn.sigmoid(g) * gn_ref[...]).astype(BF16)
    kw = SWA_KV_HEADS * hd
    k_alr = _dot_nt(h, jnp.concatenate([wt_ref[o_ka:o_va, :], wt_ref[o_alr:o_alr + LANES, :]],
                                       axis=0))
    ka_ref[...] = k_alr[:, :kw].astype(BF16)
    qg_ref[...] = _dot_nt(h, wt_ref[o_qg:o_kg, :]).astype(BF16)
    z = _dot(k_alr[:, kw:].astype(BF16), w2_ref[...]) + gb_ref[...]
    log_sig = jnp.minimum(z, 0.0) - jnp.log(1.0 + jnp.exp(-jnp.abs(z)))
    gk_ref[...] = log_sig * (1.0 / GLA_GATE_NORM)
    kg_ref[...] = _dot_nt(h, wt_ref[o_kg:o_vg, :]).astype(BF16)
    vat_ref[...] = _dot_nt(wt_ref[o_va:o_qg, :], h).astype(BF16)


def _in_proj(x2d, nw, wt, w2, gb, bg, gn):
    t = x2d.shape[0]
    tm = TM_PROJ
    row = lambda n: pl.BlockSpec((tm, n), lambda i: (i, 0))
    col = lambda n: pl.BlockSpec((n, tm), lambda i: (0, i))
    consts = (nw, wt, w2, gb, bg, gn)
    kw = SWA_KV_HEADS * SWA_HEAD_DIM
    tok = lambda n, dt=BF16: jax.ShapeDtypeStruct((t, n), dt)
    out_shape = (
        tok(SWA_HEADS * SWA_HEAD_DIM), tok(kw), jax.ShapeDtypeStruct((kw, t), BF16),
        tok(GLA_KEY_DIM), tok(GLA_KEY_DIM), jax.ShapeDtypeStruct((GLA_VAL_DIM, t), BF16),
        tok(GLA_VAL_DIM), tok(GLA_KEY_DIM, F32), tok(2 * D_MODEL),
    )
    out_specs = (
        row(SWA_HEADS * SWA_HEAD_DIM), row(kw), col(kw),
        row(GLA_KEY_DIM), row(GLA_KEY_DIM), col(GLA_VAL_DIM),
        row(GLA_VAL_DIM), row(GLA_KEY_DIM), row(2 * D_MODEL),
    )
    return pl.pallas_call(
        _in_proj_kernel,
        grid=(t // tm,),
        in_specs=[row(D_MODEL)] + [_const_spec(c.shape) for c in consts],
        out_specs=out_specs,
        out_shape=out_shape,
        compiler_params=_params(("arbitrary",)),
        name="in_proj",
    )(x2d, *consts)


def _swa_stages(sink_ref, q_ref, k_ref, kp_ref, vt_ref, vtp_ref, seq_start, write):
    blk = SWA_BLOCK
    win = blk + CHUNK
    low = lax.broadcasted_iota(jnp.int32, (CHUNK, LANES), 1) < SWA_HEAD_DIM
    low_row = lax.broadcasted_iota(jnp.int32, (1, LANES), 1) < SWA_HEAD_DIM
    prev_bias = jnp.where(seq_start, -1e30, 0.0).astype(F32)
    k_all = jnp.concatenate([kp_ref[...], k_ref[...]], axis=0)
    vt_all = jnp.concatenate([vtp_ref[...], vt_ref[...]], axis=1)
    zeros_chunk = jnp.zeros((CHUNK, LANES), BF16)
    n_pairs = SWA_HEADS // SWA_KV_HEADS
    chunks = [(u, qh) for u in range(q_ref.shape[0] // blk) for qh in range(blk // CHUNK)]
    live = {}

    def scores(p):
        res = []
        for u, qh in chunks:
            r0 = u * blk + qh * CHUNK
            qp = q_ref[r0:r0 + CHUNK, p * LANES:(p + 1) * LANES]
            zero = jnp.zeros_like(qp)
            qsel = jnp.concatenate([jnp.where(low, qp, zero), jnp.where(low, zero, qp)], axis=0)
            res.append(_dot_nt(k_all[r0:r0 + win], qsel))
        live["s", p] = res

    def softmax_values(p):
        sink_row = jnp.where(low_row, sink_ref[p], sink_ref[p + n_pairs]) * LOG2E
        res = []
        for (u, qh), st in zip(chunks, live.pop(("s", p))):
            if u == 0:
                n_prev = blk - qh * CHUNK
                st = jnp.concatenate([st[:n_prev] + prev_bias, st[n_prev:]], axis=0)
            m = jnp.maximum(jnp.max(st, axis=0, keepdims=True), sink_row)
            e = jnp.exp2(st - m)
            denom = jnp.sum(e, axis=0, keepdims=True) + jnp.exp2(sink_row - m)
            e_full = jnp.concatenate([zeros_chunk] * qh + [e.astype(BF16)]
                                     + [zeros_chunk] * (1 - qh), axis=0)
            res.append(_dot(vt_all[:, u * blk:(u + 2) * blk], e_full) * (1.0 / denom))
        live["o", p] = res

    def store(p):
        for (u, qh), ot in zip(chunks, live.pop(("o", p))):
            r0 = u * blk + qh * CHUNK
            o2 = ot.T
            write(slice(r0, r0 + CHUNK), slice(p * LANES, (p + 1) * LANES),
                  jnp.where(low, o2[:CHUNK], o2[CHUNK:]).astype(BF16))

    return [(functools.partial(scores, p), functools.partial(softmax_values, p),
             functools.partial(store, p)) for p in range(n_pairs)]


def _gla_stages(q_ref, k_ref, gk_ref, vt_ref, g_ref, st_ref, seq_start, write):
    pair = 2 * CHUNK
    heads = range(GLA_HEADS)
    ks = [slice(h * GLA_DK, (h + 1) * GLA_DK) for h in heads]
    vs = [slice(h * GLA_DV, (h + 1) * GLA_DV) for h in heads]
    r = lax.broadcasted_iota(jnp.int32, (pair, pair), 0)
    c = lax.broadcasted_iota(jnp.int32, (pair, pair), 1)
    tri = ((c <= r) & ((r // CHUNK) == (c // CHUNK))).astype(BF16)
    tri2 = jnp.concatenate([tri, tri], axis=1)
    first = lax.broadcasted_iota(jnp.int32, (pair, GLA_DK), 0) < CHUNK
    eps_scaled = EPS * GLA_DK
    n_pairs = q_ref.shape[0] // pair
    sts = [jnp.where(seq_start, jnp.zeros((GLA_DV, GLA_DK), F32), st_ref[h]) for h in heads]
    live = {}

    def decay_and_kv(i):
        sl = slice(i * pair, (i + 1) * pair)
        gk = gk_ref[sl, :]
        hi = gk.astype(BF16)
        lo = (gk - hi.astype(F32)).astype(BF16)
        b_all = _dot(tri2, jnp.concatenate([hi, lo], axis=0))
        res = []
        for h in heads:
            b = b_all[:, ks[h]]
            b_end0 = b[CHUNK - 1:CHUNK, :]
            b_end1 = b[pair - 1:pair, :]
            k_dec = (k_ref[sl, ks[h]].astype(F32)
                     * jnp.exp(jnp.where(first, b_end0, b_end1) - b)).astype(BF16)
            zero = jnp.zeros_like(k_dec)
            kd = jnp.concatenate([jnp.where(first, k_dec, zero), jnp.where(first, zero, k_dec)],
                                 axis=1)
            kv = _dot(vt_ref[vs[h], sl], kd)
            res.append((jnp.exp(b_end0), jnp.exp(b_end1), kv))
        live[i] = res

    def state_and_out(i):
        sl = slice(i * pair, (i + 1) * pair)
        dk = live.pop(i)
        outs = []
        for h in heads:
            a0, a1, kv = dk[h]
            st0 = sts[h] * a0 + kv[:, :GLA_DK]
            st1 = st0 * a1 + kv[:, GLA_DK:]
            sts[h] = st1
            q = q_ref[sl, ks[h]]
            outs.append(jnp.concatenate([_dot_nt(q[:CHUNK], st0.astype(BF16)),
                                         _dot_nt(q[CHUNK:], st1.astype(BF16))], axis=0))
        for h in heads:
            o = outs[h]
            inv = lax.rsqrt(jnp.mean(o * o, axis=-1, keepdims=True) + eps_scaled)
            write(sl, vs[h], (o * inv * g_ref[sl, vs[h]].astype(F32)).astype(BF16))

    def save_state():
        for h in heads:
            st_ref[h] = sts[h]

    stages = [functools.partial(decay_and_kv, 0)]
    for i in range(n_pairs):
        if i + 1 < n_pairs:
            stages.append(functools.partial(decay_and_kv, i + 1))
        stages.append(functools.partial(state_and_out, i))
    stages.append(save_state)
    return stages


def _mix_stages(x_ref, gate_ref, wa_ref, wg_ref, wm_ref, oa_ref, og_ref, mg_ref, o_ref):
    pieces = [slice(c * MIX_CHUNK, (c + 1) * MIX_CHUNK) for c in range(D_MODEL // MIX_CHUNK)]

    def merge(cs):
        ya = _dot(oa_ref[...], wa_ref[:, cs])
        yg = _dot(og_ref[...], wg_ref[:, cs])
        cs_b = slice(D_MODEL + cs.start, D_MODEL + cs.stop)
        mg_ref[:, cs] = (gate_ref[:, cs].astype(F32) * ya
                         + gate_ref[:, cs_b].astype(F32) * yg).astype(BF16)

    def project(cs):
        o_ref[:, cs] = x_ref[:, cs] + _dot(mg_ref[...], wm_ref[:, cs])

    return ([functools.partial(merge, cs) for cs in pieces]
            + [functools.partial(project, cs) for cs in pieces])


def _interleave(major, minor):
    out, done = [], 0
    for n, stage in enumerate(major):
        out.append(stage)
        want = (n + 1) * len(minor) // len(major)
        out.extend(minor[done:want])
        done = want
    return out


def _run_lookahead(i, n_tiles, lead, lag, merge, init=None):
    @pl.when(i == 0)
    def _():
        if init is not None:
            init()
        for stage in merge(lead(), []):
            stage()

    @pl.when(jnp.logical_and(i > 0, i < n_tiles))
    def _():
        for stage in merge(lead(), lag()):
            stage()

    @pl.when(i == n_tiles)
    def _():
        for stage in merge([], lag()):
            stage()


def _attn_mix_kernel(tiles_per_seq,
                     sink_ref, q_ref, k_ref, kp_ref, vt_ref, vtp_ref,
                     qg_ref, kg_ref, gk_ref, vgt_ref, gg_ref,
                     x_ref, gate_ref, wa_ref, wg_ref, wm_ref, fg_ref, fu_ref, fd_ref,
                     o_ref, fg16_ref, fu16_ref, fd16_ref, oa_scr, og_scr, mg_scr, st_ref):
    i = pl.program_id(0)
    n_tiles = pl.num_programs(0) - 1
    seq_start = (i % tiles_per_seq) == 0
    slot_w = i % 2
    slot_r = 1 - slot_w

    def write_oa(rows, cols, val):
        oa_scr[slot_w, rows, cols] = val

    def write_og(rows, cols, val):
        og_scr[slot_w, rows, cols] = val

    def mixers():
        swa = _swa_stages(sink_ref, q_ref, k_ref, kp_ref, vt_ref, vtp_ref, seq_start, write_oa)
        gla = _gla_stages(qg_ref, kg_ref, gk_ref, vgt_ref, gg_ref, st_ref, seq_start, write_og)
        n = len(swa)
        order = [swa[0][0]]
        for p in range(n):
            if p + 1 < n:
                order.append(swa[p + 1][0])
            order.append(swa[p][1])
            if p > 0:
                order.append(swa[p - 1][2])
        order.append(swa[n - 1][2])
        return _interleave(order, gla) + [convert_weights]

    def projections():
        return _mix_stages(x_ref, gate_ref, wa_ref, wg_ref, wm_ref,
                           oa_scr.at[slot_r], og_scr.at[slot_r], mg_scr, o_ref)

    def merge(lead, lag):
        return _interleave(lead, lag) if lead else lag

    def convert_weights():
        for src, dst in ((fg_ref, fg16_ref), (fu_ref, fu16_ref), (fd_ref, fd16_ref)):
            dst[...] = src[...].astype(BF16)

    def init():
        st_ref[...] = jnp.zeros_like(st_ref)

    _run_lookahead(i, n_tiles, mixers, projections, merge, init)


def _attn_mix(sinks, qa, ka, vat, qg, kg, gk, vgt, gg, x2d, gates, wa, wg, wm, ffn_w, seq):
    t = x2d.shape[0]
    tm = TM_ATTN
    n_tiles = t // tm
    blocks = tm // SWA_BLOCK
    kw = ka.shape[1]
    cur = lambda i: jnp.minimum(i, n_tiles - 1)
    prv = lambda i: jnp.maximum(i - 1, 0)
    prev_block = lambda i: jnp.maximum(cur(i) * blocks - 1, 0)
    tok = lambda n: pl.BlockSpec((tm, n), lambda i: (cur(i), 0))
    col = lambda n: pl.BlockSpec((n, tm), lambda i: (0, cur(i)))
    old = lambda n: pl.BlockSpec((tm, n), lambda i: (prv(i), 0))

    def slab(w):
        n_slabs = max(n for n in range(1, n_tiles + 1) if w.shape[0] % (16 * n) == 0)
        return pl.BlockSpec((w.shape[0] // n_slabs, w.shape[1]),
                            lambda i: (jnp.minimum(i, n_slabs - 1), 0))

    ffn_specs = [slab(w) for w in ffn_w]
    return pl.pallas_call(
        functools.partial(_attn_mix_kernel, seq // tm),
        grid=(n_tiles + 1,),
        in_specs=[
            pl.BlockSpec(memory_space=pltpu.SMEM),
            tok(qa.shape[1]), tok(kw),
            pl.BlockSpec((SWA_BLOCK, kw), lambda i: (prev_block(i), 0)),
            col(kw),
            pl.BlockSpec((kw, SWA_BLOCK), lambda i: (0, prev_block(i))),
            tok(GLA_KEY_DIM), tok(GLA_KEY_DIM), tok(GLA_KEY_DIM), col(GLA_VAL_DIM),
            tok(GLA_VAL_DIM),
            old(D_MODEL), old(2 * D_MODEL),
            _const_spec(wa.shape), _const_spec(wg.shape), _const_spec(wm.shape),
        ] + ffn_specs,
        out_specs=[old(D_MODEL)] + ffn_specs,
        out_shape=[jax.ShapeDtypeStruct((t, D_MODEL), F32)]
        + [jax.ShapeDtypeStruct(w.shape, BF16) for w in ffn_w],
        scratch_shapes=[
            pltpu.VMEM((2, tm, SWA_HEADS * SWA_HEAD_DIM), BF16),
            pltpu.VMEM((2, tm, GLA_VAL_DIM), BF16),
            pltpu.VMEM((tm, D_MODEL), BF16),
            pltpu.VMEM((GLA_HEADS, GLA_DV, GLA_DK), F32),
        ],
        compiler_params=_params(("arbitrary",)),
        name="attn_mix",
    )(sinks, qa, ka, ka, vat, vat, qg, kg, gk, vgt, gg, x2d, gates, wa, wg, wm, *ffn_w)


def _mem_kv_kernel(m_ref, nw_ref, w_ref, k_ref, v_ref):
    mn = _rms(m_ref[0], nw_ref[...]).astype(BF16)
    kv = _dot(mn, w_ref[...])
    k = kv[:, :MEM_WIDTH].astype(BF16)
    v = kv[:, MEM_WIDTH:].astype(BF16)
    head = lax.broadcasted_iota(jnp.int32, (N_MEM, MEM_WIDTH), 1) // MEM_HEAD_DIM
    zero = jnp.zeros_like(k)
    for h in range(MEM_HEADS):
        k_ref[0, h] = jnp.where(head == h, k, zero)
        v_ref[0, h] = jnp.where(head == h, v, zero)


def _mem_kv(mem, nw, w):
    batch = mem.shape[0]
    out = jax.ShapeDtypeStruct((batch, MEM_HEADS, N_MEM, MEM_WIDTH), BF16)
    spec = pl.BlockSpec((1, MEM_HEADS, N_MEM, MEM_WIDTH), lambda b: (b, 0, 0, 0))
    return pl.pallas_call(
        _mem_kv_kernel,
        grid=(batch,),
        in_specs=[pl.BlockSpec((1, N_MEM, D_MODEL), lambda b: (b, 0, 0)),
                  _const_spec(nw.shape), _const_spec(w.shape)],
        out_specs=(spec, spec),
        out_shape=(out, out),
        compiler_params=_params(("arbitrary",)),
        name="mem_kv",
    )(mem, nw, w)


def _xattn_stages(x_ref, k_ref, v_ref, nq_ref, wq_ref, wo_ref, nf_ref, x2_ref, hf_ref):
    live = {}

    def queries():
        hq = _rms(x_ref[...], nq_ref[...]).astype(BF16)
        live["q"] = (_dot(hq, wq_ref[...]) * (MEM_HEAD_DIM ** -0.5)).astype(BF16)
        live["o"] = jnp.zeros((x_ref.shape[0], MEM_WIDTH), F32)

    def head(h):
        s = _dot_nt(live["q"], k_ref[0, h])
        e = jnp.exp(s - jnp.max(s, axis=-1, keepdims=True))
        inv = 1.0 / jnp.sum(e, axis=-1, keepdims=True)
        live["o"] = live["o"] + _dot(e.astype(BF16), v_ref[0, h]) * inv

    def project():
        x2 = x_ref[...] + _dot(live.pop("o").astype(BF16), wo_ref[...])
        x2_ref[...] = x2
        hf_ref[...] = _rms(x2, nf_ref[...]).astype(BF16)

    return [queries] + [functools.partial(head, h) for h in range(MEM_HEADS)] + [project]


def _ffn_stages(x2_ref, hf_ref, wg_ref, wu_ref, wd_ref, nfin_ref, act_ref, o_ref):
    live = {"ss": jnp.zeros((o_ref.shape[0], 1), F32)}

    def hidden(cs):
        hf = hf_ref[...]
        g = _dot(hf, wg_ref[:, cs])
        u = _dot(hf, wu_ref[:, cs])
        act_ref[:, cs] = (g * jax.nn.sigmoid(g) * u).astype(BF16)

    def down(cs):
        y = x2_ref[:, cs] + _dot(act_ref[...], wd_ref[:, cs])
        live["ss"] = live["ss"] + jnp.sum(y * y, axis=-1, keepdims=True)
        o_ref[:, cs] = y

    def final_norm():
        inv = lax.rsqrt(live["ss"] * (1.0 / D_MODEL) + EPS)
        o_ref[...] = o_ref[...] * inv * nfin_ref[...]

    hid = [slice(c * FF_CHUNK, (c + 1) * FF_CHUNK) for c in range(D_FF // FF_CHUNK)]
    out = [slice(c * MIX_CHUNK, (c + 1) * MIX_CHUNK) for c in range(D_MODEL // MIX_CHUNK)]
    return ([functools.partial(hidden, cs) for cs in hid]
            + [functools.partial(down, cs) for cs in out] + [final_norm])


def _tail_kernel(x_ref, k_ref, v_ref, nq_ref, wq_ref, wo_ref, nf_ref, wg_ref, wu_ref, wd_ref,
                 nfin_ref, o_ref, x2_scr, hf_scr, act_ref):
    i = pl.program_id(0)
    n_tiles = pl.num_programs(0) - 1
    slot_w = i % 2
    slot_r = 1 - slot_w

    def xattn():
        return _xattn_stages(x_ref, k_ref, v_ref, nq_ref, wq_ref, wo_ref, nf_ref,
                             x2_scr.at[slot_w], hf_scr.at[slot_w])

    def ffn():
        return _ffn_stages(x2_scr.at[slot_r], hf_scr.at[slot_r], wg_ref, wu_ref, wd_ref,
                           nfin_ref, act_ref, o_ref)

    def merge(lead, lag):
        if not lag:
            return lead
        n_hidden = D_FF // FF_CHUNK
        n_first = n_hidden - len(lead)
        return lag[:n_first] + _interleave(lag[n_first:n_hidden], lead) + lag[n_hidden:]

    _run_lookahead(i, n_tiles, xattn, ffn, merge)


def _tail(x2d, kext, vext, nq, wq, wo, nf, wg, wu, wd, nfin, seq):
    tm = TM_TAIL
    t = x2d.shape[0]
    n_tiles = t // tm
    tiles_per_seq = seq // tm
    cur = lambda i: jnp.minimum(i, n_tiles - 1)
    prv = lambda i: jnp.maximum(i - 1, 0)
    mem_spec = pl.BlockSpec((1, MEM_HEADS, N_MEM, MEM_WIDTH),
                            lambda i: (cur(i) // tiles_per_seq, 0, 0, 0))
    consts = (nq, wq, wo, nf, wg, wu, wd, nfin)
    return pl.pallas_call(
        _tail_kernel,
        grid=(n_tiles + 1,),
        in_specs=[pl.BlockSpec((tm, D_MODEL), lambda i: (cur(i), 0)), mem_spec, mem_spec]
        + [_const_spec(c.shape) for c in consts],
        out_specs=pl.BlockSpec((tm, D_MODEL), lambda i: (prv(i), 0)),
        out_shape=jax.ShapeDtypeStruct((t, D_MODEL), F32),
        scratch_shapes=[pltpu.VMEM((2, tm, D_MODEL), F32), pltpu.VMEM((2, tm, D_MODEL), BF16),
                        pltpu.VMEM((tm, D_FF), BF16)],
        compiler_params=_params(("arbitrary",)),
        name="tail",
    )(x2d, kext, vext, *consts)


def _layer(x2d, mem, batch, seq, norm_mix_w, w_in, b_gate, attn_sinks, gla_gate_w2, gla_gate_b,
           gla_norm_w, w_attn_o, w_gla_o, w_mix_o, norm_mem_q_w, norm_mem_kv_w, w_mem_q,
           w_mem_kv, w_mem_o, norm_ffn_w, w_ffn_gate, w_ffn_up, w_ffn_down, out_norm_w):
    w2_p = jnp.pad(gla_gate_w2, ((0, LANES - GLA_GATE_RANK), (0, 0)))
    bf = lambda a: a.astype(BF16)
    r2 = lambda a: a.reshape(1, -1)

    qa, ka, vat, qg, kg, vgt, gg, gk, gates = _in_proj(
        x2d, r2(norm_mix_w), bf(w_in.T), bf(w2_p), r2(gla_gate_b), r2(b_gate),
        r2(jnp.tile(gla_norm_w, GLA_HEADS)))
    group = SWA_HEADS // SWA_KV_HEADS
    wa = w_attn_o.reshape(SWA_KV_HEADS, group, SWA_HEAD_DIM, D_MODEL)
    wa = wa.transpose(1, 0, 2, 3).reshape(SWA_HEADS * SWA_HEAD_DIM, D_MODEL)
    x1, wg16, wu16, wd16 = _attn_mix(
        attn_sinks, qa, ka, vat, qg, kg, gk, vgt, gg, x2d, gates,
        bf(wa), bf(w_gla_o), bf(w_mix_o), (w_ffn_gate, w_ffn_up, w_ffn_down), seq)
    kext, vext = _mem_kv(mem, r2(norm_mem_kv_w), bf(w_mem_kv))
    return _tail(x1, kext, vext, r2(norm_mem_q_w), bf(w_mem_q), bf(w_mem_o), r2(norm_ffn_w),
                 wg16, wu16, wd16, r2(out_norm_w), seq)


def kernel(x, mem, norm_mix_w, w_in, b_gate, attn_sinks, gla_gate_w2, gla_gate_b, gla_norm_w,
           w_attn_o, w_gla_o, w_mix_o, norm_mem_q_w, norm_mem_kv_w, w_mem_q, w_mem_kv, w_mem_o,
           norm_ffn_w, w_ffn_gate, w_ffn_up, w_ffn_down, norm_final_w):
    batch, seq, d = x.shape
    depth = w_in.shape[0]
    assert depth == 1 and d == D_MODEL
    assert seq % TM_PROJ == 0 and seq % TM_ATTN == 0 and seq % TM_TAIL == 0
    out = _layer(x.reshape(batch * seq, d), mem, batch, seq, norm_mix_w[0], w_in[0], b_gate[0],
                 attn_sinks[0], gla_gate_w2[0], gla_gate_b[0], gla_norm_w[0], w_attn_o[0],
                 w_gla_o[0], w_mix_o[0], norm_mem_q_w[0], norm_mem_kv_w[0], w_mem_q[0],
                 w_mem_kv[0], w_mem_o[0], norm_ffn_w[0], w_ffn_gate[0], w_ffn_up[0],
                 w_ffn_down[0], norm_final_w)
    return out.reshape(batch, seq, d)
```

```python
import functools

import jax
import jax.numpy as jnp
from jax import lax
from jax.experimental import pallas as pl
from jax.experimental.pallas import tpu as pltpu

D_MODEL = 1024
CHUNK = 64
N_MEM = 256
EPS = 1e-6

SWA_HEADS = 16
SWA_KV_HEADS = 2
SWA_HEAD_DIM = 64
SWA_BLOCK = 128

GLA_HEADS = 4
GLA_KEY_DIM = D_MODEL // 2
GLA_VAL_DIM = D_MODEL
GLA_DK = GLA_KEY_DIM // GLA_HEADS
GLA_DV = GLA_VAL_DIM // GLA_HEADS
GLA_GATE_RANK = 16
GLA_GATE_NORM = 16.0

MEM_HEADS = 4
MEM_HEAD_DIM = 64
MEM_WIDTH = MEM_HEADS * MEM_HEAD_DIM

D_FF = -(-(8 * D_MODEL) // (3 * 256)) * 256

IN_SIZES = (SWA_HEADS * SWA_HEAD_DIM, SWA_KV_HEADS * SWA_HEAD_DIM, SWA_KV_HEADS * SWA_HEAD_DIM,
            GLA_KEY_DIM, GLA_KEY_DIM, GLA_VAL_DIM, GLA_VAL_DIM, GLA_GATE_RANK, 2 * D_MODEL)
IN_OFFSETS = tuple(sum(IN_SIZES[:i]) for i in range(len(IN_SIZES) + 1))

LANES = 128
VMEM_LIMIT = 56 * 1024 * 1024

LOG2E = 1.4426950408889634

TM_PROJ = 1024
TM_ATTN = 512
TM_TAIL = 512
MIX_CHUNK = 256
FF_CHUNK = 256

BF16 = jnp.bfloat16
F32 = jnp.float32


def _rms(x, w):
    return x * lax.rsqrt(jnp.mean(x * x, axis=-1, keepdims=True) + EPS) * w


def _dot(a, b):
    return jnp.dot(a, b, preferred_element_type=F32)


def _dot_nt(a, b):
    return lax.dot_general(a, b, (((1,), (1,)), ((), ())), preferred_element_type=F32)


def _const_spec(shape):
    zeros = (0,) * len(shape)
    return pl.BlockSpec(shape, lambda *_: zeros, pipeline_mode=pl.Buffered(1))


def _params(semantics):
    return pltpu.CompilerParams(dimension_semantics=semantics, vmem_limit_bytes=VMEM_LIMIT)


def _in_proj_kernel(x_ref, nw_ref, wt_ref, w2_ref, gb_ref, bg_ref, gn_ref,
                    qa_ref, ka_ref, vat_ref, qg_ref, kg_ref, vgt_ref, gg_ref, gk_ref, gate_ref):
    o_qa, o_ka, o_va, o_qg, o_kg, o_vg, o_gg, o_alr, o_gate, o_end = IN_OFFSETS
    hd = SWA_HEAD_DIM
    h = _rms(x_ref[...], nw_ref[...]).astype(BF16)
    half = (o_end - o_gate) // 2

    def branch_gates(lo):
        pre = _dot_nt(h, wt_ref[o_gate + lo:o_gate + lo + half, :]) + bg_ref[:, lo:lo + half]
        gate_ref[:, lo:lo + half] = jax.nn.sigmoid(pre).astype(BF16)

    branch_gates(0)
    group = SWA_HEADS // SWA_KV_HEADS
    wq = jnp.concatenate(
        [wt_ref[o_qa + (p + j * group) * hd:o_qa + (p + j * group + 1) * hd, :]
         for p in range(group) for j in range(SWA_KV_HEADS)], axis=0)
    qa_ref[...] = (_dot_nt(h, wq) * (hd ** -0.5 * LOG2E)).astype(BF16)
    branch_gates(half)
    vgt_ref[...] = _dot_nt(wt_ref[o_vg:o_gg, :], h).astype(BF16)
    g = _dot_nt(h, wt_ref[o_gg:o_alr, :])
    gg_ref[...] = (g * jax.nn.sigmoid(g) * gn_ref[...]).astype(BF16)
    kw = SWA_KV_HEADS * hd
    k_alr = _dot_nt(h, jnp.concatenate([wt_ref[o_ka:o_va, :], wt_ref[o_alr:o_alr + LANES, :]],
                                       axis=0))
    ka_ref[...] = k_alr[:, :kw].astype(BF16)
    qg_ref[...] = _dot_nt(h, wt_ref[o_qg:o_kg, :]).astype(BF16)
    z = _dot(k_alr[:, kw:].astype(BF16), w2_ref[...]) + gb_ref[...]
    log_sig = jnp.minimum(z, 0.0) - jnp.log(1.0 + jnp.exp(-jnp.abs(z)))
    gk_ref[...] = (log_sig * (1.0 / GLA_GATE_NORM)).astype(BF16)
    kg_ref[...] = _dot_nt(h, wt_ref[o_kg:o_vg, :]).astype(BF16)
    vat_ref[...] = _dot_nt(wt_ref[o_va:o_qg, :], h).astype(BF16)


def _in_proj(x2d, nw, wt, w2, gb, bg, gn):
    t = x2d.shape[0]
    tm = TM_PROJ
    row = lambda n: pl.BlockSpec((tm, n), lambda i: (i, 0))
    col = lambda n: pl.BlockSpec((n, tm), lambda i: (0, i))
    consts = (nw, wt, w2, gb, bg, gn)
    kw = SWA_KV_HEADS * SWA_HEAD_DIM
    tok = lambda n, dt=BF16: jax.ShapeDtypeStruct((t, n), dt)
    out_shape = (
        tok(SWA_HEADS * SWA_HEAD_DIM), tok(kw), jax.ShapeDtypeStruct((kw, t), BF16),
        tok(GLA_KEY_DIM), tok(GLA_KEY_DIM), jax.ShapeDtypeStruct((GLA_VAL_DIM, t), BF16),
        tok(GLA_VAL_DIM), tok(GLA_KEY_DIM), tok(2 * D_MODEL),
    )
    out_specs = (
        row(SWA_HEADS * SWA_HEAD_DIM), row(kw), col(kw),
        row(GLA_KEY_DIM), row(GLA_KEY_DIM), col(GLA_VAL_DIM),
        row(GLA_VAL_DIM), row(GLA_KEY_DIM), row(2 * D_MODEL),
    )
    return pl.pallas_call(
        _in_proj_kernel,
        grid=(t // tm,),
        in_specs=[row(D_MODEL)] + [_const_spec(c.shape) for c in consts],
        out_specs=out_specs,
        out_shape=out_shape,
        compiler_params=_params(("arbitrary",)),
        name="in_proj",
    )(x2d, *consts)


def _swa_stages(sink_ref, q_ref, k_ref, kp_ref, vt_ref, vtp_ref, seq_start, write):
    blk = SWA_BLOCK
    win = blk + CHUNK
    low = lax.broadcasted_iota(jnp.int32, (CHUNK, LANES), 1) < SWA_HEAD_DIM
    low_row = lax.broadcasted_iota(jnp.int32, (1, LANES), 1) < SWA_HEAD_DIM
    prev_bias = jnp.where(seq_start, -1e30, 0.0).astype(F32)
    k_all = jnp.concatenate([kp_ref[...], k_ref[...]], axis=0)
    vt_all = jnp.concatenate([vtp_ref[...], vt_ref[...]], axis=1)
    zeros_chunk = jnp.zeros((CHUNK, LANES), BF16)
    n_pairs = SWA_HEADS // SWA_KV_HEADS
    chunks = [(u, qh) for u in range(q_ref.shape[0] // blk) for qh in range(blk // CHUNK)]
    live = {}

    def scores(p):
        res = []
        for u, qh in chunks:
            r0 = u * blk + qh * CHUNK
            qp = q_ref[r0:r0 + CHUNK, p * LANES:(p + 1) * LANES]
            zero = jnp.zeros_like(qp)
            qsel = jnp.concatenate([jnp.where(low, qp, zero), jnp.where(low, zero, qp)], axis=0)
            res.append(_dot_nt(k_all[r0:r0 + win], qsel))
        live["s", p] = res

    def softmax_values(p):
        sink_row = jnp.where(low_row, sink_ref[p], sink_ref[p + n_pairs]) * LOG2E
        res = []
        for (u, qh), st in zip(chunks, live.pop(("s", p))):
            if u == 0:
                n_prev = blk - qh * CHUNK
                st = jnp.concatenate([st[:n_prev] + prev_bias, st[n_prev:]], axis=0)
            m = jnp.maximum(jnp.max(st, axis=0, keepdims=True), sink_row)
            e = jnp.exp2(st - m)
            denom = jnp.sum(e, axis=0, keepdims=True) + jnp.exp2(sink_row - m)
            e_full = jnp.concatenate([zeros_chunk] * qh + [e.astype(BF16)]
                                     + [zeros_chunk] * (1 - qh), axis=0)
            res.append(_dot(vt_all[:, u * blk:(u + 2) * blk], e_full) * (1.0 / denom))
        live["o", p] = res

    def store(p):
        for (u, qh), ot in zip(chunks, live.pop(("o", p))):
            r0 = u * blk + qh * CHUNK
            o2 = ot.T
            write(slice(r0, r0 + CHUNK), slice(p * LANES, (p + 1) * LANES),
                  jnp.where(low, o2[:CHUNK], o2[CHUNK:]).astype(BF16))

    return [(functools.partial(scores, p), functools.partial(softmax_values, p),
             functools.partial(store, p)) for p in range(n_pairs)]


def _gla_stages(q_ref, k_ref, gk_ref, vt_ref, g_ref, st_ref, seq_start, write):
    pair = 2 * CHUNK
    heads = range(GLA_HEADS)
    ks = [slice(h * GLA_DK, (h + 1) * GLA_DK) for h in heads]
    vs = [slice(h * GLA_DV, (h + 1) * GLA_DV) for h in heads]
    r = lax.broadcasted_iota(jnp.int32, (pair, pair), 0)
    c = lax.broadcasted_iota(jnp.int32, (pair, pair), 1)
    tri = ((c <= r) & ((r // CHUNK) == (c // CHUNK))).astype(BF16)
    first = lax.broadcasted_iota(jnp.int32, (pair, GLA_DK), 0) < CHUNK
    eps_scaled = EPS * GLA_DK
    n_pairs = q_ref.shape[0] // pair
    sts = [jnp.where(seq_start, jnp.zeros((GLA_DV, GLA_DK), F32), st_ref[h]) for h in heads]
    live = {}

    def decay_and_kv(i):
        sl = slice(i * pair, (i + 1) * pair)
        b_all = _dot(tri, gk_ref[sl, :])
        res = []
        for h in heads:
            b = b_all[:, ks[h]]
            b_end0 = b[CHUNK - 1:CHUNK, :]
            b_end1 = b[pair - 1:pair, :]
            k_dec = (k_ref[sl, ks[h]].astype(F32)
                     * jnp.exp(jnp.where(first, b_end0, b_end1) - b)).astype(BF16)
            zero = jnp.zeros_like(k_dec)
            kd = jnp.concatenate([jnp.where(first, k_dec, zero), jnp.where(first, zero, k_dec)],
                                 axis=1)
            kv = _dot(vt_ref[vs[h], sl], kd)
            res.append((jnp.exp(b_end0), jnp.exp(b_end1), kv))
        live[i] = res

    def state_and_out(i):
        sl = slice(i * pair, (i + 1) * pair)
        dk = live.pop(i)
        outs = []
        for h in heads:
            a0, a1, kv = dk[h]
            st0 = sts[h] * a0 + kv[:, :GLA_DK]
            st1 = st0 * a1 + kv[:, GLA_DK:]
            sts[h] = st1
            q = q_ref[sl, ks[h]]
            outs.append(jnp.concatenate([_dot_nt(q[:CHUNK], st0.astype(BF16)),
                                         _dot_nt(q[CHUNK:], st1.astype(BF16))], axis=0))
        for h in heads:
            o = outs[h]
            inv = lax.rsqrt(jnp.mean(o * o, axis=-1, keepdims=True) + eps_scaled)
            write(sl, vs[h], (o * inv).astype(BF16) * g_ref[sl, vs[h]])

    def save_state():
        for h in heads:
            st_ref[h] = sts[h]

    stages = [functools.partial(decay_and_kv, 0)]
    for i in range(n_pairs):
        if i + 1 < n_pairs:
            stages.append(functools.partial(decay_and_kv, i + 1))
        stages.append(functools.partial(state_and_out, i))
    stages.append(save_state)
    return stages


def _mix_stages(x_ref, gate_ref, wa_ref, wg_ref, wm_ref, oa_ref, og_ref, mg_ref, o_ref):
    pieces = [slice(c * MIX_CHUNK, (c + 1) * MIX_CHUNK) for c in range(D_MODEL // MIX_CHUNK)]

    def merge(cs):
        ya = _dot(oa_ref[...], wa_ref[:, cs])
        yg = _dot(og_ref[...], wg_ref[:, cs])
        cs_b = slice(D_MODEL + cs.start, D_MODEL + cs.stop)
        mg_ref[:, cs] = (gate_ref[:, cs].astype(F32) * ya
                         + gate_ref[:, cs_b].astype(F32) * yg).astype(BF16)

    def project(cs):
        o_ref[:, cs] = x_ref[:, cs] + _dot(mg_ref[...], wm_ref[:, cs])

    return ([functools.partial(merge, cs) for cs in pieces]
            + [functools.partial(project, cs) for cs in pieces])


def _interleave(major, minor):
    out, done = [], 0
    for n, stage in enumerate(major):
        out.append(stage)
        want = (n + 1) * len(minor) // len(major)
        out.extend(minor[done:want])
        done = want
    return out


def _run_lookahead(i, n_tiles, lead, lag, merge, init=None):
    @pl.when(i == 0)
    def _():
        if init is not None:
            init()
        for stage in merge(lead(), []):
            stage()

    @pl.when(jnp.logical_and(i > 0, i < n_tiles))
    def _():
        for stage in merge(lead(), lag()):
            stage()

    @pl.when(i == n_tiles)
    def _():
        for stage in merge([], lag()):
            stage()


def _attn_mix_kernel(tiles_per_seq, n_cast,
                     sink_ref, q_ref, k_ref, kp_ref, vt_ref, vtp_ref,
                     qg_ref, kg_ref, gk_ref, vgt_ref, gg_ref,
                     x_ref, gate_ref, wa_ref, wg_ref, wm_ref, *rest):
    cast_src, (o_ref, *cast_dst) = rest[:n_cast], rest[n_cast:2 * n_cast + 1]
    oa_scr, og_scr, mg_scr, st_ref = rest[2 * n_cast + 1:]
    i = pl.program_id(0)
    n_tiles = pl.num_programs(0) - 1
    seq_start = (i % tiles_per_seq) == 0
    slot_w = i % 2
    slot_r = 1 - slot_w

    def write_oa(rows, cols, val):
        oa_scr[slot_w, rows, cols] = val

    def write_og(rows, cols, val):
        og_scr[slot_w, rows, cols] = val

    def mixers():
        swa = _swa_stages(sink_ref, q_ref, k_ref, kp_ref, vt_ref, vtp_ref, seq_start, write_oa)
        gla = _gla_stages(qg_ref, kg_ref, gk_ref, vgt_ref, gg_ref, st_ref, seq_start, write_og)
        n = len(swa)
        order = [swa[0][0]]
        for p in range(n):
            if p + 1 < n:
                order.append(swa[p + 1][0])
            order.append(swa[p][1])
            if p > 0:
                order.append(swa[p - 1][2])
        order.append(swa[n - 1][2])
        return _interleave(order, gla) + [convert_weights]

    def projections():
        return _mix_stages(x_ref, gate_ref, wa_ref, wg_ref, wm_ref,
                           oa_scr.at[slot_r], og_scr.at[slot_r], mg_scr, o_ref)

    def merge(lead, lag):
        return _interleave(lead, lag) if lead else lag

    def convert_weights():
        for src, dst in zip(cast_src, cast_dst):
            dst[...] = src[...].astype(BF16)

    def init():
        st_ref[...] = jnp.zeros_like(st_ref)

    _run_lookahead(i, n_tiles, mixers, projections, merge, init)


def _attn_mix(sinks, qa, ka, vat, qg, kg, gk, vgt, gg, x2d, gates, wa, wg, wm, cast_w, seq):
    t = x2d.shape[0]
    tm = TM_ATTN
    n_tiles = t // tm
    blocks = tm // SWA_BLOCK
    kw = ka.shape[1]
    cur = lambda i: jnp.minimum(i, n_tiles - 1)
    prv = lambda i: jnp.maximum(i - 1, 0)
    prev_block = lambda i: jnp.maximum(cur(i) * blocks - 1, 0)
    tok = lambda n: pl.BlockSpec((tm, n), lambda i: (cur(i), 0))
    col = lambda n: pl.BlockSpec((n, tm), lambda i: (0, cur(i)))
    old = lambda n: pl.BlockSpec((tm, n), lambda i: (prv(i), 0))

    def slab(w):
        n_slabs = max(n for n in range(1, n_tiles + 1) if w.shape[0] % (16 * n) == 0)
        return pl.BlockSpec((w.shape[0] // n_slabs, w.shape[1]),
                            lambda i: (jnp.minimum(i, n_slabs - 1), 0))

    cast_specs = [slab(w) for w in cast_w]
    return pl.pallas_call(
        functools.partial(_attn_mix_kernel, seq // tm, len(cast_w)),
        grid=(n_tiles + 1,),
        in_specs=[
            pl.BlockSpec(memory_space=pltpu.SMEM),
            tok(qa.shape[1]), tok(kw),
            pl.BlockSpec((SWA_BLOCK, kw), lambda i: (prev_block(i), 0)),
            col(kw),
            pl.BlockSpec((kw, SWA_BLOCK), lambda i: (0, prev_block(i))),
            tok(GLA_KEY_DIM), tok(GLA_KEY_DIM), tok(GLA_KEY_DIM), col(GLA_VAL_DIM),
            tok(GLA_VAL_DIM),
            old(D_MODEL), old(2 * D_MODEL),
            _const_spec(wa.shape), _const_spec(wg.shape), _const_spec(wm.shape),
        ] + cast_specs,
        out_specs=[old(D_MODEL)] + cast_specs,
        out_shape=[jax.ShapeDtypeStruct((t, D_MODEL), F32)]
        + [jax.ShapeDtypeStruct(w.shape, BF16) for w in cast_w],
        scratch_shapes=[
            pltpu.VMEM((2, tm, SWA_HEADS * SWA_HEAD_DIM), BF16),
            pltpu.VMEM((2, tm, GLA_VAL_DIM), BF16),
            pltpu.VMEM((tm, D_MODEL), BF16),
            pltpu.VMEM((GLA_HEADS, GLA_DV, GLA_DK), F32),
        ],
        compiler_params=_params(("arbitrary",)),
        name="attn_mix",
    )(sinks, qa, ka, ka, vat, vat, qg, kg, gk, vgt, gg, x2d, gates, wa, wg, wm, *cast_w)


def _mem_kv_kernel(m_ref, nw_ref, w_ref, k_ref, v_ref):
    mn = _rms(m_ref[0], nw_ref[...]).astype(BF16)
    kv = _dot(mn, w_ref[...])
    k = kv[:, :MEM_WIDTH].astype(BF16)
    v = kv[:, MEM_WIDTH:].astype(BF16)
    head = lax.broadcasted_iota(jnp.int32, (N_MEM, MEM_WIDTH), 1) // MEM_HEAD_DIM
    zero = jnp.zeros_like(k)
    for h in range(MEM_HEADS):
        k_ref[0, h] = jnp.where(head == h, k, zero)
        v_ref[0, h] = jnp.where(head == h, v, zero)


def _mem_kv(mem, nw, w):
    batch = mem.shape[0]
    out = jax.ShapeDtypeStruct((batch, MEM_HEADS, N_MEM, MEM_WIDTH), BF16)
    spec = pl.BlockSpec((1, MEM_HEADS, N_MEM, MEM_WIDTH), lambda b: (b, 0, 0, 0))
    return pl.pallas_call(
        _mem_kv_kernel,
        grid=(batch,),
        in_specs=[pl.BlockSpec((1, N_MEM, D_MODEL), lambda b: (b, 0, 0)),
                  _const_spec(nw.shape), _const_spec(w.shape)],
        out_specs=(spec, spec),
        out_shape=(out, out),
        compiler_params=_params(("arbitrary",)),
        name="mem_kv",
    )(mem, nw, w)


def _xattn_stages(x_ref, k_ref, v_ref, nq_ref, wq_ref, wo_ref, nf_ref, x2_ref, hf_ref):
    live = {}

    def queries():
        hq = _rms(x_ref[...], nq_ref[...]).astype(BF16)
        live["q"] = (_dot(hq, wq_ref[...]) * (MEM_HEAD_DIM ** -0.5)).astype(BF16)
        live["o"] = jnp.zeros((x_ref.shape[0], MEM_WIDTH), F32)

    def head(h):
        s = _dot_nt(live["q"], k_ref[0, h])
        e = jnp.exp(s - jnp.max(s, axis=-1, keepdims=True))
        inv = 1.0 / jnp.sum(e, axis=-1, keepdims=True)
        live["o"] = live["o"] + _dot(e.astype(BF16), v_ref[0, h]) * inv

    def project():
        x2 = x_ref[...] + _dot(live.pop("o").astype(BF16), wo_ref[...])
        x2_ref[...] = x2
        hf_ref[...] = _rms(x2, nf_ref[...]).astype(BF16)

    return [queries] + [functools.partial(head, h) for h in range(MEM_HEADS)] + [project]


def _ffn_stages(x2_ref, hf_ref, wg_ref, wu_ref, wd_ref, nfin_ref, act_ref, o_ref):
    live = {"ss": jnp.zeros((o_ref.shape[0], 1), F32)}

    def hidden(cs):
        hf = hf_ref[...]
        g = _dot(hf, wg_ref[:, cs])
        u = _dot(hf, wu_ref[:, cs])
        act_ref[:, cs] = (g * jax.nn.sigmoid(g) * u).astype(BF16)

    def down(cs):
        y = x2_ref[:, cs] + _dot(act_ref[...], wd_ref[:, cs])
        live["ss"] = live["ss"] + jnp.sum(y * y, axis=-1, keepdims=True)
        o_ref[:, cs] = y

    def final_norm():
        inv = lax.rsqrt(live["ss"] * (1.0 / D_MODEL) + EPS)
        o_ref[...] = o_ref[...] * inv * nfin_ref[...]

    hid = [slice(c * FF_CHUNK, (c + 1) * FF_CHUNK) for c in range(D_FF // FF_CHUNK)]
    out = [slice(c * MIX_CHUNK, (c + 1) * MIX_CHUNK) for c in range(D_MODEL // MIX_CHUNK)]
    return ([functools.partial(hidden, cs) for cs in hid]
            + [functools.partial(down, cs) for cs in out] + [final_norm])


def _tail_kernel(x_ref, k_ref, v_ref, nq_ref, wq_ref, wo_ref, nf_ref, wg_ref, wu_ref, wd_ref,
                 nfin_ref, o_ref, x2_scr, hf_scr, act_ref):
    i = pl.program_id(0)
    n_tiles = pl.num_programs(0) - 1
    slot_w = i % 2
    slot_r = 1 - slot_w

    def xattn():
        return _xattn_stages(x_ref, k_ref, v_ref, nq_ref, wq_ref, wo_ref, nf_ref,
                             x2_scr.at[slot_w], hf_scr.at[slot_w])

    def ffn():
        return _ffn_stages(x2_scr.at[slot_r], hf_scr.at[slot_r], wg_ref, wu_ref, wd_ref,
                           nfin_ref, act_ref, o_ref)

    def merge(lead, lag):
        if not lag:
            return lead
        n_hidden = D_FF // FF_CHUNK
        n_first = n_hidden - len(lead)
        return lag[:n_first] + _interleave(lag[n_first:n_hidden], lead) + lag[n_hidden:]

    _run_lookahead(i, n_tiles, xattn, ffn, merge)


def _tail(x2d, kext, vext, nq, wq, wo, nf, wg, wu, wd, nfin, seq):
    tm = TM_TAIL
    t = x2d.shape[0]
    n_tiles = t // tm
    tiles_per_seq = seq // tm
    cur = lambda i: jnp.minimum(i, n_tiles - 1)
    prv = lambda i: jnp.maximum(i - 1, 0)
    mem_spec = pl.BlockSpec((1, MEM_HEADS, N_MEM, MEM_WIDTH),
                            lambda i: (cur(i) // tiles_per_seq, 0, 0, 0))
    consts = (nq, wq, wo, nf, wg, wu, wd, nfin)
    return pl.pallas_call(
        _tail_kernel,
        grid=(n_tiles + 1,),
        in_specs=[pl.BlockSpec((tm, D_MODEL), lambda i: (cur(i), 0)), mem_spec, mem_spec]
        + [_const_spec(c.shape) for c in consts],
        out_specs=pl.BlockSpec((tm, D_MODEL), lambda i: (prv(i), 0)),
        out_shape=jax.ShapeDtypeStruct((t, D_MODEL), F32),
        scratch_shapes=[pltpu.VMEM((2, tm, D_MODEL), F32), pltpu.VMEM((2, tm, D_MODEL), BF16),
                        pltpu.VMEM((tm, D_FF), BF16)],
        compiler_params=_params(("arbitrary",)),
        name="tail",
    )(x2d, kext, vext, *consts)


def _layer(x2d, mem, batch, seq, norm_mix_w, w_in, b_gate, attn_sinks, gla_gate_w2, gla_gate_b,
           gla_norm_w, w_attn_o, w_gla_o, w_mix_o, norm_mem_q_w, norm_mem_kv_w, w_mem_q,
           w_mem_kv, w_mem_o, norm_ffn_w, w_ffn_gate, w_ffn_up, w_ffn_down, out_norm_w):
    w2_p = jnp.pad(gla_gate_w2, ((0, LANES - GLA_GATE_RANK), (0, 0)))
    bf = lambda a: a.astype(BF16)
    r2 = lambda a: a.reshape(1, -1)

    qa, ka, vat, qg, kg, vgt, gg, gk, gates = _in_proj(
        x2d, r2(norm_mix_w), bf(w_in.T), bf(w2_p), r2(gla_gate_b), r2(b_gate),
        r2(jnp.tile(gla_norm_w, GLA_HEADS)))
    group = SWA_HEADS // SWA_KV_HEADS
    wa = w_attn_o.reshape(SWA_KV_HEADS, group, SWA_HEAD_DIM, D_MODEL)
    wa = wa.transpose(1, 0, 2, 3).reshape(SWA_HEADS * SWA_HEAD_DIM, D_MODEL)
    x1, wg16, wu16, wd16, wq16, wo16, wkv16 = _attn_mix(
        attn_sinks, qa, ka, vat, qg, kg, gk, vgt, gg, x2d, gates,
        bf(wa), bf(w_gla_o), bf(w_mix_o),
        (w_ffn_gate, w_ffn_up, w_ffn_down, w_mem_q, w_mem_o, w_mem_kv), seq)
    kext, vext = _mem_kv(mem, r2(norm_mem_kv_w), wkv16)
    return _tail(x1, kext, vext, r2(norm_mem_q_w), wq16, wo16, r2(norm_ffn_w),
                 wg16, wu16, wd16, r2(out_norm_w), seq)


def kernel(x, mem, norm_mix_w, w_in, b_gate, attn_sinks, gla_gate_w2, gla_gate_b, gla_norm_w,
           w_attn_o, w_gla_o, w_mix_o, norm_mem_q_w, norm_mem_kv_w, w_mem_q, w_mem_kv, w_mem_o,
           norm_ffn_w, w_ffn_gate, w_ffn_up, w_ffn_down, norm_final_w):
    batch, seq, d = x.shape
    depth = w_in.shape[0]
    assert depth == 1 and d == D_MODEL
    assert seq % TM_PROJ == 0 and seq % TM_ATTN == 0 and seq % TM_TAIL == 0
    out = _layer(x.reshape(batch * seq, d), mem, batch, seq, norm_mix_w[0], w_in[0], b_gate[0],
                 attn_sinks[0], gla_gate_w2[0], gla_gate_b[0], gla_norm_w[0], w_attn_o[0],
                 w_gla_o[0], w_mix_o[0], norm_mem_q_w[0], norm_mem_kv_w[0], w_mem_q[0],
                 w_mem_kv[0], w_mem_o[0], norm_ffn_w[0], w_ffn_gate[0], w_ffn_up[0],
                 w_ffn_down[0], norm_final_w)
    return out.reshape(batch, seq, d)
```

```python
import functools

import jax
import jax.numpy as jnp
from jax import lax
from jax.experimental import pallas as pl
from jax.experimental.pallas import tpu as pltpu

D_MODEL = 1024
CHUNK = 64
N_MEM = 256
EPS = 1e-6

SWA_HEADS = 16
SWA_KV_HEADS = 2
SWA_HEAD_DIM = 64
SWA_BLOCK = 128

GLA_HEADS = 4
GLA_KEY_DIM = D_MODEL // 2
GLA_VAL_DIM = D_MODEL
GLA_DK = GLA_KEY_DIM // GLA_HEADS
GLA_DV = GLA_VAL_DIM // GLA_HEADS
GLA_GATE_RANK = 16
GLA_GATE_NORM = 16.0

MEM_HEADS = 4
MEM_HEAD_DIM = 64
MEM_WIDTH = MEM_HEADS * MEM_HEAD_DIM

D_FF = -(-(8 * D_MODEL) // (3 * 256)) * 256

IN_SIZES = (SWA_HEADS * SWA_HEAD_DIM, SWA_KV_HEADS * SWA_HEAD_DIM, SWA_KV_HEADS * SWA_HEAD_DIM,
            GLA_KEY_DIM, GLA_KEY_DIM, GLA_VAL_DIM, GLA_VAL_DIM, GLA_GATE_RANK, 2 * D_MODEL)
IN_OFFSETS = tuple(sum(IN_SIZES[:i]) for i in range(len(IN_SIZES) + 1))

LANES = 128
VMEM_LIMIT = 56 * 1024 * 1024

LOG2E = 1.4426950408889634

TM_PROJ = 1024
TM_ATTN = 512
TM_TAIL = 512
MIX_CHUNK = 256
FF_CHUNK = 256

BF16 = jnp.bfloat16
F32 = jnp.float32


def _rms(x, w):
    return x * lax.rsqrt(jnp.mean(x * x, axis=-1, keepdims=True) + EPS) * w


def _dot(a, b):
    return jnp.dot(a, b, preferred_element_type=F32)


def _dot_nt(a, b):
    return lax.dot_general(a, b, (((1,), (1,)), ((), ())), preferred_element_type=F32)


def _const_spec(shape):
    zeros = (0,) * len(shape)
    return pl.BlockSpec(shape, lambda *_: zeros, pipeline_mode=pl.Buffered(1))


def _params(semantics):
    return pltpu.CompilerParams(dimension_semantics=semantics, vmem_limit_bytes=VMEM_LIMIT)


def _in_proj_kernel(x_ref, nw_ref, wt_ref, w2_ref, gb_ref, gn_ref,
                    h_ref, qa_ref, ka_ref, vat_ref, qg_ref, kg_ref, vgt_ref, gg_ref, gk_ref):
    o_qa, o_ka, o_va, o_qg, o_kg, o_vg, o_gg, o_alr, _, _ = IN_OFFSETS
    hd = SWA_HEAD_DIM
    h = _rms(x_ref[...], nw_ref[...]).astype(BF16)
    h_ref[...] = h
    group = SWA_HEADS // SWA_KV_HEADS
    wq = jnp.concatenate(
        [wt_ref[o_qa + (p + j * group) * hd:o_qa + (p + j * group + 1) * hd, :]
         for p in range(group) for j in range(SWA_KV_HEADS)], axis=0)
    qa_ref[...] = (_dot_nt(h, wq) * (hd ** -0.5 * LOG2E)).astype(BF16)
    vgt_ref[...] = _dot_nt(wt_ref[o_vg:o_gg, :], h).astype(BF16)
    g = _dot_nt(h, wt_ref[o_gg:o_alr, :])
    gg_ref[...] = (g * jax.nn.sigmoid(g) * gn_ref[...]).astype(BF16)
    kw = SWA_KV_HEADS * hd
    k_alr = _dot_nt(h, jnp.concatenate([wt_ref[o_ka:o_va, :], wt_ref[o_alr:o_alr + LANES, :]],
                                       axis=0))
    ka_ref[...] = k_alr[:, :kw].astype(BF16)
    qg_ref[...] = _dot_nt(h, wt_ref[o_qg:o_kg, :]).astype(BF16)
    z = _dot(k_alr[:, kw:].astype(BF16), w2_ref[...]) + gb_ref[...]
    log_sig = jnp.minimum(z, 0.0) - jnp.log(1.0 + jnp.exp(-jnp.abs(z)))
    gk_ref[...] = (log_sig * (1.0 / GLA_GATE_NORM)).astype(BF16)
    kg_ref[...] = _dot_nt(h, wt_ref[o_kg:o_vg, :]).astype(BF16)
    vat_ref[...] = _dot_nt(wt_ref[o_va:o_qg, :], h).astype(BF16)


def _in_proj(x2d, nw, wt, w2, gb, gn):
    t = x2d.shape[0]
    tm = TM_PROJ
    row = lambda n: pl.BlockSpec((tm, n), lambda i: (i, 0))
    col = lambda n: pl.BlockSpec((n, tm), lambda i: (0, i))
    consts = (nw, wt, w2, gb, gn)
    kw = SWA_KV_HEADS * SWA_HEAD_DIM
    tok = lambda n: jax.ShapeDtypeStruct((t, n), BF16)
    out_shape = (
        tok(D_MODEL), tok(SWA_HEADS * SWA_HEAD_DIM), tok(kw), jax.ShapeDtypeStruct((kw, t), BF16),
        tok(GLA_KEY_DIM), tok(GLA_KEY_DIM), jax.ShapeDtypeStruct((GLA_VAL_DIM, t), BF16),
        tok(GLA_VAL_DIM), tok(GLA_KEY_DIM),
    )
    out_specs = (
        row(D_MODEL), row(SWA_HEADS * SWA_HEAD_DIM), row(kw), col(kw),
        row(GLA_KEY_DIM), row(GLA_KEY_DIM), col(GLA_VAL_DIM),
        row(GLA_VAL_DIM), row(GLA_KEY_DIM),
    )
    return pl.pallas_call(
        _in_proj_kernel,
        grid=(t // tm,),
        in_specs=[row(D_MODEL)] + [_const_spec(c.shape) for c in consts],
        out_specs=out_specs,
        out_shape=out_shape,
        compiler_params=_params(("arbitrary",)),
        name="in_proj",
    )(x2d, *consts)


def _swa_stages(sink_ref, q_ref, k_ref, kp_ref, vt_ref, vtp_ref, seq_start, write):
    blk = SWA_BLOCK
    win = blk + CHUNK
    low = lax.broadcasted_iota(jnp.int32, (CHUNK, LANES), 1) < SWA_HEAD_DIM
    low_row = lax.broadcasted_iota(jnp.int32, (1, LANES), 1) < SWA_HEAD_DIM
    prev_bias = jnp.where(seq_start, -1e30, 0.0).astype(F32)
    k_all = jnp.concatenate([kp_ref[...], k_ref[...]], axis=0)
    vt_all = jnp.concatenate([vtp_ref[...], vt_ref[...]], axis=1)
    zeros_chunk = jnp.zeros((CHUNK, LANES), BF16)
    n_pairs = SWA_HEADS // SWA_KV_HEADS
    chunks = [(u, qh) for u in range(q_ref.shape[0] // blk) for qh in range(blk // CHUNK)]
    live = {}

    def scores(p):
        res = []
        for u, qh in chunks:
            r0 = u * blk + qh * CHUNK
            qp = q_ref[r0:r0 + CHUNK, p * LANES:(p + 1) * LANES]
            zero = jnp.zeros_like(qp)
            qsel = jnp.concatenate([jnp.where(low, qp, zero), jnp.where(low, zero, qp)], axis=0)
            res.append(_dot_nt(k_all[r0:r0 + win], qsel))
        live["s", p] = res

    def softmax_values(p):
        sink_row = jnp.where(low_row, sink_ref[p], sink_ref[p + n_pairs]) * LOG2E
        res = []
        for (u, qh), st in zip(chunks, live.pop(("s", p))):
            if u == 0:
                n_prev = blk - qh * CHUNK
                st = jnp.concatenate([st[:n_prev] + prev_bias, st[n_prev:]], axis=0)
            m = jnp.maximum(jnp.max(st, axis=0, keepdims=True), sink_row)
            e = jnp.exp2(st - m)
            denom = jnp.sum(e, axis=0, keepdims=True) + jnp.exp2(sink_row - m)
            e_full = jnp.concatenate([zeros_chunk] * qh + [e.astype(BF16)]
                                     + [zeros_chunk] * (1 - qh), axis=0)
            res.append(_dot(vt_all[:, u * blk:(u + 2) * blk], e_full) * (1.0 / denom))
        live["o", p] = res

    def store(p):
        for (u, qh), ot in zip(chunks, live.pop(("o", p))):
            r0 = u * blk + qh * CHUNK
            o2 = ot.T
            write(slice(r0, r0 + CHUNK), slice(p * LANES, (p + 1) * LANES),
                  jnp.where(low, o2[:CHUNK], o2[CHUNK:]).astype(BF16))

    return [(functools.partial(scores, p), functools.partial(softmax_values, p),
             functools.partial(store, p)) for p in range(n_pairs)]


def _gla_stages(q_ref, k_ref, gk_ref, vt_ref, g_ref, st_ref, seq_start, write):
    pair = 2 * CHUNK
    heads = range(GLA_HEADS)
    ks = [slice(h * GLA_DK, (h + 1) * GLA_DK) for h in heads]
    vs = [slice(h * GLA_DV, (h + 1) * GLA_DV) for h in heads]
    r = lax.broadcasted_iota(jnp.int32, (pair, pair), 0)
    c = lax.broadcasted_iota(jnp.int32, (pair, pair), 1)
    tri = ((c <= r) & ((r // CHUNK) == (c // CHUNK))).astype(BF16)
    first = lax.broadcasted_iota(jnp.int32, (pair, GLA_DK), 0) < CHUNK
    eps_scaled = EPS * GLA_DK
    n_pairs = q_ref.shape[0] // pair
    sts = [jnp.where(seq_start, jnp.zeros((GLA_DV, GLA_DK), F32), st_ref[h]) for h in heads]
    live = {}

    def decay_and_kv(i):
        sl = slice(i * pair, (i + 1) * pair)
        b_all = _dot(tri, gk_ref[sl, :])
        res = []
        for h in heads:
            b = b_all[:, ks[h]]
            b_end0 = b[CHUNK - 1:CHUNK, :]
            b_end1 = b[pair - 1:pair, :]
            k_dec = (k_ref[sl, ks[h]].astype(F32)
                     * jnp.exp(jnp.where(first, b_end0, b_end1) - b)).astype(BF16)
            zero = jnp.zeros_like(k_dec)
            kd = jnp.concatenate([jnp.where(first, k_dec, zero), jnp.where(first, zero, k_dec)],
                                 axis=1)
            kv = _dot(vt_ref[vs[h], sl], kd)
            res.append((jnp.exp(b_end0), jnp.exp(b_end1), kv))
        live[i] = res

    def state_and_out(i):
        sl = slice(i * pair, (i + 1) * pair)
        dk = live.pop(i)
        outs = []
        for h in heads:
            a0, a1, kv = dk[h]
            st0 = sts[h] * a0 + kv[:, :GLA_DK]
            st1 = st0 * a1 + kv[:, GLA_DK:]
            sts[h] = st1
            q = q_ref[sl, ks[h]]
            outs.append(jnp.concatenate([_dot_nt(q[:CHUNK], st0.astype(BF16)),
                                         _dot_nt(q[CHUNK:], st1.astype(BF16))], axis=0))
        for h in heads:
            o = outs[h]
            inv = lax.rsqrt(jnp.mean(o * o, axis=-1, keepdims=True) + eps_scaled)
            write(sl, vs[h], (o * inv).astype(BF16) * g_ref[sl, vs[h]])

    def save_state():
        for h in heads:
            st_ref[h] = sts[h]

    stages = [functools.partial(decay_and_kv, 0)]
    for i in range(n_pairs):
        if i + 1 < n_pairs:
            stages.append(functools.partial(decay_and_kv, i + 1))
        stages.append(functools.partial(state_and_out, i))
    stages.append(save_state)
    return stages


def _mix_stages(x_ref, h_ref, wgt_ref, bg_ref, wa_ref, wg_ref, wm_ref, oa_ref, og_ref, mg_ref,
                o_ref):
    pieces = [slice(c * MIX_CHUNK, (c + 1) * MIX_CHUNK) for c in range(D_MODEL // MIX_CHUNK)]
    live = {}

    def gates(cs):
        h = h_ref[...]
        for branch in range(2):
            bs = slice(branch * D_MODEL + cs.start, branch * D_MODEL + cs.stop)
            live[branch] = jax.nn.sigmoid(_dot_nt(h, wgt_ref[bs, :]) + bg_ref[:, bs])

    def merge(cs):
        ya = _dot(oa_ref[...], wa_ref[:, cs])
        yg = _dot(og_ref[...], wg_ref[:, cs])
        mg_ref[:, cs] = (live.pop(0) * ya + live.pop(1) * yg).astype(BF16)

    def project(cs):
        o_ref[:, cs] = x_ref[:, cs] + _dot(mg_ref[...], wm_ref[:, cs])

    stages = []
    for cs in pieces:
        stages += [functools.partial(gates, cs), functools.partial(merge, cs)]
    return stages + [functools.partial(project, cs) for cs in pieces]


def _interleave(major, minor):
    out, done = [], 0
    for n, stage in enumerate(major):
        out.append(stage)
        want = (n + 1) * len(minor) // len(major)
        out.extend(minor[done:want])
        done = want
    return out


def _run_lookahead(i, n_tiles, lead, lag, merge, init=None):
    @pl.when(i == 0)
    def _():
        if init is not None:
            init()
        for stage in merge(lead(), []):
            stage()

    @pl.when(jnp.logical_and(i > 0, i < n_tiles))
    def _():
        for stage in merge(lead(), lag()):
            stage()

    @pl.when(i == n_tiles)
    def _():
        for stage in merge([], lag()):
            stage()


def _attn_mix_kernel(tiles_per_seq, n_cast,
                     sink_ref, q_ref, k_ref, kp_ref, vt_ref, vtp_ref,
                     qg_ref, kg_ref, gk_ref, vgt_ref, gg_ref,
                     x_ref, h_ref, wgt_ref, bg_ref, wa_ref, wg_ref, wm_ref, *rest):
    cast_src, (o_ref, *cast_dst) = rest[:n_cast], rest[n_cast:2 * n_cast + 1]
    oa_scr, og_scr, mg_scr, st_ref = rest[2 * n_cast + 1:]
    i = pl.program_id(0)
    n_tiles = pl.num_programs(0) - 1
    seq_start = (i % tiles_per_seq) == 0
    slot_w = i % 2
    slot_r = 1 - slot_w

    def write_oa(rows, cols, val):
        oa_scr[slot_w, rows, cols] = val

    def write_og(rows, cols, val):
        og_scr[slot_w, rows, cols] = val

    def mixers():
        swa = _swa_stages(sink_ref, q_ref, k_ref, kp_ref, vt_ref, vtp_ref, seq_start, write_oa)
        gla = _gla_stages(qg_ref, kg_ref, gk_ref, vgt_ref, gg_ref, st_ref, seq_start, write_og)
        n = len(swa)
        order = [swa[0][0]]
        for p in range(n):
            if p + 1 < n:
                order.append(swa[p + 1][0])
            order.append(swa[p][1])
            if p > 0:
                order.append(swa[p - 1][2])
        order.append(swa[n - 1][2])
        return _interleave(order, gla) + [convert_weights]

    def projections():
        return _mix_stages(x_ref, h_ref, wgt_ref, bg_ref, wa_ref, wg_ref, wm_ref,
                           oa_scr.at[slot_r], og_scr.at[slot_r], mg_scr, o_ref)

    def merge(lead, lag):
        return _interleave(lead, lag) if lead else lag

    def convert_weights():
        for src, dst in zip(cast_src, cast_dst):
            dst[...] = src[...].astype(BF16)

    def init():
        st_ref[...] = jnp.zeros_like(st_ref)

    _run_lookahead(i, n_tiles, mixers, projections, merge, init)


def _attn_mix(sinks, qa, ka, vat, qg, kg, gk, vgt, gg, x2d, h, wgt, bg, wa, wg, wm, cast_w, seq):
    t = x2d.shape[0]
    tm = TM_ATTN
    n_tiles = t // tm
    blocks = tm // SWA_BLOCK
    kw = ka.shape[1]
    cur = lambda i: jnp.minimum(i, n_tiles - 1)
    prv = lambda i: jnp.maximum(i - 1, 0)
    prev_block = lambda i: jnp.maximum(cur(i) * blocks - 1, 0)
    tok = lambda n: pl.BlockSpec((tm, n), lambda i: (cur(i), 0))
    col = lambda n: pl.BlockSpec((n, tm), lambda i: (0, cur(i)))
    old = lambda n: pl.BlockSpec((tm, n), lambda i: (prv(i), 0))

    def slab(w):
        n_slabs = max(n for n in range(1, n_tiles + 1) if w.shape[0] % (16 * n) == 0)
        return pl.BlockSpec((w.shape[0] // n_slabs, w.shape[1]),
                            lambda i: (jnp.minimum(i, n_slabs - 1), 0))

    cast_specs = [slab(w) for w in cast_w]
    return pl.pallas_call(
        functools.partial(_attn_mix_kernel, seq // tm, len(cast_w)),
        grid=(n_tiles + 1,),
        in_specs=[
            pl.BlockSpec(memory_space=pltpu.SMEM),
            tok(qa.shape[1]), tok(kw),
            pl.BlockSpec((SWA_BLOCK, kw), lambda i: (prev_block(i), 0)),
            col(kw),
            pl.BlockSpec((kw, SWA_BLOCK), lambda i: (0, prev_block(i))),
            tok(GLA_KEY_DIM), tok(GLA_KEY_DIM), tok(GLA_KEY_DIM), col(GLA_VAL_DIM),
            tok(GLA_VAL_DIM),
            old(D_MODEL), old(D_MODEL), _const_spec(wgt.shape), _const_spec(bg.shape),
            _const_spec(wa.shape), _const_spec(wg.shape), _const_spec(wm.shape),
        ] + cast_specs,
        out_specs=[old(D_MODEL)] + cast_specs,
        out_shape=[jax.ShapeDtypeStruct((t, D_MODEL), F32)]
        + [jax.ShapeDtypeStruct(w.shape, BF16) for w in cast_w],
        scratch_shapes=[
            pltpu.VMEM((2, tm, SWA_HEADS * SWA_HEAD_DIM), BF16),
            pltpu.VMEM((2, tm, GLA_VAL_DIM), BF16),
            pltpu.VMEM((tm, D_MODEL), BF16),
            pltpu.VMEM((GLA_HEADS, GLA_DV, GLA_DK), F32),
        ],
        compiler_params=_params(("arbitrary",)),
        name="attn_mix",
    )(sinks, qa, ka, ka, vat, vat, qg, kg, gk, vgt, gg, x2d, h, wgt, bg, wa, wg, wm, *cast_w)


def _mem_kv_kernel(m_ref, nw_ref, w_ref, k_ref, v_ref):
    mn = _rms(m_ref[0], nw_ref[...]).astype(BF16)
    kv = _dot(mn, w_ref[...])
    k = kv[:, :MEM_WIDTH].astype(BF16)
    v = kv[:, MEM_WIDTH:].astype(BF16)
    head = lax.broadcasted_iota(jnp.int32, (N_MEM, MEM_WIDTH), 1) // MEM_HEAD_DIM
    zero = jnp.zeros_like(k)
    for h in range(MEM_HEADS):
        k_ref[0, h] = jnp.where(head == h, k, zero)
        v_ref[0, h] = jnp.where(head == h, v, zero)


def _mem_kv(mem, nw, w):
    batch = mem.shape[0]
    out = jax.ShapeDtypeStruct((batch, MEM_HEADS, N_MEM, MEM_WIDTH), BF16)
    spec = pl.BlockSpec((1, MEM_HEADS, N_MEM, MEM_WIDTH), lambda b: (b, 0, 0, 0))
    return pl.pallas_call(
        _mem_kv_kernel,
        grid=(batch,),
        in_specs=[pl.BlockSpec((1, N_MEM, D_MODEL), lambda b: (b, 0, 0)),
                  _const_spec(nw.shape), _const_spec(w.shape)],
        out_specs=(spec, spec),
        out_shape=(out, out),
        compiler_params=_params(("arbitrary",)),
        name="mem_kv",
    )(mem, nw, w)


def _xattn_stages(x_ref, k_ref, v_ref, nq_ref, wq_ref, wo_ref, nf_ref, x2_ref, hf_ref):
    live = {}

    def queries():
        hq = _rms(x_ref[...], nq_ref[...]).astype(BF16)
        live["q"] = (_dot(hq, wq_ref[...]) * (MEM_HEAD_DIM ** -0.5)).astype(BF16)
        live["o"] = jnp.zeros((x_ref.shape[0], MEM_WIDTH), F32)

    def head(h):
        s = _dot_nt(live["q"], k_ref[0, h])
        e = jnp.exp(s - jnp.max(s, axis=-1, keepdims=True))
        inv = 1.0 / jnp.sum(e, axis=-1, keepdims=True)
        live["o"] = live["o"] + _dot(e.astype(BF16), v_ref[0, h]) * inv

    def project():
        x2 = x_ref[...] + _dot(live.pop("o").astype(BF16), wo_ref[...])
        x2_ref[...] = x2
        hf_ref[...] = _rms(x2, nf_ref[...]).astype(BF16)

    return [queries] + [functools.partial(head, h) for h in range(MEM_HEADS)] + [project]


def _ffn_stages(x2_ref, hf_ref, wg_ref, wu_ref, wd_ref, nfin_ref, act_ref, o_ref):
    live = {"ss": jnp.zeros((o_ref.shape[0], 1), F32)}

    def hidden(cs):
        hf = hf_ref[...]
        g = _dot(hf, wg_ref[:, cs])
        u = _dot(hf, wu_ref[:, cs])
        act_ref[:, cs] = (g * jax.nn.sigmoid(g) * u).astype(BF16)

    def down(cs):
        y = x2_ref[:, cs] + _dot(act_ref[...], wd_ref[:, cs])
        live["ss"] = live["ss"] + jnp.sum(y * y, axis=-1, keepdims=True)
        o_ref[:, cs] = y

    def final_norm():
        inv = lax.rsqrt(live["ss"] * (1.0 / D_MODEL) + EPS)
        o_ref[...] = o_ref[...] * inv * nfin_ref[...]

    hid = [slice(c * FF_CHUNK, (c + 1) * FF_CHUNK) for c in range(D_FF // FF_CHUNK)]
    out = [slice(c * MIX_CHUNK, (c + 1) * MIX_CHUNK) for c in range(D_MODEL // MIX_CHUNK)]
    return ([functools.partial(hidden, cs) for cs in hid]
            + [functools.partial(down, cs) for cs in out] + [final_norm])


def _tail_kernel(x_ref, k_ref, v_ref, nq_ref, wq_ref, wo_ref, nf_ref, wg_ref, wu_ref, wd_ref,
                 nfin_ref, o_ref, x2_scr, hf_scr, act_ref):
    i = pl.program_id(0)
    n_tiles = pl.num_programs(0) - 1
    slot_w = i % 2
    slot_r = 1 - slot_w

    def xattn():
        return _xattn_stages(x_ref, k_ref, v_ref, nq_ref, wq_ref, wo_ref, nf_ref,
                             x2_scr.at[slot_w], hf_scr.at[slot_w])

    def ffn():
        return _ffn_stages(x2_scr.at[slot_r], hf_scr.at[slot_r], wg_ref, wu_ref, wd_ref,
                           nfin_ref, act_ref, o_ref)

    def merge(lead, lag):
        if not lag:
            return lead
        n_hidden = D_FF // FF_CHUNK
        n_first = n_hidden - len(lead)
        return lag[:n_first] + _interleave(lag[n_first:n_hidden], lead) + lag[n_hidden:]

    _run_lookahead(i, n_tiles, xattn, ffn, merge)


def _tail(x2d, kext, vext, nq, wq, wo, nf, wg, wu, wd, nfin, seq):
    tm = TM_TAIL
    t = x2d.shape[0]
    n_tiles = t // tm
    tiles_per_seq = seq // tm
    cur = lambda i: jnp.minimum(i, n_tiles - 1)
    prv = lambda i: jnp.maximum(i - 1, 0)
    mem_spec = pl.BlockSpec((1, MEM_HEADS, N_MEM, MEM_WIDTH),
                            lambda i: (cur(i) // tiles_per_seq, 0, 0, 0))
    consts = (nq, wq, wo, nf, wg, wu, wd, nfin)
    return pl.pallas_call(
        _tail_kernel,
        grid=(n_tiles + 1,),
        in_specs=[pl.BlockSpec((tm, D_MODEL), lambda i: (cur(i), 0)), mem_spec, mem_spec]
        + [_const_spec(c.shape) for c in consts],
        out_specs=pl.BlockSpec((tm, D_MODEL), lambda i: (prv(i), 0)),
        out_shape=jax.ShapeDtypeStruct((t, D_MODEL), F32),
        scratch_shapes=[pltpu.VMEM((2, tm, D_MODEL), F32), pltpu.VMEM((2, tm, D_MODEL), BF16),
                        pltpu.VMEM((tm, D_FF), BF16)],
        compiler_params=_params(("arbitrary",)),
        name="tail",
    )(x2d, kext, vext, *consts)


def _layer(x2d, mem, batch, seq, norm_mix_w, w_in, b_gate, attn_sinks, gla_gate_w2, gla_gate_b,
           gla_norm_w, w_attn_o, w_gla_o, w_mix_o, norm_mem_q_w, norm_mem_kv_w, w_mem_q,
           w_mem_kv, w_mem_o, norm_ffn_w, w_ffn_gate, w_ffn_up, w_ffn_down, out_norm_w):
    w2_p = jnp.pad(gla_gate_w2, ((0, LANES - GLA_GATE_RANK), (0, 0)))
    bf = lambda a: a.astype(BF16)
    r2 = lambda a: a.reshape(1, -1)

    o_alr, o_gate = IN_OFFSETS[7], IN_OFFSETS[8]
    wt = w_in.T
    h, qa, ka, vat, qg, kg, vgt, gg, gk = _in_proj(
        x2d, r2(norm_mix_w), bf(wt[:o_alr + LANES]), bf(w2_p), r2(gla_gate_b),
        r2(jnp.tile(gla_norm_w, GLA_HEADS)))
    group = SWA_HEADS // SWA_KV_HEADS
    wa = w_attn_o.reshape(SWA_KV_HEADS, group, SWA_HEAD_DIM, D_MODEL)
    wa = wa.transpose(1, 0, 2, 3).reshape(SWA_HEADS * SWA_HEAD_DIM, D_MODEL)
    x1, wg16, wu16, wd16, wq16, wo16, wkv16 = _attn_mix(
        attn_sinks, qa, ka, vat, qg, kg, gk, vgt, gg, x2d, h, bf(wt[o_gate:]), r2(b_gate),
        bf(wa), bf(w_gla_o), bf(w_mix_o),
        (w_ffn_gate, w_ffn_up, w_ffn_down, w_mem_q, w_mem_o, w_mem_kv), seq)
    kext, vext = _mem_kv(mem, r2(norm_mem_kv_w), wkv16)
    return _tail(x1, kext, vext, r2(norm_mem_q_w), wq16, wo16, r2(norm_ffn_w),
                 wg16, wu16, wd16, r2(out_norm_w), seq)


def kernel(x, mem, norm_mix_w, w_in, b_gate, attn_sinks, gla_gate_w2, gla_gate_b, gla_norm_w,
           w_attn_o, w_gla_o, w_mix_o, norm_mem_q_w, norm_mem_kv_w, w_mem_q, w_mem_kv, w_mem_o,
           norm_ffn_w, w_ffn_gate, w_ffn_up, w_ffn_down, norm_final_w):
    batch, seq, d = x.shape
    depth = w_in.shape[0]
    assert depth == 1 and d == D_MODEL
    assert seq % TM_PROJ == 0 and seq % TM_ATTN == 0 and seq % TM_TAIL == 0
    out = _layer(x.reshape(batch * seq, d), mem, batch, seq, norm_mix_w[0], w_in[0], b_gate[0],
                 attn_sinks[0], gla_gate_w2[0], gla_gate_b[0], gla_norm_w[0], w_attn_o[0],
                 w_gla_o[0], w_mix_o[0], norm_mem_q_w[0], norm_mem_kv_w[0], w_mem_q[0],
                 w_mem_kv[0], w_mem_o[0], norm_ffn_w[0], w_ffn_gate[0], w_ffn_up[0],
                 w_ffn_down[0], norm_final_w)
    return out.reshape(batch, seq, d)
```

```python
import functools

import jax
import jax.numpy as jnp
from jax import lax
from jax.experimental import pallas as pl
from jax.experimental.pallas import tpu as pltpu

D_MODEL = 1024
CHUNK = 64
N_MEM = 256
EPS = 1e-6

SWA_HEADS = 16
SWA_KV_HEADS = 2
SWA_HEAD_DIM = 64
SWA_BLOCK = 128

GLA_HEADS = 4
GLA_KEY_DIM = D_MODEL // 2
GLA_VAL_DIM = D_MODEL
GLA_DK = GLA_KEY_DIM // GLA_HEADS
GLA_DV = GLA_VAL_DIM // GLA_HEADS
GLA_GATE_RANK = 16
GLA_GATE_NORM = 16.0

MEM_HEADS = 4
MEM_HEAD_DIM = 64
MEM_WIDTH = MEM_HEADS * MEM_HEAD_DIM

D_FF = -(-(8 * D_MODEL) // (3 * 256)) * 256

IN_SIZES = (SWA_HEADS * SWA_HEAD_DIM, SWA_KV_HEADS * SWA_HEAD_DIM, SWA_KV_HEADS * SWA_HEAD_DIM,
            GLA_KEY_DIM, GLA_KEY_DIM, GLA_VAL_DIM, GLA_VAL_DIM, GLA_GATE_RANK, 2 * D_MODEL)
IN_OFFSETS = tuple(sum(IN_SIZES[:i]) for i in range(len(IN_SIZES) + 1))

LANES = 128
VMEM_LIMIT = 56 * 1024 * 1024

LOG2E = 1.4426950408889634

TM_PROJ = 1024
TM_ATTN = 512
TM_TAIL = 512
MIX_CHUNK = 256
FF_CHUNK = 256

BF16 = jnp.bfloat16
F32 = jnp.float32


def _rms(x, w):
    return x * lax.rsqrt(jnp.mean(x * x, axis=-1, keepdims=True) + EPS) * w


def _dot(a, b):
    return jnp.dot(a, b, preferred_element_type=F32)


def _dot_nt(a, b):
    return lax.dot_general(a, b, (((1,), (1,)), ((), ())), preferred_element_type=F32)


def _const_spec(shape):
    zeros = (0,) * len(shape)
    return pl.BlockSpec(shape, lambda *_: zeros, pipeline_mode=pl.Buffered(1))


def _params(semantics):
    return pltpu.CompilerParams(dimension_semantics=semantics, vmem_limit_bytes=VMEM_LIMIT)


def _in_proj_kernel(x_ref, nw_ref, wt_ref, w2_ref, gb_ref, bg_ref, gn_ref,
                    qa_ref, ka_ref, vat_ref, qg_ref, kg_ref, vgt_ref, gg_ref, gk_ref, gate_ref):
    o_qa, o_ka, o_va, o_qg, o_kg, o_vg, o_gg, o_alr, o_gate, o_end = IN_OFFSETS
    hd = SWA_HEAD_DIM
    h = _rms(x_ref[...], nw_ref[...]).astype(BF16)
    half = (o_end - o_gate) // 2

    def branch_gates(lo):
        pre = _dot_nt(h, wt_ref[o_gate + lo:o_gate + lo + half, :]) + bg_ref[:, lo:lo + half]
        gate_ref[:, lo:lo + half] = jax.nn.sigmoid(pre).astype(BF16)

    branch_gates(0)
    group = SWA_HEADS // SWA_KV_HEADS
    wq = jnp.concatenate(
        [wt_ref[o_qa + (p + j * group) * hd:o_qa + (p + j * group + 1) * hd, :]
         for p in range(group) for j in range(SWA_KV_HEADS)], axis=0)
    qa_ref[...] = (_dot_nt(h, wq) * (hd ** -0.5 * LOG2E)).astype(BF16)
    branch_gates(half)
    vgt_ref[...] = _dot_nt(wt_ref[o_vg:o_gg, :], h).astype(BF16)
    g = _dot_nt(h, wt_ref[o_gg:o_alr, :])
    gg_ref[...] = (g * jax.nn.sigmoid(g) * gn_ref[...]).astype(BF16)
    kw = SWA_KV_HEADS * hd
    k_alr = _dot_nt(h, jnp.concatenate([wt_ref[o_ka:o_va, :], wt_ref[o_alr:o_alr + LANES, :]],
                                       axis=0))
    ka_ref[...] = k_alr[:, :kw].astype(BF16)
    qg_ref[...] = _dot_nt(h, wt_ref[o_qg:o_kg, :]).astype(BF16)
    z = _dot(k_alr[:, kw:].astype(BF16), w2_ref[...]) + gb_ref[...]
    log_sig = jnp.minimum(z, 0.0) - jnp.log(1.0 + jnp.exp(-jnp.abs(z)))
    gk_ref[...] = (log_sig * (1.0 / GLA_GATE_NORM)).astype(BF16)
    kg_ref[...] = _dot_nt(h, wt_ref[o_kg:o_vg, :]).astype(BF16)
    vat_ref[...] = _dot_nt(wt_ref[o_va:o_qg, :], h).astype(BF16)


def _in_proj(x2d, nw, wt, w2, gb, bg, gn):
    t = x2d.shape[0]
    tm = TM_PROJ
    row = lambda n: pl.BlockSpec((tm, n), lambda i: (i, 0))
    col = lambda n: pl.BlockSpec((n, tm), lambda i: (0, i))
    consts = (nw, wt, w2, gb, bg, gn)
    kw = SWA_KV_HEADS * SWA_HEAD_DIM
    tok = lambda n: jax.ShapeDtypeStruct((t, n), BF16)
    out_shape = (
        tok(SWA_HEADS * SWA_HEAD_DIM), tok(kw), jax.ShapeDtypeStruct((kw, t), BF16),
        tok(GLA_KEY_DIM), tok(GLA_KEY_DIM), jax.ShapeDtypeStruct((GLA_VAL_DIM, t), BF16),
        tok(GLA_VAL_DIM), tok(GLA_KEY_DIM), tok(2 * D_MODEL),
    )
    out_specs = (
        row(SWA_HEADS * SWA_HEAD_DIM), row(kw), col(kw),
        row(GLA_KEY_DIM), row(GLA_KEY_DIM), col(GLA_VAL_DIM),
        row(GLA_VAL_DIM), row(GLA_KEY_DIM), row(2 * D_MODEL),
    )
    return pl.pallas_call(
        _in_proj_kernel,
        grid=(t // tm,),
        in_specs=[row(D_MODEL)] + [_const_spec(c.shape) for c in consts],
        out_specs=out_specs,
        out_shape=out_shape,
        compiler_params=_params(("arbitrary",)),
        name="in_proj",
    )(x2d, *consts)


def _swa_stages(sink_ref, q_ref, k_ref, kp_ref, vt_ref, vtp_ref, seq_start, write):
    blk = SWA_BLOCK
    win = blk + CHUNK
    low = lax.broadcasted_iota(jnp.int32, (CHUNK, LANES), 1) < SWA_HEAD_DIM
    low_row = lax.broadcasted_iota(jnp.int32, (1, LANES), 1) < SWA_HEAD_DIM
    prev_bias = jnp.where(seq_start, -1e30, 0.0).astype(F32)
    k_all = jnp.concatenate([kp_ref[...], k_ref[...]], axis=0)
    vt_all = jnp.concatenate([vtp_ref[...], vt_ref[...]], axis=1)
    zeros_chunk = jnp.zeros((CHUNK, LANES), BF16)
    n_pairs = SWA_HEADS // SWA_KV_HEADS
    n_blocks = q_ref.shape[0] // blk
    n_ch = blk // CHUNK
    live = {}

    def scores(p):
        res = []
        for u in range(n_blocks):
            qsel = []
            for qh in range(n_ch):
                r0 = u * blk + qh * CHUNK
                qp = q_ref[r0:r0 + CHUNK, p * LANES:(p + 1) * LANES]
                zero = jnp.zeros_like(qp)
                qsel += [jnp.where(low, qp, zero), jnp.where(low, zero, qp)]
            res.append(_dot_nt(k_all[u * blk:(u + 2) * blk], jnp.concatenate(qsel, axis=0)))
        live["s", p] = res

    def softmax_values(p):
        sink_row = jnp.where(low_row, sink_ref[p], sink_ref[p + n_pairs]) * LOG2E
        res = []
        for u, s_blk in enumerate(live.pop(("s", p))):
            probs, inv = [], []
            for qh in range(n_ch):
                st = s_blk[qh * CHUNK:qh * CHUNK + win, qh * LANES:(qh + 1) * LANES]
                if u == 0:
                    n_prev = blk - qh * CHUNK
                    st = jnp.concatenate([st[:n_prev] + prev_bias, st[n_prev:]], axis=0)
                m = jnp.maximum(jnp.max(st, axis=0, keepdims=True), sink_row)
                e = jnp.exp2(st - m)
                inv.append(1.0 / (jnp.sum(e, axis=0, keepdims=True) + jnp.exp2(sink_row - m)))
                probs.append(jnp.concatenate([zeros_chunk] * qh + [e.astype(BF16)]
                                             + [zeros_chunk] * (n_ch - 1 - qh), axis=0))
            ot = _dot(vt_all[:, u * blk:(u + 2) * blk], jnp.concatenate(probs, axis=1))
            res.append([ot[:, qh * LANES:(qh + 1) * LANES] * inv[qh] for qh in range(n_ch)])
        live["o", p] = res

    def store(p):
        for u, ots in enumerate(live.pop(("o", p))):
            for qh, ot in enumerate(ots):
                r0 = u * blk + qh * CHUNK
                o2 = ot.T
                write(slice(r0, r0 + CHUNK), slice(p * LANES, (p + 1) * LANES),
                      jnp.where(low, o2[:CHUNK], o2[CHUNK:]).astype(BF16))

    return [(functools.partial(scores, p), functools.partial(softmax_values, p),
             functools.partial(store, p)) for p in range(n_pairs)]


def _gla_stages(q_ref, k_ref, gk_ref, vt_ref, g_ref, st_ref, seq_start, write):
    pair = 2 * CHUNK
    heads = range(GLA_HEADS)
    ks = [slice(h * GLA_DK, (h + 1) * GLA_DK) for h in heads]
    vs = [slice(h * GLA_DV, (h + 1) * GLA_DV) for h in heads]
    r = lax.broadcasted_iota(jnp.int32, (pair, pair), 0)
    c = lax.broadcasted_iota(jnp.int32, (pair, pair), 1)
    tri = ((c <= r) & ((r // CHUNK) == (c // CHUNK))).astype(BF16)
    first = lax.broadcasted_iota(jnp.int32, (pair, GLA_DK), 0) < CHUNK
    eps_scaled = EPS * GLA_DK
    n_pairs = q_ref.shape[0] // pair
    sts = [jnp.where(seq_start, jnp.zeros((GLA_DV, GLA_DK), F32), st_ref[h]) for h in heads]
    live = {}

    def decay_and_kv(i):
        sl = slice(i * pair, (i + 1) * pair)
        b_all = _dot(tri, gk_ref[sl, :])
        res = []
        for h in heads:
            b = b_all[:, ks[h]]
            b_end0 = b[CHUNK - 1:CHUNK, :]
            b_end1 = b[pair - 1:pair, :]
            k_dec = (k_ref[sl, ks[h]].astype(F32)
                     * jnp.exp(jnp.where(first, b_end0, b_end1) - b)).astype(BF16)
            zero = jnp.zeros_like(k_dec)
            kd = jnp.concatenate([jnp.where(first, k_dec, zero), jnp.where(first, zero, k_dec)],
                                 axis=1)
            kv = _dot(vt_ref[vs[h], sl], kd)
            res.append((jnp.exp(b_end0), jnp.exp(b_end1), kv))
        live[i] = res

    def state_and_out(i):
        sl = slice(i * pair, (i + 1) * pair)
        dk = live.pop(i)
        outs = []
        for h in heads:
            a0, a1, kv = dk[h]
            st0 = sts[h] * a0 + kv[:, :GLA_DK]
            st1 = st0 * a1 + kv[:, GLA_DK:]
            sts[h] = st1
            q = q_ref[sl, ks[h]]
            outs.append(jnp.concatenate([_dot_nt(q[:CHUNK], st0.astype(BF16)),
                                         _dot_nt(q[CHUNK:], st1.astype(BF16))], axis=0))
        for h in heads:
            o = outs[h]
            inv = lax.rsqrt(jnp.mean(o * o, axis=-1, keepdims=True) + eps_scaled)
            write(sl, vs[h], (o * inv).astype(BF16) * g_ref[sl, vs[h]])

    def save_state():
        for h in heads:
            st_ref[h] = sts[h]

    stages = [functools.partial(decay_and_kv, 0)]
    for i in range(n_pairs):
        if i + 1 < n_pairs:
            stages.append(functools.partial(decay_and_kv, i + 1))
        stages.append(functools.partial(state_and_out, i))
    stages.append(save_state)
    return stages


def _mix_stages(x_ref, gate_ref, wa_ref, wg_ref, wm_ref, oa_ref, og_ref, mg_ref, o_ref):
    pieces = [slice(c * MIX_CHUNK, (c + 1) * MIX_CHUNK) for c in range(D_MODEL // MIX_CHUNK)]

    def merge(cs):
        ya = _dot(oa_ref[...], wa_ref[:, cs])
        yg = _dot(og_ref[...], wg_ref[:, cs])
        cs_b = slice(D_MODEL + cs.start, D_MODEL + cs.stop)
        mg_ref[:, cs] = (gate_ref[:, cs].astype(F32) * ya
                         + gate_ref[:, cs_b].astype(F32) * yg).astype(BF16)

    def project(cs):
        o_ref[:, cs] = x_ref[:, cs] + _dot(mg_ref[...], wm_ref[:, cs])

    return ([functools.partial(merge, cs) for cs in pieces]
            + [functools.partial(project, cs) for cs in pieces])


def _interleave(major, minor):
    out, done = [], 0
    for n, stage in enumerate(major):
        out.append(stage)
        want = (n + 1) * len(minor) // len(major)
        out.extend(minor[done:want])
        done = want
    return out


def _run_lookahead(i, n_tiles, lead, lag, merge, init=None):
    @pl.when(i == 0)
    def _():
        if init is not None:
            init()
        for stage in merge(lead(), []):
            stage()

    @pl.when(jnp.logical_and(i > 0, i < n_tiles))
    def _():
        for stage in merge(lead(), lag()):
            stage()

    @pl.when(i == n_tiles)
    def _():
        for stage in merge([], lag()):
            stage()


def _attn_mix_kernel(tiles_per_seq, n_cast,
                     sink_ref, q_ref, k_ref, kp_ref, vt_ref, vtp_ref,
                     qg_ref, kg_ref, gk_ref, vgt_ref, gg_ref,
                     x_ref, gate_ref, wa_ref, wg_ref, wm_ref, *rest):
    cast_src, (o_ref, *cast_dst) = rest[:n_cast], rest[n_cast:2 * n_cast + 1]
    oa_scr, og_scr, mg_scr, st_ref = rest[2 * n_cast + 1:]
    i = pl.program_id(0)
    n_tiles = pl.num_programs(0) - 1
    seq_start = (i % tiles_per_seq) == 0
    slot_w = i % 2
    slot_r = 1 - slot_w

    def write_oa(rows, cols, val):
        oa_scr[slot_w, rows, cols] = val

    def write_og(rows, cols, val):
        og_scr[slot_w, rows, cols] = val

    def mixers():
        swa = _swa_stages(sink_ref, q_ref, k_ref, kp_ref, vt_ref, vtp_ref, seq_start, write_oa)
        gla = _gla_stages(qg_ref, kg_ref, gk_ref, vgt_ref, gg_ref, st_ref, seq_start, write_og)
        n = len(swa)
        order = [swa[0][0]]
        for p in range(n):
            if p + 1 < n:
                order.append(swa[p + 1][0])
            order.append(swa[p][1])
            if p > 0:
                order.append(swa[p - 1][2])
        order.append(swa[n - 1][2])
        return _interleave(order, gla) + [convert_weights]

    def projections():
        return _mix_stages(x_ref, gate_ref, wa_ref, wg_ref, wm_ref,
                           oa_scr.at[slot_r], og_scr.at[slot_r], mg_scr, o_ref)

    def merge(lead, lag):
        return _interleave(lead, lag) if lead else lag

    def convert_weights():
        for src, dst in zip(cast_src, cast_dst):
            dst[...] = src[...].astype(BF16)

    def init():
        st_ref[...] = jnp.zeros_like(st_ref)

    _run_lookahead(i, n_tiles, mixers, projections, merge, init)


def _attn_mix(sinks, qa, ka, vat, qg, kg, gk, vgt, gg, x2d, gates, wa, wg, wm, cast_w, seq):
    t = x2d.shape[0]
    tm = TM_ATTN
    n_tiles = t // tm
    blocks = tm // SWA_BLOCK
    kw = ka.shape[1]
    cur = lambda i: jnp.minimum(i, n_tiles - 1)
    prv = lambda i: jnp.maximum(i - 1, 0)
    prev_block = lambda i: jnp.maximum(cur(i) * blocks - 1, 0)
    tok = lambda n: pl.BlockSpec((tm, n), lambda i: (cur(i), 0))
    col = lambda n: pl.BlockSpec((n, tm), lambda i: (0, cur(i)))
    old = lambda n: pl.BlockSpec((tm, n), lambda i: (prv(i), 0))

    def slab(w):
        n_slabs = max(n for n in range(1, n_tiles + 1) if w.shape[0] % (16 * n) == 0)
        return pl.BlockSpec((w.shape[0] // n_slabs, w.shape[1]),
                            lambda i: (jnp.minimum(i, n_slabs - 1), 0))

    cast_specs = [slab(w) for w in cast_w]
    return pl.pallas_call(
        functools.partial(_attn_mix_kernel, seq // tm, len(cast_w)),
        grid=(n_tiles + 1,),
        in_specs=[
            pl.BlockSpec(memory_space=pltpu.SMEM),
            tok(qa.shape[1]), tok(kw),
            pl.BlockSpec((SWA_BLOCK, kw), lambda i: (prev_block(i), 0)),
            col(kw),
            pl.BlockSpec((kw, SWA_BLOCK), lambda i: (0, prev_block(i))),
            tok(GLA_KEY_DIM), tok(GLA_KEY_DIM), tok(GLA_KEY_DIM), col(GLA_VAL_DIM),
            tok(GLA_VAL_DIM),
            old(D_MODEL), old(2 * D_MODEL),
            _const_spec(wa.shape), _const_spec(wg.shape), _const_spec(wm.shape),
        ] + cast_specs,
        out_specs=[old(D_MODEL)] + cast_specs,
        out_shape=[jax.ShapeDtypeStruct((t, D_MODEL), F32)]
        + [jax.ShapeDtypeStruct(w.shape, BF16) for w in cast_w],
        scratch_shapes=[
            pltpu.VMEM((2, tm, SWA_HEADS * SWA_HEAD_DIM), BF16),
            pltpu.VMEM((2, tm, GLA_VAL_DIM), BF16),
            pltpu.VMEM((tm, D_MODEL), BF16),
            pltpu.VMEM((GLA_HEADS, GLA_DV, GLA_DK), F32),
        ],
        compiler_params=_params(("arbitrary",)),
        name="attn_mix",
    )(sinks, qa, ka, ka, vat, vat, qg, kg, gk, vgt, gg, x2d, gates, wa, wg, wm, *cast_w)


def _mem_kv_kernel(m_ref, nw_ref, w_ref, k_ref, v_ref):
    mn = _rms(m_ref[0], nw_ref[...]).astype(BF16)
    kv = _dot(mn, w_ref[...])
    k = kv[:, :MEM_WIDTH].astype(BF16)
    v = kv[:, MEM_WIDTH:].astype(BF16)
    head = lax.broadcasted_iota(jnp.int32, (N_MEM, MEM_WIDTH), 1) // MEM_HEAD_DIM
    zero = jnp.zeros_like(k)
    for h in range(MEM_HEADS):
        k_ref[0, h] = jnp.where(head == h, k, zero)
        v_ref[0, h] = jnp.where(head == h, v, zero)


def _mem_kv(mem, nw, w):
    batch = mem.shape[0]
    out = jax.ShapeDtypeStruct((batch, MEM_HEADS, N_MEM, MEM_WIDTH), BF16)
    spec = pl.BlockSpec((1, MEM_HEADS, N_MEM, MEM_WIDTH), lambda b: (b, 0, 0, 0))
    return pl.pallas_call(
        _mem_kv_kernel,
        grid=(batch,),
        in_specs=[pl.BlockSpec((1, N_MEM, D_MODEL), lambda b: (b, 0, 0)),
                  _const_spec(nw.shape), _const_spec(w.shape)],
        out_specs=(spec, spec),
        out_shape=(out, out),
        compiler_params=_params(("arbitrary",)),
        name="mem_kv",
    )(mem, nw, w)


def _xattn_stages(x_ref, k_ref, v_ref, nq_ref, wq_ref, wo_ref, nf_ref, x2_ref, hf_ref):
    live = {}

    def queries():
        hq = _rms(x_ref[...], nq_ref[...]).astype(BF16)
        live["q"] = (_dot(hq, wq_ref[...]) * (MEM_HEAD_DIM ** -0.5)).astype(BF16)
        live["o"] = jnp.zeros((x_ref.shape[0], MEM_WIDTH), F32)

    def head(h):
        s = _dot_nt(live["q"], k_ref[0, h])
        e = jnp.exp(s - jnp.max(s, axis=-1, keepdims=True))
        inv = 1.0 / jnp.sum(e, axis=-1, keepdims=True)
        live["o"] = live["o"] + _dot(e.astype(BF16), v_ref[0, h]) * inv

    def project():
        x2 = x_ref[...] + _dot(live.pop("o").astype(BF16), wo_ref[...])
        x2_ref[...] = x2
        hf_ref[...] = _rms(x2, nf_ref[...]).astype(BF16)

    return [queries] + [functools.partial(head, h) for h in range(MEM_HEADS)] + [project]


def _ffn_stages(x2_ref, hf_ref, wg_ref, wu_ref, wd_ref, nfin_ref, act_ref, o_ref):
    live = {"ss": jnp.zeros((o_ref.shape[0], 1), F32)}

    def hidden(cs):
        hf = hf_ref[...]
        g = _dot(hf, wg_ref[:, cs])
        u = _dot(hf, wu_ref[:, cs])
        act_ref[:, cs] = (g * jax.nn.sigmoid(g) * u).astype(BF16)

    def down(cs):
        y = x2_ref[:, cs] + _dot(act_ref[...], wd_ref[:, cs])
        live["ss"] = live["ss"] + jnp.sum(y * y, axis=-1, keepdims=True)
        o_ref[:, cs] = y

    def final_norm():
        inv = lax.rsqrt(live["ss"] * (1.0 / D_MODEL) + EPS)
        o_ref[...] = o_ref[...] * inv * nfin_ref[...]

    hid = [slice(c * FF_CHUNK, (c + 1) * FF_CHUNK) for c in range(D_FF // FF_CHUNK)]
    out = [slice(c * MIX_CHUNK, (c + 1) * MIX_CHUNK) for c in range(D_MODEL // MIX_CHUNK)]
    return ([functools.partial(hidden, cs) for cs in hid]
            + [functools.partial(down, cs) for cs in out] + [final_norm])


def _tail_kernel(x_ref, k_ref, v_ref, nq_ref, wq_ref, wo_ref, nf_ref, wg_ref, wu_ref, wd_ref,
                 nfin_ref, o_ref, x2_scr, hf_scr, act_ref):
    i = pl.program_id(0)
    n_tiles = pl.num_programs(0) - 1
    slot_w = i % 2
    slot_r = 1 - slot_w

    def xattn():
        return _xattn_stages(x_ref, k_ref, v_ref, nq_ref, wq_ref, wo_ref, nf_ref,
                             x2_scr.at[slot_w], hf_scr.at[slot_w])

    def ffn():
        return _ffn_stages(x2_scr.at[slot_r], hf_scr.at[slot_r], wg_ref, wu_ref, wd_ref,
                           nfin_ref, act_ref, o_ref)

    def merge(lead, lag):
        if not lag:
            return lead
        n_hidden = D_FF // FF_CHUNK
        n_first = n_hidden - len(lead)
        return lag[:n_first] + _interleave(lag[n_first:n_hidden], lead) + lag[n_hidden:]

    _run_lookahead(i, n_tiles, xattn, ffn, merge)


def _tail(x2d, kext, vext, nq, wq, wo, nf, wg, wu, wd, nfin, seq):
    tm = TM_TAIL
    t = x2d.shape[0]
    n_tiles = t // tm
    tiles_per_seq = seq // tm
    cur = lambda i: jnp.minimum(i, n_tiles - 1)
    prv = lambda i: jnp.maximum(i - 1, 0)
    mem_spec = pl.BlockSpec((1, MEM_HEADS, N_MEM, MEM_WIDTH),
                            lambda i: (cur(i) // tiles_per_seq, 0, 0, 0))
    consts = (nq, wq, wo, nf, wg, wu, wd, nfin)
    return pl.pallas_call(
        _tail_kernel,
        grid=(n_tiles + 1,),
        in_specs=[pl.BlockSpec((tm, D_MODEL), lambda i: (cur(i), 0)), mem_spec, mem_spec]
        + [_const_spec(c.shape) for c in consts],
        out_specs=pl.BlockSpec((tm, D_MODEL), lambda i: (prv(i), 0)),
        out_shape=jax.ShapeDtypeStruct((t, D_MODEL), F32),
        scratch_shapes=[pltpu.VMEM((2, tm, D_MODEL), F32), pltpu.VMEM((2, tm, D_MODEL), BF16),
                        pltpu.VMEM((tm, D_FF), BF16)],
        compiler_params=_params(("arbitrary",)),
        name="tail",
    )(x2d, kext, vext, *consts)


def _layer(x2d, mem, batch, seq, norm_mix_w, w_in, b_gate, attn_sinks, gla_gate_w2, gla_gate_b,
           gla_norm_w, w_attn_o, w_gla_o, w_mix_o, norm_mem_q_w, norm_mem_kv_w, w_mem_q,
           w_mem_kv, w_mem_o, norm_ffn_w, w_ffn_gate, w_ffn_up, w_ffn_down, out_norm_w):
    w2_p = jnp.pad(gla_gate_w2, ((0, LANES - GLA_GATE_RANK), (0, 0)))
    bf = lambda a: a.astype(BF16)
    r2 = lambda a: a.reshape(1, -1)

    qa, ka, vat, qg, kg, vgt, gg, gk, gates = _in_proj(
        x2d, r2(norm_mix_w), bf(w_in.T), bf(w2_p), r2(gla_gate_b), r2(b_gate),
        r2(jnp.tile(gla_norm_w, GLA_HEADS)))
    group = SWA_HEADS // SWA_KV_HEADS
    wa = w_attn_o.reshape(SWA_KV_HEADS, group, SWA_HEAD_DIM, D_MODEL)
    wa = wa.transpose(1, 0, 2, 3).reshape(SWA_HEADS * SWA_HEAD_DIM, D_MODEL)
    x1, wg16, wu16, wd16, wq16, wo16, wkv16 = _attn_mix(
        attn_sinks, qa, ka, vat, qg, kg, gk, vgt, gg, x2d, gates,
        bf(wa), bf(w_gla_o), bf(w_mix_o),
        (w_ffn_gate, w_ffn_up, w_ffn_down, w_mem_q, w_mem_o, w_mem_kv), seq)
    kext, vext = _mem_kv(mem, r2(norm_mem_kv_w), wkv16)
    return _tail(x1, kext, vext, r2(norm_mem_q_w), wq16, wo16, r2(norm_ffn_w),
                 wg16, wu16, wd16, r2(out_norm_w), seq)


def kernel(x, mem, norm_mix_w, w_in, b_gate, attn_sinks, gla_gate_w2, gla_gate_b, gla_norm_w,
           w_attn_o, w_gla_o, w_mix_o, norm_mem_q_w, norm_mem_kv_w, w_mem_q, w_mem_kv, w_mem_o,
           norm_ffn_w, w_ffn_gate, w_ffn_up, w_ffn_down, norm_final_w):
    batch, seq, d = x.shape
    depth = w_in.shape[0]
    assert depth == 1 and d == D_MODEL
    assert seq % TM_PROJ == 0 and seq % TM_ATTN == 0 and seq % TM_TAIL == 0
    out = _layer(x.reshape(batch * seq, d), mem, batch, seq, norm_mix_w[0], w_in[0], b_gate[0],
                 attn_sinks[0], gla_gate_w2[0], gla_gate_b[0], gla_norm_w[0], w_attn_o[0],
                 w_gla_o[0], w_mix_o[0], norm_mem_q_w[0], norm_mem_kv_w[0], w_mem_q[0],
                 w_mem_kv[0], w_mem_o[0], norm_ffn_w[0], w_ffn_gate[0], w_ffn_up[0],
                 w_ffn_down[0], norm_final_w)
    return out.reshape(batch, seq, d)
```

```python
import functools

import jax
import jax.numpy as jnp
from jax import lax
from jax.experimental import pallas as pl
from jax.experimental.pallas import tpu as pltpu

D_MODEL = 1024
CHUNK = 64
N_MEM = 256
EPS = 1e-6

SWA_HEADS = 16
SWA_KV_HEADS = 2
SWA_HEAD_DIM = 64
SWA_BLOCK = 128

GLA_HEADS = 4
GLA_KEY_DIM = D_MODEL // 2
GLA_VAL_DIM = D_MODEL
GLA_DK = GLA_KEY_DIM // GLA_HEADS
GLA_DV = GLA_VAL_DIM // GLA_HEADS
GLA_GATE_RANK = 16
GLA_GATE_NORM = 16.0

MEM_HEADS = 4
MEM_HEAD_DIM = 64
MEM_WIDTH = MEM_HEADS * MEM_HEAD_DIM

D_FF = -(-(8 * D_MODEL) // (3 * 256)) * 256

IN_SIZES = (SWA_HEADS * SWA_HEAD_DIM, SWA_KV_HEADS * SWA_HEAD_DIM, SWA_KV_HEADS * SWA_HEAD_DIM,
            GLA_KEY_DIM, GLA_KEY_DIM, GLA_VAL_DIM, GLA_VAL_DIM, GLA_GATE_RANK, 2 * D_MODEL)
IN_OFFSETS = tuple(sum(IN_SIZES[:i]) for i in range(len(IN_SIZES) + 1))

LANES = 128
VMEM_LIMIT = 56 * 1024 * 1024

LOG2E = 1.4426950408889634

TM_PROJ = 1024
TM_ATTN = 512
TM_TAIL = 512
MIX_CHUNK = 256
FF_CHUNK = 256

BF16 = jnp.bfloat16
F32 = jnp.float32


def _rms(x, w):
    return x * lax.rsqrt(jnp.mean(x * x, axis=-1, keepdims=True) + EPS) * w


def _dot(a, b):
    return jnp.dot(a, b, preferred_element_type=F32)


def _dot_nt(a, b):
    return lax.dot_general(a, b, (((1,), (1,)), ((), ())), preferred_element_type=F32)


def _const_spec(shape):
    zeros = (0,) * len(shape)
    return pl.BlockSpec(shape, lambda *_: zeros, pipeline_mode=pl.Buffered(1))


def _params(semantics):
    return pltpu.CompilerParams(dimension_semantics=semantics, vmem_limit_bytes=VMEM_LIMIT)


def _in_proj_kernel(n_cast, x_ref, nw_ref, wt_ref, w2_ref, gb_ref, bg_ref, gn_ref, *rest):
    cast_src, cast_dst = rest[:n_cast], rest[len(rest) - n_cast:]
    (qa_ref, ka_ref, vat_ref, qg_ref, kg_ref, vgt_ref, gg_ref, gk_ref,
     gate_ref) = rest[n_cast:len(rest) - n_cast]
    for src, dst in zip(cast_src, cast_dst):
        dst[...] = src[...].astype(BF16)
    o_qa, o_ka, o_va, o_qg, o_kg, o_vg, o_gg, o_alr, o_gate, o_end = IN_OFFSETS
    hd = SWA_HEAD_DIM
    h = _rms(x_ref[...], nw_ref[...]).astype(BF16)
    half = (o_end - o_gate) // 2

    def branch_gates(lo):
        pre = _dot_nt(h, wt_ref[o_gate + lo:o_gate + lo + half, :]) + bg_ref[:, lo:lo + half]
        gate_ref[:, lo:lo + half] = jax.nn.sigmoid(pre).astype(BF16)

    branch_gates(0)
    group = SWA_HEADS // SWA_KV_HEADS
    wq = jnp.concatenate(
        [wt_ref[o_qa + (p + j * group) * hd:o_qa + (p + j * group + 1) * hd, :]
         for p in range(group) for j in range(SWA_KV_HEADS)], axis=0)
    qa_ref[...] = (_dot_nt(h, wq) * (hd ** -0.5 * LOG2E)).astype(BF16)
    branch_gates(half)
    vgt_ref[...] = _dot_nt(wt_ref[o_vg:o_gg, :], h).astype(BF16)
    g = _dot_nt(h, wt_ref[o_gg:o_alr, :])
    gg_ref[...] = (g * jax.nn.sigmoid(g) * gn_ref[...]).astype(BF16)
    kw = SWA_KV_HEADS * hd
    k_alr = _dot_nt(h, jnp.concatenate([wt_ref[o_ka:o_va, :], wt_ref[o_alr:o_alr + LANES, :]],
                                       axis=0))
    ka_ref[...] = k_alr[:, :kw].astype(BF16)
    qg_ref[...] = _dot_nt(h, wt_ref[o_qg:o_kg, :]).astype(BF16)
    z = _dot(k_alr[:, kw:].astype(BF16), w2_ref[...]) + gb_ref[...]
    log_sig = jnp.minimum(z, 0.0) - jnp.log(1.0 + jnp.exp(-jnp.abs(z)))
    gk_ref[...] = (log_sig * (1.0 / GLA_GATE_NORM)).astype(BF16)
    kg_ref[...] = _dot_nt(h, wt_ref[o_kg:o_vg, :]).astype(BF16)
    vat_ref[...] = _dot_nt(wt_ref[o_va:o_qg, :], h).astype(BF16)


def _in_proj(x2d, nw, wt, w2, gb, bg, gn, w_attn_o, w_gla_o, w_mix_o):
    t = x2d.shape[0]
    tm = TM_PROJ
    steps = t // tm
    row = lambda n: pl.BlockSpec((tm, n), lambda i: (i, 0))
    col = lambda n: pl.BlockSpec((n, tm), lambda i: (0, i))
    consts = (nw, wt, w2, gb, bg, gn)
    kw = SWA_KV_HEADS * SWA_HEAD_DIM
    tok = lambda n: jax.ShapeDtypeStruct((t, n), BF16)
    out_shape = [
        tok(SWA_HEADS * SWA_HEAD_DIM), tok(kw), jax.ShapeDtypeStruct((kw, t), BF16),
        tok(GLA_KEY_DIM), tok(GLA_KEY_DIM), jax.ShapeDtypeStruct((GLA_VAL_DIM, t), BF16),
        tok(GLA_VAL_DIM), tok(GLA_KEY_DIM), tok(2 * D_MODEL),
    ]
    out_specs = [
        row(SWA_HEADS * SWA_HEAD_DIM), row(kw), col(kw),
        row(GLA_KEY_DIM), row(GLA_KEY_DIM), col(GLA_VAL_DIM),
        row(GLA_VAL_DIM), row(GLA_KEY_DIM), row(2 * D_MODEL),
    ]
    cast_w = (w_attn_o, w_gla_o, w_mix_o)
    slab_rows = D_MODEL // steps
    group = SWA_HEADS // SWA_KV_HEADS
    assert slab_rows == SWA_HEAD_DIM and steps == SWA_HEADS and slab_rows % 16 == 0
    slab = pl.BlockSpec((slab_rows, D_MODEL), lambda i: (i, 0))
    head_slab = pl.BlockSpec((slab_rows, D_MODEL),
                             lambda i: (i // SWA_KV_HEADS + group * (i % SWA_KV_HEADS), 0))
    return pl.pallas_call(
        functools.partial(_in_proj_kernel, len(cast_w)),
        grid=(steps,),
        in_specs=[row(D_MODEL)] + [_const_spec(c.shape) for c in consts]
        + [head_slab, slab, slab],
        out_specs=out_specs + [slab] * len(cast_w),
        out_shape=out_shape + [jax.ShapeDtypeStruct(w.shape, BF16) for w in cast_w],
        compiler_params=_params(("arbitrary",)),
        name="in_proj",
    )(x2d, *consts, *cast_w)


def _swa_stages(sink_ref, q_ref, k_ref, kp_ref, vt_ref, vtp_ref, seq_start, write):
    blk = SWA_BLOCK
    win = blk + CHUNK
    low = lax.broadcasted_iota(jnp.int32, (CHUNK, LANES), 1) < SWA_HEAD_DIM
    low_row = lax.broadcasted_iota(jnp.int32, (1, LANES), 1) < SWA_HEAD_DIM
    prev_bias = jnp.where(seq_start, -1e30, 0.0).astype(F32)
    k_all = jnp.concatenate([kp_ref[...], k_ref[...]], axis=0)
    vt_all = jnp.concatenate([vtp_ref[...], vt_ref[...]], axis=1)
    zeros_chunk = jnp.zeros((CHUNK, LANES), BF16)
    n_pairs = SWA_HEADS // SWA_KV_HEADS
    n_blocks = q_ref.shape[0] // blk
    n_ch = blk // CHUNK
    live = {}

    def scores(p):
        res = []
        for u in range(n_blocks):
            qsel = []
            for qh in range(n_ch):
                r0 = u * blk + qh * CHUNK
                qp = q_ref[r0:r0 + CHUNK, p * LANES:(p + 1) * LANES]
                zero = jnp.zeros_like(qp)
                qsel += [jnp.where(low, qp, zero), jnp.where(low, zero, qp)]
            res.append(_dot_nt(k_all[u * blk:(u + 2) * blk], jnp.concatenate(qsel, axis=0)))
        live["s", p] = res

    def softmax_values(p):
        sink_row = jnp.where(low_row, sink_ref[p], sink_ref[p + n_pairs]) * LOG2E
        res = []
        for u, s_blk in enumerate(live.pop(("s", p))):
            probs, inv = [], []
            for qh in range(n_ch):
                st = s_blk[qh * CHUNK:qh * CHUNK + win, qh * LANES:(qh + 1) * LANES]
                if u == 0:
                    n_prev = blk - qh * CHUNK
                    st = jnp.concatenate([st[:n_prev] + prev_bias, st[n_prev:]], axis=0)
                m = jnp.maximum(jnp.max(st, axis=0, keepdims=True), sink_row)
                e = jnp.exp2(st - m)
                inv.append(1.0 / (jnp.sum(e, axis=0, keepdims=True) + jnp.exp2(sink_row - m)))
                probs.append(jnp.concatenate([zeros_chunk] * qh + [e.astype(BF16)]
                                             + [zeros_chunk] * (n_ch - 1 - qh), axis=0))
            ot = _dot(vt_all[:, u * blk:(u + 2) * blk], jnp.concatenate(probs, axis=1))
            res.append([ot[:, qh * LANES:(qh + 1) * LANES] * inv[qh] for qh in range(n_ch)])
        live["o", p] = res

    def store(p):
        for u, ots in enumerate(live.pop(("o", p))):
            for qh, ot in enumerate(ots):
                r0 = u * blk + qh * CHUNK
                o2 = ot.T
                write(slice(r0, r0 + CHUNK), slice(p * LANES, (p + 1) * LANES),
                      jnp.where(low, o2[:CHUNK], o2[CHUNK:]).astype(BF16))

    return [(functools.partial(scores, p), functools.partial(softmax_values, p),
             functools.partial(store, p)) for p in range(n_pairs)]


def _gla_stages(q_ref, k_ref, gk_ref, vt_ref, g_ref, st_ref, seq_start, write):
    pair = 2 * CHUNK
    heads = range(GLA_HEADS)
    ks = [slice(h * GLA_DK, (h + 1) * GLA_DK) for h in heads]
    vs = [slice(h * GLA_DV, (h + 1) * GLA_DV) for h in heads]
    r = lax.broadcasted_iota(jnp.int32, (pair, pair), 0)
    c = lax.broadcasted_iota(jnp.int32, (pair, pair), 1)
    tri = ((c <= r) & ((r // CHUNK) == (c // CHUNK))).astype(BF16)
    first = lax.broadcasted_iota(jnp.int32, (pair, GLA_DK), 0) < CHUNK
    eps_scaled = EPS * GLA_DK
    n_pairs = q_ref.shape[0] // pair
    sts = [jnp.where(seq_start, jnp.zeros((GLA_DV, GLA_DK), F32), st_ref[h]) for h in heads]
    live = {}

    def decay_and_kv(i):
        sl = slice(i * pair, (i + 1) * pair)
        b_all = _dot(tri, gk_ref[sl, :])
        res = []
        for h in heads:
            b = b_all[:, ks[h]]
            b_end0 = b[CHUNK - 1:CHUNK, :]
            b_end1 = b[pair - 1:pair, :]
            k_dec = (k_ref[sl, ks[h]].astype(F32)
                     * jnp.exp(jnp.where(first, b_end0, b_end1) - b)).astype(BF16)
            zero = jnp.zeros_like(k_dec)
            kd = jnp.concatenate([jnp.where(first, k_dec, zero), jnp.where(first, zero, k_dec)],
                                 axis=1)
            kv = _dot(vt_ref[vs[h], sl], kd)
            res.append((jnp.exp(b_end0), jnp.exp(b_end1), kv))
        live[i] = res

    def state_and_out(i):
        sl = slice(i * pair, (i + 1) * pair)
        dk = live.pop(i)
        outs = []
        for h in heads:
            a0, a1, kv = dk[h]
            st0 = sts[h] * a0 + kv[:, :GLA_DK]
            st1 = st0 * a1 + kv[:, GLA_DK:]
            sts[h] = st1
            q = q_ref[sl, ks[h]]
            outs.append(jnp.concatenate([_dot_nt(q[:CHUNK], st0.astype(BF16)),
                                         _dot_nt(q[CHUNK:], st1.astype(BF16))], axis=0))
        for h in heads:
            o = outs[h]
            inv = lax.rsqrt(jnp.mean(o * o, axis=-1, keepdims=True) + eps_scaled)
            write(sl, vs[h], (o * inv).astype(BF16) * g_ref[sl, vs[h]])

    def save_state():
        for h in heads:
            st_ref[h] = sts[h]

    stages = [functools.partial(decay_and_kv, 0)]
    for i in range(n_pairs):
        if i + 1 < n_pairs:
            stages.append(functools.partial(decay_and_kv, i + 1))
        stages.append(functools.partial(state_and_out, i))
    stages.append(save_state)
    return stages


def _mix_stages(x_ref, gate_ref, wa_ref, wg_ref, wm_ref, oa_ref, og_ref, mg_ref, o_ref):
    pieces = [slice(c * MIX_CHUNK, (c + 1) * MIX_CHUNK) for c in range(D_MODEL // MIX_CHUNK)]

    def merge(cs):
        ya = _dot(oa_ref[...], wa_ref[:, cs])
        yg = _dot(og_ref[...], wg_ref[:, cs])
        cs_b = slice(D_MODEL + cs.start, D_MODEL + cs.stop)
        mg_ref[:, cs] = (gate_ref[:, cs].astype(F32) * ya
                         + gate_ref[:, cs_b].astype(F32) * yg).astype(BF16)

    def project(cs):
        o_ref[:, cs] = x_ref[:, cs] + _dot(mg_ref[...], wm_ref[:, cs])

    return ([functools.partial(merge, cs) for cs in pieces]
            + [functools.partial(project, cs) for cs in pieces])


def _interleave(major, minor):
    out, done = [], 0
    for n, stage in enumerate(major):
        out.append(stage)
        want = (n + 1) * len(minor) // len(major)
        out.extend(minor[done:want])
        done = want
    return out


def _run_lookahead(i, n_tiles, lead, lag, merge, init=None):
    @pl.when(i == 0)
    def _():
        if init is not None:
            init()
        for stage in merge(lead(), []):
            stage()

    @pl.when(jnp.logical_and(i > 0, i < n_tiles))
    def _():
        for stage in merge(lead(), lag()):
            stage()

    @pl.when(i == n_tiles)
    def _():
        for stage in merge([], lag()):
            stage()


def _attn_mix_kernel(tiles_per_seq, n_cast,
                     sink_ref, q_ref, k_ref, kp_ref, vt_ref, vtp_ref,
                     qg_ref, kg_ref, gk_ref, vgt_ref, gg_ref,
                     x_ref, gate_ref, wa_ref, wg_ref, wm_ref, *rest):
    cast_src, (o_ref, *cast_dst) = rest[:n_cast], rest[n_cast:2 * n_cast + 1]
    oa_scr, og_scr, mg_scr, st_ref = rest[2 * n_cast + 1:]
    i = pl.program_id(0)
    n_tiles = pl.num_programs(0) - 1
    seq_start = (i % tiles_per_seq) == 0
    slot_w = i % 2
    slot_r = 1 - slot_w

    def write_oa(rows, cols, val):
        oa_scr[slot_w, rows, cols] = val

    def write_og(rows, cols, val):
        og_scr[slot_w, rows, cols] = val

    def mixers():
        swa = _swa_stages(sink_ref, q_ref, k_ref, kp_ref, vt_ref, vtp_ref, seq_start, write_oa)
        gla = _gla_stages(qg_ref, kg_ref, gk_ref, vgt_ref, gg_ref, st_ref, seq_start, write_og)
        n = len(swa)
        order = [swa[0][0]]
        for p in range(n):
            if p + 1 < n:
                order.append(swa[p + 1][0])
            order.append(swa[p][1])
            if p > 0:
                order.append(swa[p - 1][2])
        order.append(swa[n - 1][2])
        return _interleave(order, gla) + [convert_weights]

    def projections():
        return _mix_stages(x_ref, gate_ref, wa_ref, wg_ref, wm_ref,
                           oa_scr.at[slot_r], og_scr.at[slot_r], mg_scr, o_ref)

    def merge(lead, lag):
        return _interleave(lead, lag) if lead else lag

    def convert_weights():
        for src, dst in zip(cast_src, cast_dst):
            dst[...] = src[...].astype(BF16)

    def init():
        st_ref[...] = jnp.zeros_like(st_ref)

    _run_lookahead(i, n_tiles, mixers, projections, merge, init)


def _attn_mix(sinks, qa, ka, vat, qg, kg, gk, vgt, gg, x2d, gates, wa, wg, wm, cast_w, seq):
    t = x2d.shape[0]
    tm = TM_ATTN
    n_tiles = t // tm
    blocks = tm // SWA_BLOCK
    kw = ka.shape[1]
    cur = lambda i: jnp.minimum(i, n_tiles - 1)
    prv = lambda i: jnp.maximum(i - 1, 0)
    prev_block = lambda i: jnp.maximum(cur(i) * blocks - 1, 0)
    tok = lambda n: pl.BlockSpec((tm, n), lambda i: (cur(i), 0))
    col = lambda n: pl.BlockSpec((n, tm), lambda i: (0, cur(i)))
    old = lambda n: pl.BlockSpec((tm, n), lambda i: (prv(i), 0))

    def slab(w):
        n_slabs = max(n for n in range(1, n_tiles + 1) if w.shape[0] % (16 * n) == 0)
        return pl.BlockSpec((w.shape[0] // n_slabs, w.shape[1]),
                            lambda i: (jnp.minimum(i, n_slabs - 1), 0))

    cast_specs = [slab(w) for w in cast_w]
    return pl.pallas_call(
        functools.partial(_attn_mix_kernel, seq // tm, len(cast_w)),
        grid=(n_tiles + 1,),
        in_specs=[
            pl.BlockSpec(memory_space=pltpu.SMEM),
            tok(qa.shape[1]), tok(kw),
            pl.BlockSpec((SWA_BLOCK, kw), lambda i: (prev_block(i), 0)),
            col(kw),
            pl.BlockSpec((kw, SWA_BLOCK), lambda i: (0, prev_block(i))),
            tok(GLA_KEY_DIM), tok(GLA_KEY_DIM), tok(GLA_KEY_DIM), col(GLA_VAL_DIM),
            tok(GLA_VAL_DIM),
            old(D_MODEL), old(2 * D_MODEL),
            _const_spec(wa.shape), _const_spec(wg.shape), _const_spec(wm.shape),
        ] + cast_specs,
        out_specs=[old(D_MODEL)] + cast_specs,
        out_shape=[jax.ShapeDtypeStruct((t, D_MODEL), F32)]
        + [jax.ShapeDtypeStruct(w.shape, BF16) for w in cast_w],
        scratch_shapes=[
            pltpu.VMEM((2, tm, SWA_HEADS * SWA_HEAD_DIM), BF16),
            pltpu.VMEM((2, tm, GLA_VAL_DIM), BF16),
            pltpu.VMEM((tm, D_MODEL), BF16),
            pltpu.VMEM((GLA_HEADS, GLA_DV, GLA_DK), F32),
        ],
        compiler_params=_params(("arbitrary",)),
        name="attn_mix",
    )(sinks, qa, ka, ka, vat, vat, qg, kg, gk, vgt, gg, x2d, gates, wa, wg, wm, *cast_w)


def _mem_kv_kernel(m_ref, nw_ref, w_ref, k_ref, v_ref):
    mn = _rms(m_ref[0], nw_ref[...]).astype(BF16)
    kv = _dot(mn, w_ref[...])
    k = kv[:, :MEM_WIDTH].astype(BF16)
    v = kv[:, MEM_WIDTH:].astype(BF16)
    head = lax.broadcasted_iota(jnp.int32, (N_MEM, MEM_WIDTH), 1) // MEM_HEAD_DIM
    zero = jnp.zeros_like(k)
    for h in range(MEM_HEADS):
        k_ref[0, h] = jnp.where(head == h, k, zero)
        v_ref[0, h] = jnp.where(head == h, v, zero)


def _mem_kv(mem, nw, w):
    batch = mem.shape[0]
    out = jax.ShapeDtypeStruct((batch, MEM_HEADS, N_MEM, MEM_WIDTH), BF16)
    spec = pl.BlockSpec((1, MEM_HEADS, N_MEM, MEM_WIDTH), lambda b: (b, 0, 0, 0))
    return pl.pallas_call(
        _mem_kv_kernel,
        grid=(batch,),
        in_specs=[pl.BlockSpec((1, N_MEM, D_MODEL), lambda b: (b, 0, 0)),
                  _const_spec(nw.shape), _const_spec(w.shape)],
        out_specs=(spec, spec),
        out_shape=(out, out),
        compiler_params=_params(("arbitrary",)),
        name="mem_kv",
    )(mem, nw, w)


def _xattn_stages(x_ref, k_ref, v_ref, nq_ref, wq_ref, wo_ref, nf_ref, x2_ref, hf_ref):
    live = {}

    def queries():
        hq = _rms(x_ref[...], nq_ref[...]).astype(BF16)
        live["q"] = (_dot(hq, wq_ref[...]) * (MEM_HEAD_DIM ** -0.5)).astype(BF16)
        live["o"] = jnp.zeros((x_ref.shape[0], MEM_WIDTH), F32)

    def head(h):
        s = _dot_nt(live["q"], k_ref[0, h])
        e = jnp.exp(s - jnp.max(s, axis=-1, keepdims=True))
        inv = 1.0 / jnp.sum(e, axis=-1, keepdims=True)
        live["o"] = live["o"] + _dot(e.astype(BF16), v_ref[0, h]) * inv

    def project():
        x2 = x_ref[...] + _dot(live.pop("o").astype(BF16), wo_ref[...])
        x2_ref[...] = x2
        hf_ref[...] = _rms(x2, nf_ref[...]).astype(BF16)

    return [queries] + [functools.partial(head, h) for h in range(MEM_HEADS)] + [project]


def _ffn_stages(x2_ref, hf_ref, wg_ref, wu_ref, wd_ref, nfin_ref, act_ref, o_ref):
    live = {"ss": jnp.zeros((o_ref.shape[0], 1), F32)}

    def hidden(cs):
        hf = hf_ref[...]
        g = _dot(hf, wg_ref[:, cs])
        u = _dot(hf, wu_ref[:, cs])
        act_ref[:, cs] = (g * jax.nn.sigmoid(g) * u).astype(BF16)

    def down(cs):
        y = x2_ref[:, cs] + _dot(act_ref[...], wd_ref[:, cs])
        live["ss"] = live["ss"] + jnp.sum(y * y, axis=-1, keepdims=True)
        o_ref[:, cs] = y

    def final_norm():
        inv = lax.rsqrt(live["ss"] * (1.0 / D_MODEL) + EPS)
        o_ref[...] = o_ref[...] * inv * nfin_ref[...]

    hid = [slice(c * FF_CHUNK, (c + 1) * FF_CHUNK) for c in range(D_FF // FF_CHUNK)]
    out = [slice(c * MIX_CHUNK, (c + 1) * MIX_CHUNK) for c in range(D_MODEL // MIX_CHUNK)]
    return ([functools.partial(hidden, cs) for cs in hid]
            + [functools.partial(down, cs) for cs in out] + [final_norm])


def _tail_kernel(x_ref, k_ref, v_ref, nq_ref, wq_ref, wo_ref, nf_ref, wg_ref, wu_ref, wd_ref,
                 nfin_ref, o_ref, x2_scr, hf_scr, act_ref):
    i = pl.program_id(0)
    n_tiles = pl.num_programs(0) - 1
    slot_w = i % 2
    slot_r = 1 - slot_w

    def xattn():
        return _xattn_stages(x_ref, k_ref, v_ref, nq_ref, wq_ref, wo_ref, nf_ref,
                             x2_scr.at[slot_w], hf_scr.at[slot_w])

    def ffn():
        return _ffn_stages(x2_scr.at[slot_r], hf_scr.at[slot_r], wg_ref, wu_ref, wd_ref,
                           nfin_ref, act_ref, o_ref)

    def merge(lead, lag):
        if not lag:
            return lead
        n_hidden = D_FF // FF_CHUNK
        n_first = n_hidden - len(lead)
        return lag[:n_first] + _interleave(lag[n_first:n_hidden], lead) + lag[n_hidden:]

    _run_lookahead(i, n_tiles, xattn, ffn, merge)


def _tail(x2d, kext, vext, nq, wq, wo, nf, wg, wu, wd, nfin, seq):
    tm = TM_TAIL
    t = x2d.shape[0]
    n_tiles = t // tm
    tiles_per_seq = seq // tm
    cur = lambda i: jnp.minimum(i, n_tiles - 1)
    prv = lambda i: jnp.maximum(i - 1, 0)
    mem_spec = pl.BlockSpec((1, MEM_HEADS, N_MEM, MEM_WIDTH),
                            lambda i: (cur(i) // tiles_per_seq, 0, 0, 0))
    consts = (nq, wq, wo, nf, wg, wu, wd, nfin)
    return pl.pallas_call(
        _tail_kernel,
        grid=(n_tiles + 1,),
        in_specs=[pl.BlockSpec((tm, D_MODEL), lambda i: (cur(i), 0)), mem_spec, mem_spec]
        + [_const_spec(c.shape) for c in consts],
        out_specs=pl.BlockSpec((tm, D_MODEL), lambda i: (prv(i), 0)),
        out_shape=jax.ShapeDtypeStruct((t, D_MODEL), F32),
        scratch_shapes=[pltpu.VMEM((2, tm, D_MODEL), F32), pltpu.VMEM((2, tm, D_MODEL), BF16),
                        pltpu.VMEM((tm, D_FF), BF16)],
        compiler_params=_params(("arbitrary",)),
        name="tail",
    )(x2d, kext, vext, *consts)


def _layer(x2d, mem, batch, seq, norm_mix_w, w_in, b_gate, attn_sinks, gla_gate_w2, gla_gate_b,
           gla_norm_w, w_attn_o, w_gla_o, w_mix_o, norm_mem_q_w, norm_mem_kv_w, w_mem_q,
           w_mem_kv, w_mem_o, norm_ffn_w, w_ffn_gate, w_ffn_up, w_ffn_down, out_norm_w):
    w2_p = jnp.pad(gla_gate_w2, ((0, LANES - GLA_GATE_RANK), (0, 0)))
    bf = lambda a: a.astype(BF16)
    r2 = lambda a: a.reshape(1, -1)

    qa, ka, vat, qg, kg, vgt, gg, gk, gates, wa16, wgo16, wmo16 = _in_proj(
        x2d, r2(norm_mix_w), bf(w_in.T), bf(w2_p), r2(gla_gate_b), r2(b_gate),
        r2(jnp.tile(gla_norm_w, GLA_HEADS)), w_attn_o, w_gla_o, w_mix_o)
    x1, wg16, wu16, wd16, wq16, wo16, wkv16 = _attn_mix(
        attn_sinks, qa, ka, vat, qg, kg, gk, vgt, gg, x2d, gates, wa16, wgo16, wmo16,
        (w_ffn_gate, w_ffn_up, w_ffn_down, w_mem_q, w_mem_o, w_mem_kv), seq)
    kext, vext = _mem_kv(mem, r2(norm_mem_kv_w), wkv16)
    return _tail(x1, kext, vext, r2(norm_mem_q_w), wq16, wo16, r2(norm_ffn_w),
                 wg16, wu16, wd16, r2(out_norm_w), seq)


def kernel(x, mem, norm_mix_w, w_in, b_gate, attn_sinks, gla_gate_w2, gla_gate_b, gla_norm_w,
           w_attn_o, w_gla_o, w_mix_o, norm_mem_q_w, norm_mem_kv_w, w_mem_q, w_mem_kv, w_mem_o,
           norm_ffn_w, w_ffn_gate, w_ffn_up, w_ffn_down, norm_final_w):
    batch, seq, d = x.shape
    depth = w_in.shape[0]
    assert depth == 1 and d == D_MODEL
    assert seq % TM_PROJ == 0 and seq % TM_ATTN == 0 and seq % TM_TAIL == 0
    out = _layer(x.reshape(batch * seq, d), mem, batch, seq, norm_mix_w[0], w_in[0], b_gate[0],
                 attn_sinks[0], gla_gate_w2[0], gla_gate_b[0], gla_norm_w[0], w_attn_o[0],
                 w_gla_o[0], w_mix_o[0], norm_mem_q_w[0], norm_mem_kv_w[0], w_mem_q[0],
                 w_mem_kv[0], w_mem_o[0], norm_ffn_w[0], w_ffn_gate[0], w_ffn_up[0],
                 w_ffn_down[0], norm_final_w)
    return out.reshape(batch, seq, d)
```

```python
import functools

import jax
import jax.numpy as jnp
from jax import lax
from jax.experimental import pallas as pl
from jax.experimental.pallas import tpu as pltpu

D_MODEL = 1024
CHUNK = 64
N_MEM = 256
EPS = 1e-6

SWA_HEADS = 16
SWA_KV_HEADS = 2
SWA_HEAD_DIM = 64
SWA_BLOCK = 128

GLA_HEADS = 4
GLA_KEY_DIM = D_MODEL // 2
GLA_VAL_DIM = D_MODEL
GLA_DK = GLA_KEY_DIM // GLA_HEADS
GLA_DV = GLA_VAL_DIM // GLA_HEADS
GLA_GATE_RANK = 16
GLA_GATE_NORM = 16.0

MEM_HEADS = 4
MEM_HEAD_DIM = 64
MEM_WIDTH = MEM_HEADS * MEM_HEAD_DIM

D_FF = -(-(8 * D_MODEL) // (3 * 256)) * 256

IN_SIZES = (SWA_HEADS * SWA_HEAD_DIM, SWA_KV_HEADS * SWA_HEAD_DIM, SWA_KV_HEADS * SWA_HEAD_DIM,
            GLA_KEY_DIM, GLA_KEY_DIM, GLA_VAL_DIM, GLA_VAL_DIM, GLA_GATE_RANK, 2 * D_MODEL)
IN_OFFSETS = tuple(sum(IN_SIZES[:i]) for i in range(len(IN_SIZES) + 1))

LANES = 128
VMEM_LIMIT = 56 * 1024 * 1024

LOG2E = 1.4426950408889634

TM_PROJ = 1024
W_IN_CHUNK = 128
TM_ATTN = 512
TM_TAIL = 512
MIX_CHUNK = 256
FF_CHUNK = 256

BF16 = jnp.bfloat16
F32 = jnp.float32


def _rms(x, w):
    return x * lax.rsqrt(jnp.mean(x * x, axis=-1, keepdims=True) + EPS) * w


def _dot(a, b):
    return jnp.dot(a, b, preferred_element_type=F32)


def _dot_nt(a, b):
    return lax.dot_general(a, b, (((1,), (1,)), ((), ())), preferred_element_type=F32)


def _const_spec(shape):
    zeros = (0,) * len(shape)
    return pl.BlockSpec(shape, lambda *_: zeros, pipeline_mode=pl.Buffered(1))


def _params(semantics):
    return pltpu.CompilerParams(dimension_semantics=semantics, vmem_limit_bytes=VMEM_LIMIT)


def _load_cast_rows(src_hbm, dst_ref, stage_ref, sem_ref):
    rows, chunk = src_hbm.shape[0], stage_ref.shape[1]
    chunks = [(r0, min(chunk, rows - r0)) for r0 in range(0, rows, chunk)]

    def copy(c):
        r0, n = chunks[c]
        return pltpu.make_async_copy(src_hbm.at[pl.ds(r0, n), :],
                                     stage_ref.at[c % 2, pl.ds(0, n), :], sem_ref.at[c % 2])

    copy(0).start()
    for c, (r0, n) in enumerate(chunks):
        if c + 1 < len(chunks):
            copy(c + 1).start()
        copy(c).wait()
        dst_ref[r0:r0 + n, :] = stage_ref[c % 2, 0:n, :].astype(BF16)


def _in_proj_kernel(n_cast, x_ref, nw_ref, wt_hbm, w2_ref, gb_ref, bg_ref, gn_ref, *rest):
    cast_src, cast_dst = rest[:n_cast], rest[len(rest) - n_cast - 3:len(rest) - 3]
    (qa_ref, ka_ref, vat_ref, qg_ref, kg_ref, vgt_ref, gg_ref, gk_ref,
     gate_ref) = rest[n_cast:len(rest) - n_cast - 3]
    wt_ref, stage_ref, sem_ref = rest[len(rest) - 3:]

    @pl.when(pl.program_id(0) == 0)
    def _():
        _load_cast_rows(wt_hbm, wt_ref, stage_ref, sem_ref)

    for src, dst in zip(cast_src, cast_dst):
        dst[...] = src[...].astype(BF16)
    o_qa, o_ka, o_va, o_qg, o_kg, o_vg, o_gg, o_alr, o_gate, o_end = IN_OFFSETS
    hd = SWA_HEAD_DIM
    h = _rms(x_ref[...], nw_ref[...]).astype(BF16)
    half = (o_end - o_gate) // 2

    def branch_gates(lo):
        pre = _dot_nt(h, wt_ref[o_gate + lo:o_gate + lo + half, :]) + bg_ref[:, lo:lo + half]
        gate_ref[:, lo:lo + half] = jax.nn.sigmoid(pre).astype(BF16)

    branch_gates(0)
    group = SWA_HEADS // SWA_KV_HEADS
    wq = jnp.concatenate(
        [wt_ref[o_qa + (p + j * group) * hd:o_qa + (p + j * group + 1) * hd, :]
         for p in range(group) for j in range(SWA_KV_HEADS)], axis=0)
    qa_ref[...] = (_dot_nt(h, wq) * (hd ** -0.5 * LOG2E)).astype(BF16)
    branch_gates(half)
    vgt_ref[...] = _dot_nt(wt_ref[o_vg:o_gg, :], h).astype(BF16)
    g = _dot_nt(h, wt_ref[o_gg:o_alr, :])
    gg_ref[...] = (g * jax.nn.sigmoid(g) * gn_ref[...]).astype(BF16)
    kw = SWA_KV_HEADS * hd
    k_alr = _dot_nt(h, jnp.concatenate([wt_ref[o_ka:o_va, :], wt_ref[o_alr:o_alr + LANES, :]],
                                       axis=0))
    ka_ref[...] = k_alr[:, :kw].astype(BF16)
    qg_ref[...] = _dot_nt(h, wt_ref[o_qg:o_kg, :]).astype(BF16)
    z = _dot(k_alr[:, kw:].astype(BF16), w2_ref[...]) + gb_ref[...]
    log_sig = jnp.minimum(z, 0.0) - jnp.log(1.0 + jnp.exp(-jnp.abs(z)))
    gk_ref[...] = (log_sig * (1.0 / GLA_GATE_NORM)).astype(BF16)
    kg_ref[...] = _dot_nt(h, wt_ref[o_kg:o_vg, :]).astype(BF16)
    vat_ref[...] = _dot_nt(wt_ref[o_va:o_qg, :], h).astype(BF16)


def _in_proj(x2d, nw, wt, w2, gb, bg, gn, w_attn_o, w_gla_o, w_mix_o):
    t = x2d.shape[0]
    tm = TM_PROJ
    steps = t // tm
    row = lambda n: pl.BlockSpec((tm, n), lambda i: (i, 0))
    col = lambda n: pl.BlockSpec((n, tm), lambda i: (0, i))
    consts = (nw, wt, w2, gb, bg, gn)
    const_specs = [pl.BlockSpec(memory_space=pl.ANY) if c is wt else _const_spec(c.shape)
                   for c in consts]
    kw = SWA_KV_HEADS * SWA_HEAD_DIM
    tok = lambda n: jax.ShapeDtypeStruct((t, n), BF16)
    out_shape = [
        tok(SWA_HEADS * SWA_HEAD_DIM), tok(kw), jax.ShapeDtypeStruct((kw, t), BF16),
        tok(GLA_KEY_DIM), tok(GLA_KEY_DIM), jax.ShapeDtypeStruct((GLA_VAL_DIM, t), BF16),
        tok(GLA_VAL_DIM), tok(GLA_KEY_DIM), tok(2 * D_MODEL),
    ]
    out_specs = [
        row(SWA_HEADS * SWA_HEAD_DIM), row(kw), col(kw),
        row(GLA_KEY_DIM), row(GLA_KEY_DIM), col(GLA_VAL_DIM),
        row(GLA_VAL_DIM), row(GLA_KEY_DIM), row(2 * D_MODEL),
    ]
    cast_w = (w_attn_o, w_gla_o, w_mix_o)
    slab_rows = D_MODEL // steps
    group = SWA_HEADS // SWA_KV_HEADS
    assert slab_rows == SWA_HEAD_DIM and steps == SWA_HEADS and slab_rows % 16 == 0
    slab = pl.BlockSpec((slab_rows, D_MODEL), lambda i: (i, 0))
    head_slab = pl.BlockSpec((slab_rows, D_MODEL),
                             lambda i: (i // SWA_KV_HEADS + group * (i % SWA_KV_HEADS), 0))
    return pl.pallas_call(
        functools.partial(_in_proj_kernel, len(cast_w)),
        grid=(steps,),
        in_specs=[row(D_MODEL)] + const_specs + [head_slab, slab, slab],
        out_specs=out_specs + [slab] * len(cast_w),
        out_shape=out_shape + [jax.ShapeDtypeStruct(w.shape, BF16) for w in cast_w],
        scratch_shapes=[pltpu.VMEM(wt.shape, BF16), pltpu.VMEM((2, W_IN_CHUNK, D_MODEL), F32),
                        pltpu.SemaphoreType.DMA((2,))],
        compiler_params=_params(("arbitrary",)),
        name="in_proj",
    )(x2d, *consts, *cast_w)


def _swa_stages(sink_ref, q_ref, k_ref, kp_ref, vt_ref, vtp_ref, seq_start, write):
    blk = SWA_BLOCK
    win = blk + CHUNK
    low = lax.broadcasted_iota(jnp.int32, (CHUNK, LANES), 1) < SWA_HEAD_DIM
    low_row = lax.broadcasted_iota(jnp.int32, (1, LANES), 1) < SWA_HEAD_DIM
    prev_bias = jnp.where(seq_start, -1e30, 0.0).astype(F32)
    k_all = jnp.concatenate([kp_ref[...], k_ref[...]], axis=0)
    vt_all = jnp.concatenate([vtp_ref[...], vt_ref[...]], axis=1)
    zeros_chunk = jnp.zeros((CHUNK, LANES), BF16)
    n_pairs = SWA_HEADS // SWA_KV_HEADS
    n_blocks = q_ref.shape[0] // blk
    n_ch = blk // CHUNK
    live = {}

    def scores(p):
        res = []
        for u in range(n_blocks):
            qsel = []
            for qh in range(n_ch):
                r0 = u * blk + qh * CHUNK
                qp = q_ref[r0:r0 + CHUNK, p * LANES:(p + 1) * LANES]
                zero = jnp.zeros_like(qp)
                qsel += [jnp.where(low, qp, zero), jnp.where(low, zero, qp)]
            res.append(_dot_nt(k_all[u * blk:(u + 2) * blk], jnp.concatenate(qsel, axis=0)))
        live["s", p] = res

    def softmax_values(p):
        sink_row = jnp.where(low_row, sink_ref[p], sink_ref[p + n_pairs]) * LOG2E
        res = []
        for u, s_blk in enumerate(live.pop(("s", p))):
            probs, inv = [], []
            for qh in range(n_ch):
                st = s_blk[qh * CHUNK:qh * CHUNK + win, qh * LANES:(qh + 1) * LANES]
                if u == 0:
                    n_prev = blk - qh * CHUNK
                    st = jnp.concatenate([st[:n_prev] + prev_bias, st[n_prev:]], axis=0)
                m = jnp.maximum(jnp.max(st, axis=0, keepdims=True), sink_row)
                e = jnp.exp2(st - m)
                inv.append(1.0 / (jnp.sum(e, axis=0, keepdims=True) + jnp.exp2(sink_row - m)))
                probs.append(jnp.concatenate([zeros_chunk] * qh + [e.astype(BF16)]
                                             + [zeros_chunk] * (n_ch - 1 - qh), axis=0))
            ot = _dot(vt_all[:, u * blk:(u + 2) * blk], jnp.concatenate(probs, axis=1))
            res.append([ot[:, qh * LANES:(qh + 1) * LANES] * inv[qh] for qh in range(n_ch)])
        live["o", p] = res

    def store(p):
        for u, ots in enumerate(live.pop(("o", p))):
            for qh, ot in enumerate(ots):
                r0 = u * blk + qh * CHUNK
                o2 = ot.T
                write(slice(r0, r0 + CHUNK), slice(p * LANES, (p + 1) * LANES),
                      jnp.where(low, o2[:CHUNK], o2[CHUNK:]).astype(BF16))

    return [(functools.partial(scores, p), functools.partial(softmax_values, p),
             functools.partial(store, p)) for p in range(n_pairs)]


def _gla_stages(q_ref, k_ref, gk_ref, vt_ref, g_ref, st_ref, seq_start, write):
    pair = 2 * CHUNK
    heads = range(GLA_HEADS)
    ks = [slice(h * GLA_DK, (h + 1) * GLA_DK) for h in heads]
    vs = [slice(h * GLA_DV, (h + 1) * GLA_DV) for h in heads]
    r = lax.broadcasted_iota(jnp.int32, (pair, pair), 0)
    c = lax.broadcasted_iota(jnp.int32, (pair, pair), 1)
    tri = ((c <= r) & ((r // CHUNK) == (c // CHUNK))).astype(BF16)
    first = lax.broadcasted_iota(jnp.int32, (pair, GLA_DK), 0) < CHUNK
    eps_scaled = EPS * GLA_DK
    n_pairs = q_ref.shape[0] // pair
    sts = [jnp.where(seq_start, jnp.zeros((GLA_DV, GLA_DK), F32), st_ref[h]) for h in heads]
    live = {}

    def decay_and_kv(i):
        sl = slice(i * pair, (i + 1) * pair)
        b_all = _dot(tri, gk_ref[sl, :])
        res = []
        for h in heads:
            b = b_all[:, ks[h]]
            b_end0 = b[CHUNK - 1:CHUNK, :]
            b_end1 = b[pair - 1:pair, :]
            k_dec = (k_ref[sl, ks[h]].astype(F32)
                     * jnp.exp(jnp.where(first, b_end0, b_end1) - b)).astype(BF16)
            zero = jnp.zeros_like(k_dec)
            kd = jnp.concatenate([jnp.where(first, k_dec, zero), jnp.where(first, zero, k_dec)],
                                 axis=1)
            kv = _dot(vt_ref[vs[h], sl], kd)
            res.append((jnp.exp(b_end0), jnp.exp(b_end1), kv))
        live[i] = res

    def state_and_out(i):
        sl = slice(i * pair, (i + 1) * pair)
        dk = live.pop(i)
        outs = []
        for h in heads:
            a0, a1, kv = dk[h]
            st0 = sts[h] * a0 + kv[:, :GLA_DK]
            st1 = st0 * a1 + kv[:, GLA_DK:]
            sts[h] = st1
            q = q_ref[sl, ks[h]]
            outs.append(jnp.concatenate([_dot_nt(q[:CHUNK], st0.astype(BF16)),
                                         _dot_nt(q[CHUNK:], st1.astype(BF16))], axis=0))
        for h in heads:
            o = outs[h]
            inv = lax.rsqrt(jnp.mean(o * o, axis=-1, keepdims=True) + eps_scaled)
            write(sl, vs[h], (o * inv).astype(BF16) * g_ref[sl, vs[h]])

    def save_state():
        for h in heads:
            st_ref[h] = sts[h]

    stages = [functools.partial(decay_and_kv, 0)]
    for i in range(n_pairs):
        if i + 1 < n_pairs:
            stages.append(functools.partial(decay_and_kv, i + 1))
        stages.append(functools.partial(state_and_out, i))
    stages.append(save_state)
    return stages


def _mix_stages(x_ref, gate_ref, wa_ref, wg_ref, wm_ref, oa_ref, og_ref, mg_ref, o_ref):
    pieces = [slice(c * MIX_CHUNK, (c + 1) * MIX_CHUNK) for c in range(D_MODEL // MIX_CHUNK)]

    def merge(cs):
        ya = _dot(oa_ref[...], wa_ref[:, cs])
        yg = _dot(og_ref[...], wg_ref[:, cs])
        cs_b = slice(D_MODEL + cs.start, D_MODEL + cs.stop)
        mg_ref[:, cs] = (gate_ref[:, cs].astype(F32) * ya
                         + gate_ref[:, cs_b].astype(F32) * yg).astype(BF16)

    def project(cs):
        o_ref[:, cs] = x_ref[:, cs] + _dot(mg_ref[...], wm_ref[:, cs])

    return ([functools.partial(merge, cs) for cs in pieces]
            + [functools.partial(project, cs) for cs in pieces])


def _interleave(major, minor):
    out, done = [], 0
    for n, stage in enumerate(major):
        out.append(stage)
        want = (n + 1) * len(minor) // len(major)
        out.extend(minor[done:want])
        done = want
    return out


def _run_lookahead(i, n_tiles, lead, lag, merge, init=None):
    @pl.when(i == 0)
    def _():
        if init is not None:
            init()
        for stage in merge(lead(), []):
            stage()

    @pl.when(jnp.logical_and(i > 0, i < n_tiles))
    def _():
        for stage in merge(lead(), lag()):
            stage()

    @pl.when(i == n_tiles)
    def _():
        for stage in merge([], lag()):
            stage()


def _attn_mix_kernel(tiles_per_seq, n_cast,
                     sink_ref, q_ref, k_ref, kp_ref, vt_ref, vtp_ref,
                     qg_ref, kg_ref, gk_ref, vgt_ref, gg_ref,
                     x_ref, gate_ref, wa_ref, wg_ref, wm_ref, *rest):
    cast_src, (o_ref, *cast_dst) = rest[:n_cast], rest[n_cast:2 * n_cast + 1]
    oa_scr, og_scr, mg_scr, st_ref = rest[2 * n_cast + 1:]
    i = pl.program_id(0)
    n_tiles = pl.num_programs(0) - 1
    seq_start = (i % tiles_per_seq) == 0
    slot_w = i % 2
    slot_r = 1 - slot_w

    def write_oa(rows, cols, val):
        oa_scr[slot_w, rows, cols] = val

    def write_og(rows, cols, val):
        og_scr[slot_w, rows, cols] = val

    def mixers():
        swa = _swa_stages(sink_ref, q_ref, k_ref, kp_ref, vt_ref, vtp_ref, seq_start, write_oa)
        gla = _gla_stages(qg_ref, kg_ref, gk_ref, vgt_ref, gg_ref, st_ref, seq_start, write_og)
        n = len(swa)
        order = [swa[0][0]]
        for p in range(n):
            if p + 1 < n:
                order.append(swa[p + 1][0])
            order.append(swa[p][1])
            if p > 0:
                order.append(swa[p - 1][2])
        order.append(swa[n - 1][2])
        return _interleave(order, gla) + [convert_weights]

    def projections():
        return _mix_stages(x_ref, gate_ref, wa_ref, wg_ref, wm_ref,
                           oa_scr.at[slot_r], og_scr.at[slot_r], mg_scr, o_ref)

    def merge(lead, lag):
        return _interleave(lead, lag) if lead else lag

    def convert_weights():
        for src, dst in zip(cast_src, cast_dst):
            dst[...] = src[...].astype(BF16)

    def init():
        st_ref[...] = jnp.zeros_like(st_ref)

    _run_lookahead(i, n_tiles, mixers, projections, merge, init)


def _attn_mix(sinks, qa, ka, vat, qg, kg, gk, vgt, gg, x2d, gates, wa, wg, wm, cast_w, seq):
    t = x2d.shape[0]
    tm = TM_ATTN
    n_tiles = t // tm
    blocks = tm // SWA_BLOCK
    kw = ka.shape[1]
    cur = lambda i: jnp.minimum(i, n_tiles - 1)
    prv = lambda i: jnp.maximum(i - 1, 0)
    prev_block = lambda i: jnp.maximum(cur(i) * blocks - 1, 0)
    tok = lambda n: pl.BlockSpec((tm, n), lambda i: (cur(i), 0))
    col = lambda n: pl.BlockSpec((n, tm), lambda i: (0, cur(i)))
    old = lambda n: pl.BlockSpec((tm, n), lambda i: (prv(i), 0))

    def slab(w):
        n_slabs = max(n for n in range(1, n_tiles + 1) if w.shape[0] % (16 * n) == 0)
        return pl.BlockSpec((w.shape[0] // n_slabs, w.shape[1]),
                            lambda i: (jnp.minimum(i, n_slabs - 1), 0))

    cast_specs = [slab(w) for w in cast_w]
    return pl.pallas_call(
        functools.partial(_attn_mix_kernel, seq // tm, len(cast_w)),
        grid=(n_tiles + 1,),
        in_specs=[
            pl.BlockSpec(memory_space=pltpu.SMEM),
            tok(qa.shape[1]), tok(kw),
            pl.BlockSpec((SWA_BLOCK, kw), lambda i: (prev_block(i), 0)),
            col(kw),
            pl.BlockSpec((kw, SWA_BLOCK), lambda i: (0, prev_block(i))),
            tok(GLA_KEY_DIM), tok(GLA_KEY_DIM), tok(GLA_KEY_DIM), col(GLA_VAL_DIM),
            tok(GLA_VAL_DIM),
            old(D_MODEL), old(2 * D_MODEL),
            _const_spec(wa.shape), _const_spec(wg.shape), _const_spec(wm.shape),
        ] + cast_specs,
        out_specs=[old(D_MODEL)] + cast_specs,
        out_shape=[jax.ShapeDtypeStruct((t, D_MODEL), F32)]
        + [jax.ShapeDtypeStruct(w.shape, BF16) for w in cast_w],
        scratch_shapes=[
            pltpu.VMEM((2, tm, SWA_HEADS * SWA_HEAD_DIM), BF16),
            pltpu.VMEM((2, tm, GLA_VAL_DIM), BF16),
            pltpu.VMEM((tm, D_MODEL), BF16),
            pltpu.VMEM((GLA_HEADS, GLA_DV, GLA_DK), F32),
        ],
        compiler_params=_params(("arbitrary",)),
        name="attn_mix",
    )(sinks, qa, ka, ka, vat, vat, qg, kg, gk, vgt, gg, x2d, gates, wa, wg, wm, *cast_w)


def _mem_kv_kernel(m_ref, nw_ref, w_ref, k_ref, v_ref):
    mn = _rms(m_ref[0], nw_ref[...]).astype(BF16)
    kv = _dot(mn, w_ref[...])
    k = kv[:, :MEM_WIDTH].astype(BF16)
    v = kv[:, MEM_WIDTH:].astype(BF16)
    head = lax.broadcasted_iota(jnp.int32, (N_MEM, MEM_WIDTH), 1) // MEM_HEAD_DIM
    zero = jnp.zeros_like(k)
    for h in range(MEM_HEADS):
        k_ref[0, h] = jnp.where(head == h, k, zero)
        v_ref[0, h] = jnp.where(head == h, v, zero)


def _mem_kv(mem, nw, w):
    batch = mem.shape[0]
    out = jax.ShapeDtypeStruct((batch, MEM_HEADS, N_MEM, MEM_WIDTH), BF16)
    spec = pl.BlockSpec((1, MEM_HEADS, N_MEM, MEM_WIDTH), lambda b: (b, 0, 0, 0))
    return pl.pallas_call(
        _mem_kv_kernel,
        grid=(batch,),
        in_specs=[pl.BlockSpec((1, N_MEM, D_MODEL), lambda b: (b, 0, 0)),
                  _const_spec(nw.shape), _const_spec(w.shape)],
        out_specs=(spec, spec),
        out_shape=(out, out),
        compiler_params=_params(("arbitrary",)),
        name="mem_kv",
    )(mem, nw, w)


def _xattn_stages(x_ref, k_ref, v_ref, nq_ref, wq_ref, wo_ref, nf_ref, x2_ref, hf_ref):
    live = {}

    def queries():
        hq = _rms(x_ref[...], nq_ref[...]).astype(BF16)
        live["q"] = (_dot(hq, wq_ref[...]) * (MEM_HEAD_DIM ** -0.5)).astype(BF16)
        live["o"] = jnp.zeros((x_ref.shape[0], MEM_WIDTH), F32)

    def head(h):
        s = _dot_nt(live["q"], k_ref[0, h])
        e = jnp.exp(s - jnp.max(s, axis=-1, keepdims=True))
        inv = 1.0 / jnp.sum(e, axis=-1, keepdims=True)
        live["o"] = live["o"] + _dot(e.astype(BF16), v_ref[0, h]) * inv

    def project():
        x2 = x_ref[...] + _dot(live.pop("o").astype(BF16), wo_ref[...])
        x2_ref[...] = x2
        hf_ref[...] = _rms(x2, nf_ref[...]).astype(BF16)

    return [queries] + [functools.partial(head, h) for h in range(MEM_HEADS)] + [project]


def _ffn_stages(x2_ref, hf_ref, wg_ref, wu_ref, wd_ref, nfin_ref, act_ref, o_ref):
    live = {"ss": jnp.zeros((o_ref.shape[0], 1), F32)}

    def hidden(cs):
        hf = hf_ref[...]
        g = _dot(hf, wg_ref[:, cs])
        u = _dot(hf, wu_ref[:, cs])
        act_ref[:, cs] = (g * jax.nn.sigmoid(g) * u).astype(BF16)

    def down(cs):
        y = x2_ref[:, cs] + _dot(act_ref[...], wd_ref[:, cs])
        live["ss"] = live["ss"] + jnp.sum(y * y, axis=-1, keepdims=True)
        o_ref[:, cs] = y

    def final_norm():
        inv = lax.rsqrt(live["ss"] * (1.0 / D_MODEL) + EPS)
        o_ref[...] = o_ref[...] * inv * nfin_ref[...]

    hid = [slice(c * FF_CHUNK, (c + 1) * FF_CHUNK) for c in range(D_FF // FF_CHUNK)]
    out = [slice(c * MIX_CHUNK, (c + 1) * MIX_CHUNK) for c in range(D_MODEL // MIX_CHUNK)]
    return ([functools.partial(hidden, cs) for cs in hid]
            + [functools.partial(down, cs) for cs in out] + [final_norm])


def _tail_kernel(x_ref, k_ref, v_ref, nq_ref, wq_ref, wo_ref, nf_ref, wg_ref, wu_ref, wd_ref,
                 nfin_ref, o_ref, x2_scr, hf_scr, act_ref):
    i = pl.program_id(0)
    n_tiles = pl.num_programs(0) - 1
    slot_w = i % 2
    slot_r = 1 - slot_w

    def xattn():
        return _xattn_stages(x_ref, k_ref, v_ref, nq_ref, wq_ref, wo_ref, nf_ref,
                             x2_scr.at[slot_w], hf_scr.at[slot_w])

    def ffn():
        return _ffn_stages(x2_scr.at[slot_r], hf_scr.at[slot_r], wg_ref, wu_ref, wd_ref,
                           nfin_ref, act_ref, o_ref)

    def merge(lead, lag):
        if not lag:
            return lead
        n_hidden = D_FF // FF_CHUNK
        n_first = n_hidden - len(lead)
        return lag[:n_first] + _interleave(lag[n_first:n_hidden], lead) + lag[n_hidden:]

    _run_lookahead(i, n_tiles, xattn, ffn, merge)


def _tail(x2d, kext, vext, nq, wq, wo, nf, wg, wu, wd, nfin, seq):
    tm = TM_TAIL
    t = x2d.shape[0]
    n_tiles = t // tm
    tiles_per_seq = seq // tm
    cur = lambda i: jnp.minimum(i, n_tiles - 1)
    prv = lambda i: jnp.maximum(i - 1, 0)
    mem_spec = pl.BlockSpec((1, MEM_HEADS, N_MEM, MEM_WIDTH),
                            lambda i: (cur(i) // tiles_per_seq, 0, 0, 0))
    consts = (nq, wq, wo, nf, wg, wu, wd, nfin)
    return pl.pallas_call(
        _tail_kernel,
        grid=(n_tiles + 1,),
        in_specs=[pl.BlockSpec((tm, D_MODEL), lambda i: (cur(i), 0)), mem_spec, mem_spec]
        + [_const_spec(c.shape) for c in consts],
        out_specs=pl.BlockSpec((tm, D_MODEL), lambda i: (prv(i), 0)),
        out_shape=jax.ShapeDtypeStruct((t, D_MODEL), F32),
        scratch_shapes=[pltpu.VMEM((2, tm, D_MODEL), F32), pltpu.VMEM((2, tm, D_MODEL), BF16),
                        pltpu.VMEM((tm, D_FF), BF16)],
        compiler_params=_params(("arbitrary",)),
        name="tail",
    )(x2d, kext, vext, *consts)


def _layer(x2d, mem, batch, seq, norm_mix_w, w_in, b_gate, attn_sinks, gla_gate_w2, gla_gate_b,
           gla_norm_w, w_attn_o, w_gla_o, w_mix_o, norm_mem_q_w, norm_mem_kv_w, w_mem_q,
           w_mem_kv, w_mem_o, norm_ffn_w, w_ffn_gate, w_ffn_up, w_ffn_down, out_norm_w):
    w2_p = jnp.pad(gla_gate_w2, ((0, LANES - GLA_GATE_RANK), (0, 0)))
    bf = lambda a: a.astype(BF16)
    r2 = lambda a: a.reshape(1, -1)

    qa, ka, vat, qg, kg, vgt, gg, gk, gates, wa16, wgo16, wmo16 = _in_proj(
        x2d, r2(norm_mix_w), w_in.T, bf(w2_p), r2(gla_gate_b), r2(b_gate),
        r2(jnp.tile(gla_norm_w, GLA_HEADS)), w_attn_o, w_gla_o, w_mix_o)
    x1, wg16, wu16, wd16, wq16, wo16, wkv16 = _attn_mix(
        attn_sinks, qa, ka, vat, qg, kg, gk, vgt, gg, x2d, gates, wa16, wgo16, wmo16,
        (w_ffn_gate, w_ffn_up, w_ffn_down, w_mem_q, w_mem_o, w_mem_kv), seq)
    kext, vext = _mem_kv(mem, r2(norm_mem_kv_w), wkv16)
    return _tail(x1, kext, vext, r2(norm_mem_q_w), wq16, wo16, r2(norm_ffn_w),
                 wg16, wu16, wd16, r2(out_norm_w), seq)


def kernel(x, mem, norm_mix_w, w_in, b_gate, attn_sinks, gla_gate_w2, gla_gate_b, gla_norm_w,
           w_attn_o, w_gla_o, w_mix_o, norm_mem_q_w, norm_mem_kv_w, w_mem_q, w_mem_kv, w_mem_o,
           norm_ffn_w, w_ffn_gate, w_ffn_up, w_ffn_down, norm_final_w):
    batch, seq, d = x.shape
    depth = w_in.shape[0]
    assert depth == 1 and d == D_MODEL
    assert seq % TM_PROJ == 0 and seq % TM_ATTN == 0 and seq % TM_TAIL == 0
    out = _layer(x.reshape(batch * seq, d), mem, batch, seq, norm_mix_w[0], w_in[0], b_gate[0],
                 attn_sinks[0], gla_gate_w2[0], gla_gate_b[0], gla_norm_w[0], w_attn_o[0],
                 w_gla_o[0], w_mix_o[0], norm_mem_q_w[0], norm_mem_kv_w[0], w_mem_q[0],
                 w_mem_kv[0], w_mem_o[0], norm_ffn_w[0], w_ffn_gate[0], w_ffn_up[0],
                 w_ffn_down[0], norm_final_w)
    return out.reshape(batch, seq, d)
```

```python
import functools

import jax
import jax.numpy as jnp
from jax import lax
from jax.experimental import pallas as pl
from jax.experimental.pallas import tpu as pltpu

D_MODEL = 1024
CHUNK = 64
N_MEM = 256
EPS = 1e-6

SWA_HEADS = 16
SWA_KV_HEADS = 2
SWA_HEAD_DIM = 64
SWA_BLOCK = 128

GLA_HEADS = 4
GLA_KEY_DIM = D_MODEL // 2
GLA_VAL_DIM = D_MODEL
GLA_DK = GLA_KEY_DIM // GLA_HEADS
GLA_DV = GLA_VAL_DIM // GLA_HEADS
GLA_GATE_RANK = 16
GLA_GATE_NORM = 16.0

MEM_HEADS = 4
MEM_HEAD_DIM = 64
MEM_WIDTH = MEM_HEADS * MEM_HEAD_DIM

D_FF = -(-(8 * D_MODEL) // (3 * 256)) * 256

IN_SIZES = (SWA_HEADS * SWA_HEAD_DIM, SWA_KV_HEADS * SWA_HEAD_DIM, SWA_KV_HEADS * SWA_HEAD_DIM,
            GLA_KEY_DIM, GLA_KEY_DIM, GLA_VAL_DIM, GLA_VAL_DIM, GLA_GATE_RANK, 2 * D_MODEL)
IN_OFFSETS = tuple(sum(IN_SIZES[:i]) for i in range(len(IN_SIZES) + 1))

LANES = 128
VMEM_LIMIT = 62 * 1024 * 1024

LOG2E = 1.4426950408889634

TM_PROJ = 1024
TM_ATTN = 512
TM_TAIL = 1024
MIX_CHUNK = 256
FF_CHUNK = 256

BF16 = jnp.bfloat16
F32 = jnp.float32


def _rms(x, w):
    return x * lax.rsqrt(jnp.mean(x * x, axis=-1, keepdims=True) + EPS) * w


def _dot(a, b):
    return jnp.dot(a, b, preferred_element_type=F32)


def _dot_nt(a, b):
    return lax.dot_general(a, b, (((1,), (1,)), ((), ())), preferred_element_type=F32)


def _const_spec(shape):
    zeros = (0,) * len(shape)
    return pl.BlockSpec(shape, lambda *_: zeros, pipeline_mode=pl.Buffered(1))


def _params(semantics):
    return pltpu.CompilerParams(dimension_semantics=semantics, vmem_limit_bytes=VMEM_LIMIT)


def _in_proj_kernel(n_cast, x_ref, nw_ref, wt_ref, w2_ref, gb_ref, bg_ref, gn_ref, *rest):
    cast_src, cast_dst = rest[:n_cast], rest[len(rest) - n_cast:]
    (qa_ref, ka_ref, vat_ref, qg_ref, kg_ref, vgt_ref, gg_ref, gk_ref,
     gate_ref) = rest[n_cast:len(rest) - n_cast]
    for src, dst in zip(cast_src, cast_dst):
        dst[...] = src[...].astype(BF16)
    o_qa, o_ka, o_va, o_qg, o_kg, o_vg, o_gg, o_alr, o_gate, o_end = IN_OFFSETS
    hd = SWA_HEAD_DIM
    h = _rms(x_ref[...], nw_ref[...]).astype(BF16)
    half = (o_end - o_gate) // 2

    def branch_gates(lo):
        pre = _dot_nt(h, wt_ref[o_gate + lo:o_gate + lo + half, :]) + bg_ref[:, lo:lo + half]
        gate_ref[:, lo:lo + half] = jax.nn.sigmoid(pre).astype(BF16)

    branch_gates(0)
    group = SWA_HEADS // SWA_KV_HEADS
    wq = jnp.concatenate(
        [wt_ref[o_qa + (p + j * group) * hd:o_qa + (p + j * group + 1) * hd, :]
         for p in range(group) for j in range(SWA_KV_HEADS)], axis=0)
    qa_ref[...] = (_dot_nt(h, wq) * (hd ** -0.5 * LOG2E)).astype(BF16)
    branch_gates(half)
    vgt_ref[...] = _dot_nt(wt_ref[o_vg:o_gg, :], h).astype(BF16)
    g = _dot_nt(h, wt_ref[o_gg:o_alr, :])
    gg_ref[...] = (g * jax.nn.sigmoid(g) * gn_ref[...]).astype(BF16)
    kw = SWA_KV_HEADS * hd
    k_alr = _dot_nt(h, jnp.concatenate([wt_ref[o_ka:o_va, :], wt_ref[o_alr:o_alr + LANES, :]],
                                       axis=0))
    ka_ref[...] = k_alr[:, :kw].astype(BF16)
    qg_ref[...] = _dot_nt(h, wt_ref[o_qg:o_kg, :]).astype(BF16)
    z = _dot(k_alr[:, kw:].astype(BF16), w2_ref[...]) + gb_ref[...]
    log_sig = jnp.minimum(z, 0.0) - jnp.log(1.0 + jnp.exp(-jnp.abs(z)))
    gk_ref[...] = (log_sig * (1.0 / GLA_GATE_NORM)).astype(BF16)
    kg_ref[...] = _dot_nt(h, wt_ref[o_kg:o_vg, :]).astype(BF16)
    vat_ref[...] = _dot_nt(wt_ref[o_va:o_qg, :], h).astype(BF16)


def _in_proj(x2d, nw, wt, w2, gb, bg, gn, w_attn_o, w_gla_o, w_mix_o):
    t = x2d.shape[0]
    tm = TM_PROJ
    steps = t // tm
    row = lambda n: pl.BlockSpec((tm, n), lambda i: (i, 0))
    col = lambda n: pl.BlockSpec((n, tm), lambda i: (0, i))
    consts = (nw, wt, w2, gb, bg, gn)
    kw = SWA_KV_HEADS * SWA_HEAD_DIM
    tok = lambda n: jax.ShapeDtypeStruct((t, n), BF16)
    out_shape = [
        tok(SWA_HEADS * SWA_HEAD_DIM), tok(kw), jax.ShapeDtypeStruct((kw, t), BF16),
        tok(GLA_KEY_DIM), tok(GLA_KEY_DIM), jax.ShapeDtypeStruct((GLA_VAL_DIM, t), BF16),
        tok(GLA_VAL_DIM), tok(GLA_KEY_DIM), tok(2 * D_MODEL),
    ]
    out_specs = [
        row(SWA_HEADS * SWA_HEAD_DIM), row(kw), col(kw),
        row(GLA_KEY_DIM), row(GLA_KEY_DIM), col(GLA_VAL_DIM),
        row(GLA_VAL_DIM), row(GLA_KEY_DIM), row(2 * D_MODEL),
    ]
    cast_w = (w_attn_o, w_gla_o, w_mix_o)
    slab_rows = D_MODEL // steps
    group = SWA_HEADS // SWA_KV_HEADS
    assert slab_rows == SWA_HEAD_DIM and steps == SWA_HEADS and slab_rows % 16 == 0
    slab = pl.BlockSpec((slab_rows, D_MODEL), lambda i: (i, 0))
    head_slab = pl.BlockSpec((slab_rows, D_MODEL),
                             lambda i: (i // SWA_KV_HEADS + group * (i % SWA_KV_HEADS), 0))
    return pl.pallas_call(
        functools.partial(_in_proj_kernel, len(cast_w)),
        grid=(steps,),
        in_specs=[row(D_MODEL)] + [_const_spec(c.shape) for c in consts]
        + [head_slab, slab, slab],
        out_specs=out_specs + [slab] * len(cast_w),
        out_shape=out_shape + [jax.ShapeDtypeStruct(w.shape, BF16) for w in cast_w],
        compiler_params=_params(("arbitrary",)),
        name="in_proj",
    )(x2d, *consts, *cast_w)


def _swa_stages(sink_ref, q_ref, k_ref, kp_ref, vt_ref, vtp_ref, seq_start, write):
    blk = SWA_BLOCK
    win = blk + CHUNK
    low = lax.broadcasted_iota(jnp.int32, (CHUNK, LANES), 1) < SWA_HEAD_DIM
    low_row = lax.broadcasted_iota(jnp.int32, (1, LANES), 1) < SWA_HEAD_DIM
    prev_bias = jnp.where(seq_start, -1e30, 0.0).astype(F32)
    k_all = jnp.concatenate([kp_ref[...], k_ref[...]], axis=0)
    vt_all = jnp.concatenate([vtp_ref[...], vt_ref[...]], axis=1)
    zeros_chunk = jnp.zeros((CHUNK, LANES), BF16)
    n_pairs = SWA_HEADS // SWA_KV_HEADS
    n_blocks = q_ref.shape[0] // blk
    n_ch = blk // CHUNK
    live = {}

    def scores(p):
        res = []
        for u in range(n_blocks):
            qsel = []
            for qh in range(n_ch):
                r0 = u * blk + qh * CHUNK
                qp = q_ref[r0:r0 + CHUNK, p * LANES:(p + 1) * LANES]
                zero = jnp.zeros_like(qp)
                qsel += [jnp.where(low, qp, zero), jnp.where(low, zero, qp)]
            res.append(_dot_nt(k_all[u * blk:(u + 2) * blk], jnp.concatenate(qsel, axis=0)))
        live["s", p] = res

    def softmax_values(p):
        sink_row = jnp.where(low_row, sink_ref[p], sink_ref[p + n_pairs]) * LOG2E
        res = []
        for u, s_blk in enumerate(live.pop(("s", p))):
            probs, inv = [], []
            for qh in range(n_ch):
                st = s_blk[qh * CHUNK:qh * CHUNK + win, qh * LANES:(qh + 1) * LANES]
                if u == 0:
                    n_prev = blk - qh * CHUNK
                    st = jnp.concatenate([st[:n_prev] + prev_bias, st[n_prev:]], axis=0)
                m = jnp.maximum(jnp.max(st, axis=0, keepdims=True), sink_row)
                e = jnp.exp2(st - m)
                inv.append(1.0 / (jnp.sum(e, axis=0, keepdims=True) + jnp.exp2(sink_row - m)))
                probs.append(jnp.concatenate([zeros_chunk] * qh + [e.astype(BF16)]
                                             + [zeros_chunk] * (n_ch - 1 - qh), axis=0))
            ot = _dot(vt_all[:, u * blk:(u + 2) * blk], jnp.concatenate(probs, axis=1))
            res.append([ot[:, qh * LANES:(qh + 1) * LANES] * inv[qh] for qh in range(n_ch)])
        live["o", p] = res

    def store(p):
        for u, ots in enumerate(live.pop(("o", p))):
            for qh, ot in enumerate(ots):
                r0 = u * blk + qh * CHUNK
                o2 = ot.T
                write(slice(r0, r0 + CHUNK), slice(p * LANES, (p + 1) * LANES),
                      jnp.where(low, o2[:CHUNK], o2[CHUNK:]).astype(BF16))

    return [(functools.partial(scores, p), functools.partial(softmax_values, p),
             functools.partial(store, p)) for p in range(n_pairs)]


def _gla_stages(q_ref, k_ref, gk_ref, vt_ref, g_ref, st_ref, seq_start, write):
    pair = 2 * CHUNK
    heads = range(GLA_HEADS)
    ks = [slice(h * GLA_DK, (h + 1) * GLA_DK) for h in heads]
    vs = [slice(h * GLA_DV, (h + 1) * GLA_DV) for h in heads]
    r = lax.broadcasted_iota(jnp.int32, (pair, pair), 0)
    c = lax.broadcasted_iota(jnp.int32, (pair, pair), 1)
    tri = ((c <= r) & ((r // CHUNK) == (c // CHUNK))).astype(BF16)
    first = lax.broadcasted_iota(jnp.int32, (pair, GLA_DK), 0) < CHUNK
    eps_scaled = EPS * GLA_DK
    n_pairs = q_ref.shape[0] // pair
    sts = [jnp.where(seq_start, jnp.zeros((GLA_DV, GLA_DK), F32), st_ref[h]) for h in heads]
    live = {}

    def decay_and_kv(i):
        sl = slice(i * pair, (i + 1) * pair)
        b_all = _dot(tri, gk_ref[sl, :])
        res = []
        for h in heads:
            b = b_all[:, ks[h]]
            b_end0 = b[CHUNK - 1:CHUNK, :]
            b_end1 = b[pair - 1:pair, :]
            k_dec = (k_ref[sl, ks[h]].astype(F32)
                     * jnp.exp(jnp.where(first, b_end0, b_end1) - b)).astype(BF16)
            zero = jnp.zeros_like(k_dec)
            kd = jnp.concatenate([jnp.where(first, k_dec, zero), jnp.where(first, zero, k_dec)],
                                 axis=1)
            kv = _dot(vt_ref[vs[h], sl], kd)
            res.append((jnp.exp(b_end0), jnp.exp(b_end1), kv))
        live[i] = res

    def state_and_out(i):
        sl = slice(i * pair, (i + 1) * pair)
        dk = live.pop(i)
        outs = []
        for h in heads:
            a0, a1, kv = dk[h]
            st0 = sts[h] * a0 + kv[:, :GLA_DK]
            st1 = st0 * a1 + kv[:, GLA_DK:]
            sts[h] = st1
            q = q_ref[sl, ks[h]]
            outs.append(jnp.concatenate([_dot_nt(q[:CHUNK], st0.astype(BF16)),
                                         _dot_nt(q[CHUNK:], st1.astype(BF16))], axis=0))
        for h in heads:
            o = outs[h]
            inv = lax.rsqrt(jnp.mean(o * o, axis=-1, keepdims=True) + eps_scaled)
            write(sl, vs[h], (o * inv).astype(BF16) * g_ref[sl, vs[h]])

    def save_state():
        for h in heads:
            st_ref[h] = sts[h]

    stages = [functools.partial(decay_and_kv, 0)]
    for i in range(n_pairs):
        if i + 1 < n_pairs:
            stages.append(functools.partial(decay_and_kv, i + 1))
        stages.append(functools.partial(state_and_out, i))
    stages.append(save_state)
    return stages


def _mix_stages(x_ref, gate_ref, wa_ref, wg_ref, wm_ref, oa_ref, og_ref, mg_ref, o_ref):
    pieces = [slice(c * MIX_CHUNK, (c + 1) * MIX_CHUNK) for c in range(D_MODEL // MIX_CHUNK)]

    def merge(cs):
        ya = _dot(oa_ref[...], wa_ref[:, cs])
        yg = _dot(og_ref[...], wg_ref[:, cs])
        cs_b = slice(D_MODEL + cs.start, D_MODEL + cs.stop)
        mg_ref[:, cs] = (gate_ref[:, cs].astype(F32) * ya
                         + gate_ref[:, cs_b].astype(F32) * yg).astype(BF16)

    def project(cs):
        o_ref[:, cs] = x_ref[:, cs] + _dot(mg_ref[...], wm_ref[:, cs])

    return ([functools.partial(merge, cs) for cs in pieces]
            + [functools.partial(project, cs) for cs in pieces])


def _interleave(major, minor):
    out, done = [], 0
    for n, stage in enumerate(major):
        out.append(stage)
        want = (n + 1) * len(minor) // len(major)
        out.extend(minor[done:want])
        done = want
    return out


def _run_lookahead(i, n_tiles, lead, lag, merge, init=None):
    @pl.when(i == 0)
    def _():
        if init is not None:
            init()
        for stage in merge(lead(), []):
            stage()

    @pl.when(jnp.logical_and(i > 0, i < n_tiles))
    def _():
        for stage in merge(lead(), lag()):
            stage()

    @pl.when(i == n_tiles)
    def _():
        for stage in merge([], lag()):
            stage()


def _attn_mix_kernel(tiles_per_seq, n_cast,
                     sink_ref, q_ref, k_ref, kp_ref, vt_ref, vtp_ref,
                     qg_ref, kg_ref, gk_ref, vgt_ref, gg_ref,
                     x_ref, gate_ref, wa_ref, wg_ref, wm_ref, *rest):
    cast_src, (o_ref, *cast_dst) = rest[:n_cast], rest[n_cast:2 * n_cast + 1]
    oa_scr, og_scr, mg_scr, st_ref = rest[2 * n_cast + 1:]
    i = pl.program_id(0)
    n_tiles = pl.num_programs(0) - 1
    seq_start = (i % tiles_per_seq) == 0
    slot_w = i % 2
    slot_r = 1 - slot_w

    def write_oa(rows, cols, val):
        oa_scr[slot_w, rows, cols] = val

    def write_og(rows, cols, val):
        og_scr[slot_w, rows, cols] = val

    def mixers():
        swa = _swa_stages(sink_ref, q_ref, k_ref, kp_ref, vt_ref, vtp_ref, seq_start, write_oa)
        gla = _gla_stages(qg_ref, kg_ref, gk_ref, vgt_ref, gg_ref, st_ref, seq_start, write_og)
        n = len(swa)
        order = [swa[0][0]]
        for p in range(n):
            if p + 1 < n:
                order.append(swa[p + 1][0])
            order.append(swa[p][1])
            if p > 0:
                order.append(swa[p - 1][2])
        order.append(swa[n - 1][2])
        return _interleave(order, gla) + [convert_weights]

    def projections():
        return _mix_stages(x_ref, gate_ref, wa_ref, wg_ref, wm_ref,
                           oa_scr.at[slot_r], og_scr.at[slot_r], mg_scr, o_ref)

    def merge(lead, lag):
        return _interleave(lead, lag) if lead else lag

    def convert_weights():
        for src, dst in zip(cast_src, cast_dst):
            dst[...] = src[...].astype(BF16)

    def init():
        st_ref[...] = jnp.zeros_like(st_ref)

    _run_lookahead(i, n_tiles, mixers, projections, merge, init)


def _attn_mix(sinks, qa, ka, vat, qg, kg, gk, vgt, gg, x2d, gates, wa, wg, wm, cast_w, seq):
    t = x2d.shape[0]
    tm = TM_ATTN
    n_tiles = t // tm
    blocks = tm // SWA_BLOCK
    kw = ka.shape[1]
    cur = lambda i: jnp.minimum(i, n_tiles - 1)
    prv = lambda i: jnp.maximum(i - 1, 0)
    prev_block = lambda i: jnp.maximum(cur(i) * blocks - 1, 0)
    tok = lambda n: pl.BlockSpec((tm, n), lambda i: (cur(i), 0))
    col = lambda n: pl.BlockSpec((n, tm), lambda i: (0, cur(i)))
    old = lambda n: pl.BlockSpec((tm, n), lambda i: (prv(i), 0))

    def slab(w):
        n_slabs = max(n for n in range(1, n_tiles + 1) if w.shape[0] % (16 * n) == 0)
        return pl.BlockSpec((w.shape[0] // n_slabs, w.shape[1]),
                            lambda i: (jnp.minimum(i, n_slabs - 1), 0))

    cast_specs = [slab(w) for w in cast_w]
    return pl.pallas_call(
        functools.partial(_attn_mix_kernel, seq // tm, len(cast_w)),
        grid=(n_tiles + 1,),
        in_specs=[
            pl.BlockSpec(memory_space=pltpu.SMEM),
            tok(qa.shape[1]), tok(kw),
            pl.BlockSpec((SWA_BLOCK, kw), lambda i: (prev_block(i), 0)),
            col(kw),
            pl.BlockSpec((kw, SWA_BLOCK), lambda i: (0, prev_block(i))),
            tok(GLA_KEY_DIM), tok(GLA_KEY_DIM), tok(GLA_KEY_DIM), col(GLA_VAL_DIM),
            tok(GLA_VAL_DIM),
            old(D_MODEL), old(2 * D_MODEL),
            _const_spec(wa.shape), _const_spec(wg.shape), _const_spec(wm.shape),
        ] + cast_specs,
        out_specs=[old(D_MODEL)] + cast_specs,
        out_shape=[jax.ShapeDtypeStruct((t, D_MODEL), F32)]
        + [jax.ShapeDtypeStruct(w.shape, BF16) for w in cast_w],
        scratch_shapes=[
            pltpu.VMEM((2, tm, SWA_HEADS * SWA_HEAD_DIM), BF16),
            pltpu.VMEM((2, tm, GLA_VAL_DIM), BF16),
            pltpu.VMEM((tm, D_MODEL), BF16),
            pltpu.VMEM((GLA_HEADS, GLA_DV, GLA_DK), F32),
        ],
        compiler_params=_params(("arbitrary",)),
        name="attn_mix",
    )(sinks, qa, ka, ka, vat, vat, qg, kg, gk, vgt, gg, x2d, gates, wa, wg, wm, *cast_w)


def _mem_kv_kernel(m_ref, nw_ref, w_ref, k_ref, v_ref):
    mn = _rms(m_ref[0], nw_ref[...]).astype(BF16)
    kv = _dot(mn, w_ref[...])
    k = kv[:, :MEM_WIDTH].astype(BF16)
    v = kv[:, MEM_WIDTH:].astype(BF16)
    head = lax.broadcasted_iota(jnp.int32, (N_MEM, MEM_WIDTH), 1) // MEM_HEAD_DIM
    zero = jnp.zeros_like(k)
    for h in range(MEM_HEADS):
        k_ref[0, h] = jnp.where(head == h, k, zero)
        v_ref[0, h] = jnp.where(head == h, v, zero)


def _mem_kv(mem, nw, w):
    batch = mem.shape[0]
    out = jax.ShapeDtypeStruct((batch, MEM_HEADS, N_MEM, MEM_WIDTH), BF16)
    spec = pl.BlockSpec((1, MEM_HEADS, N_MEM, MEM_WIDTH), lambda b: (b, 0, 0, 0))
    return pl.pallas_call(
        _mem_kv_kernel,
        grid=(batch,),
        in_specs=[pl.BlockSpec((1, N_MEM, D_MODEL), lambda b: (b, 0, 0)),
                  _const_spec(nw.shape), _const_spec(w.shape)],
        out_specs=(spec, spec),
        out_shape=(out, out),
        compiler_params=_params(("arbitrary",)),
        name="mem_kv",
    )(mem, nw, w)


def _xattn_stages(x_ref, k_ref, v_ref, nq_ref, wq_ref, wo_ref, nf_ref, x2_ref, hf_ref):
    live = {}

    def queries():
        hq = _rms(x_ref[...], nq_ref[...]).astype(BF16)
        live["q"] = (_dot(hq, wq_ref[...]) * (MEM_HEAD_DIM ** -0.5)).astype(BF16)
        live["o"] = jnp.zeros((x_ref.shape[0], MEM_WIDTH), F32)

    def head(h):
        s = _dot_nt(live["q"], k_ref[0, h])
        e = jnp.exp(s - jnp.max(s, axis=-1, keepdims=True))
        inv = 1.0 / jnp.sum(e, axis=-1, keepdims=True)
        live["o"] = live["o"] + _dot(e.astype(BF16), v_ref[0, h]) * inv

    def project():
        x2 = x_ref[...] + _dot(live.pop("o").astype(BF16), wo_ref[...])
        x2_ref[...] = x2
        hf_ref[...] = _rms(x2, nf_ref[...]).astype(BF16)

    return [queries] + [functools.partial(head, h) for h in range(MEM_HEADS)] + [project]


def _ffn_stages(x2_ref, hf_ref, wg_ref, wu_ref, wd_ref, nfin_ref, act_ref, o_ref):
    live = {"ss": jnp.zeros((o_ref.shape[0], 1), F32)}

    def hidden(cs):
        hf = hf_ref[...]
        g = _dot(hf, wg_ref[:, cs])
        u = _dot(hf, wu_ref[:, cs])
        act_ref[:, cs] = (g * jax.nn.sigmoid(g) * u).astype(BF16)

    def down(cs):
        y = x2_ref[:, cs] + _dot(act_ref[...], wd_ref[:, cs])
        live["ss"] = live["ss"] + jnp.sum(y * y, axis=-1, keepdims=True)
        o_ref[:, cs] = y

    def final_norm():
        inv = lax.rsqrt(live["ss"] * (1.0 / D_MODEL) + EPS)
        o_ref[...] = o_ref[...] * inv * nfin_ref[...]

    hid = [slice(c * FF_CHUNK, (c + 1) * FF_CHUNK) for c in range(D_FF // FF_CHUNK)]
    out = [slice(c * MIX_CHUNK, (c + 1) * MIX_CHUNK) for c in range(D_MODEL // MIX_CHUNK)]
    return ([functools.partial(hidden, cs) for cs in hid]
            + [functools.partial(down, cs) for cs in out] + [final_norm])


def _tail_kernel(x_ref, k_ref, v_ref, nq_ref, wq_ref, wo_ref, nf_ref, wg_ref, wu_ref, wd_ref,
                 nfin_ref, o_ref, x2_scr, hf_scr, act_ref):
    i = pl.program_id(0)
    n_tiles = pl.num_programs(0) - 1
    slot_w = i % 2
    slot_r = 1 - slot_w

    def xattn():
        return _xattn_stages(x_ref, k_ref, v_ref, nq_ref, wq_ref, wo_ref, nf_ref,
                             x2_scr.at[slot_w], hf_scr.at[slot_w])

    def ffn():
        return _ffn_stages(x2_scr.at[slot_r], hf_scr.at[slot_r], wg_ref, wu_ref, wd_ref,
                           nfin_ref, act_ref, o_ref)

    def merge(lead, lag):
        if not lag:
            return lead
        n_hidden = D_FF // FF_CHUNK
        n_first = n_hidden - len(lead)
        return lag[:n_first] + _interleave(lag[n_first:n_hidden], lead) + lag[n_hidden:]

    _run_lookahead(i, n_tiles, xattn, ffn, merge)


def _tail(x2d, kext, vext, nq, wq, wo, nf, wg, wu, wd, nfin, seq):
    tm = TM_TAIL
    t = x2d.shape[0]
    n_tiles = t // tm
    tiles_per_seq = seq // tm
    cur = lambda i: jnp.minimum(i, n_tiles - 1)
    prv = lambda i: jnp.maximum(i - 1, 0)
    mem_spec = pl.BlockSpec((1, MEM_HEADS, N_MEM, MEM_WIDTH),
                            lambda i: (cur(i) // tiles_per_seq, 0, 0, 0))
    consts = (nq, wq, wo, nf, wg, wu, wd, nfin)
    return pl.pallas_call(
        _tail_kernel,
        grid=(n_tiles + 1,),
        in_specs=[pl.BlockSpec((tm, D_MODEL), lambda i: (cur(i), 0)), mem_spec, mem_spec]
        + [_const_spec(c.shape) for c in consts],
        out_specs=pl.BlockSpec((tm, D_MODEL), lambda i: (prv(i), 0)),
        out_shape=jax.ShapeDtypeStruct((t, D_MODEL), F32),
        scratch_shapes=[pltpu.VMEM((2, tm, D_MODEL), F32), pltpu.VMEM((2, tm, D_MODEL), BF16),
                        pltpu.VMEM((tm, D_FF), BF16)],
        compiler_params=_params(("arbitrary",)),
        name="tail",
    )(x2d, kext, vext, *consts)


def _layer(x2d, mem, batch, seq, norm_mix_w, w_in, b_gate, attn_sinks, gla_gate_w2, gla_gate_b,
           gla_norm_w, w_attn_o, w_gla_o, w_mix_o, norm_mem_q_w, norm_mem_kv_w, w_mem_q,
           w_mem_kv, w_mem_o, norm_ffn_w, w_ffn_gate, w_ffn_up, w_ffn_down, out_norm_w):
    w2_p = jnp.pad(gla_gate_w2, ((0, LANES - GLA_GATE_RANK), (0, 0)))
    bf = lambda a: a.astype(BF16)
    r2 = lambda a: a.reshape(1, -1)

    qa, ka, vat, qg, kg, vgt, gg, gk, gates, wa16, wgo16, wmo16 = _in_proj(
        x2d, r2(norm_mix_w), bf(w_in.T), bf(w2_p), r2(gla_gate_b), r2(b_gate),
        r2(jnp.tile(gla_norm_w, GLA_HEADS)), w_attn_o, w_gla_o, w_mix_o)
    x1, wg16, wu16, wd16, wq16, wo16, wkv16 = _attn_mix(
        attn_sinks, qa, ka, vat, qg, kg, gk, vgt, gg, x2d, gates, wa16, wgo16, wmo16,
        (w_ffn_gate, w_ffn_up, w_ffn_down, w_mem_q, w_mem_o, w_mem_kv), seq)
    kext, vext = _mem_kv(mem, r2(norm_mem_kv_w), wkv16)
    return _tail(x1, kext, vext, r2(norm_mem_q_w), wq16, wo16, r2(norm_ffn_w),
                 wg16, wu16, wd16, r2(out_norm_w), seq)


def kernel(x, mem, norm_mix_w, w_in, b_gate, attn_sinks, gla_gate_w2, gla_gate_b, gla_norm_w,
           w_attn_o, w_gla_o, w_mix_o, norm_mem_q_w, norm_mem_kv_w, w_mem_q, w_mem_kv, w_mem_o,
           norm_ffn_w, w_ffn_gate, w_ffn_up, w_ffn_down, norm_final_w):
    batch, seq, d = x.shape
    depth = w_in.shape[0]
    assert depth == 1 and d == D_MODEL
    assert seq % TM_PROJ == 0 and seq % TM_ATTN == 0 and seq % TM_TAIL == 0
    out = _layer(x.reshape(batch * seq, d), mem, batch, seq, norm_mix_w[0], w_in[0], b_gate[0],
                 attn_sinks[0], gla_gate_w2[0], gla_gate_b[0], gla_norm_w[0], w_attn_o[0],
                 w_gla_o[0], w_mix_o[0], norm_mem_q_w[0], norm_mem_kv_w[0], w_mem_q[0],
                 w_mem_kv[0], w_mem_o[0], norm_ffn_w[0], w_ffn_gate[0], w_ffn_up[0],
                 w_ffn_down[0], norm_final_w)
    return out.reshape(batch, seq, d)
```

```python
import functools

import jax
import jax.numpy as jnp
from jax import lax
from jax.experimental import pallas as pl
from jax.experimental.pallas import tpu as pltpu

D_MODEL = 1024
CHUNK = 64
N_MEM = 256
EPS = 1e-6

SWA_HEADS = 16
SWA_KV_HEADS = 2
SWA_HEAD_DIM = 64
SWA_BLOCK = 128

GLA_HEADS = 4
GLA_KEY_DIM = D_MODEL // 2
GLA_VAL_DIM = D_MODEL
GLA_DK = GLA_KEY_DIM // GLA_HEADS
GLA_DV = GLA_VAL_DIM // GLA_HEADS
GLA_GATE_RANK = 16
GLA_GATE_NORM = 16.0

MEM_HEADS = 4
MEM_HEAD_DIM = 64
MEM_WIDTH = MEM_HEADS * MEM_HEAD_DIM

D_FF = -(-(8 * D_MODEL) // (3 * 256)) * 256

IN_SIZES = (SWA_HEADS * SWA_HEAD_DIM, SWA_KV_HEADS * SWA_HEAD_DIM, SWA_KV_HEADS * SWA_HEAD_DIM,
            GLA_KEY_DIM, GLA_KEY_DIM, GLA_VAL_DIM, GLA_VAL_DIM, GLA_GATE_RANK, 2 * D_MODEL)
IN_OFFSETS = tuple(sum(IN_SIZES[:i]) for i in range(len(IN_SIZES) + 1))

LANES = 128
VMEM_LIMIT = 56 * 1024 * 1024

LOG2E = 1.4426950408889634

TM_PROJ = 1024
TM_ATTN = 512
TM_TAIL = 512
MIX_CHUNK = 256
FF_CHUNK = 256

BF16 = jnp.bfloat16
F32 = jnp.float32


def _rms(x, w):
    return x * lax.rsqrt(jnp.mean(x * x, axis=-1, keepdims=True) + EPS) * w


def _dot(a, b):
    return jnp.dot(a, b, preferred_element_type=F32)


def _dot_nt(a, b):
    return lax.dot_general(a, b, (((1,), (1,)), ((), ())), preferred_element_type=F32)


def _const_spec(shape):
    zeros = (0,) * len(shape)
    return pl.BlockSpec(shape, lambda *_: zeros, pipeline_mode=pl.Buffered(1))


def _params(semantics):
    return pltpu.CompilerParams(dimension_semantics=semantics, vmem_limit_bytes=VMEM_LIMIT)


def _in_proj_kernel(n_cast, x_ref, nw_ref, wt_ref, w2_ref, gb_ref, bg_ref, gn_ref, *rest):
    cast_src, cast_dst = rest[:n_cast], rest[len(rest) - n_cast:]
    (qa_ref, ka_ref, vat_ref, qg_ref, kg_ref, vgt_ref, gg_ref, gk_ref,
     gate_ref) = rest[n_cast:len(rest) - n_cast]
    for src, dst in zip(cast_src, cast_dst):
        dst[...] = src[...].astype(BF16)
    o_qa, o_ka, o_va, o_qg, o_kg, o_vg, o_gg, o_alr, o_gate, o_end = IN_OFFSETS
    hd = SWA_HEAD_DIM
    h = _rms(x_ref[...], nw_ref[...]).astype(BF16)
    half = (o_end - o_gate) // 2

    def branch_gates(lo):
        pre = _dot_nt(h, wt_ref[o_gate + lo:o_gate + lo + half, :]) + bg_ref[:, lo:lo + half]
        gate_ref[:, lo:lo + half] = jax.nn.sigmoid(pre).astype(BF16)

    branch_gates(0)
    group = SWA_HEADS // SWA_KV_HEADS
    wq = jnp.concatenate(
        [wt_ref[o_qa + (p + j * group) * hd:o_qa + (p + j * group + 1) * hd, :]
         for p in range(group) for j in range(SWA_KV_HEADS)], axis=0)
    qa_ref[...] = (_dot_nt(h, wq) * (hd ** -0.5 * LOG2E)).astype(BF16)
    branch_gates(half)
    vgt_ref[...] = _dot_nt(wt_ref[o_vg:o_gg, :], h).astype(BF16)
    g = _dot_nt(h, wt_ref[o_gg:o_alr, :])
    gg_ref[...] = (g * jax.nn.sigmoid(g) * gn_ref[...]).astype(BF16)
    kw = SWA_KV_HEADS * hd
    k_alr = _dot_nt(h, jnp.concatenate([wt_ref[o_ka:o_va, :], wt_ref[o_alr:o_alr + LANES, :]],
                                       axis=0))
    ka_ref[...] = k_alr[:, :kw].astype(BF16)
    qg_ref[...] = _dot_nt(h, wt_ref[o_qg:o_kg, :]).astype(BF16)
    z = _dot(k_alr[:, kw:].astype(BF16), w2_ref[...]) + gb_ref[...]
    log_sig = jnp.minimum(z, 0.0) - jnp.log(1.0 + jnp.exp(-jnp.abs(z)))
    gk_ref[...] = (log_sig * (1.0 / GLA_GATE_NORM)).astype(BF16)
    kg_ref[...] = _dot_nt(h, wt_ref[o_kg:o_vg, :]).astype(BF16)
    vat_ref[...] = _dot_nt(wt_ref[o_va:o_qg, :], h).astype(BF16)


def _in_proj(x2d, nw, wt, w2, gb, bg, gn, w_attn_o, w_gla_o, w_mix_o):
    t = x2d.shape[0]
    tm = TM_PROJ
    steps = t // tm
    row = lambda n: pl.BlockSpec((tm, n), lambda i: (i, 0))
    col = lambda n: pl.BlockSpec((n, tm), lambda i: (0, i))
    consts = (nw, wt, w2, gb, bg, gn)
    kw = SWA_KV_HEADS * SWA_HEAD_DIM
    tok = lambda n: jax.ShapeDtypeStruct((t, n), BF16)
    out_shape = [
        tok(SWA_HEADS * SWA_HEAD_DIM), tok(kw), jax.ShapeDtypeStruct((kw, t), BF16),
        tok(GLA_KEY_DIM), tok(GLA_KEY_DIM), jax.ShapeDtypeStruct((GLA_VAL_DIM, t), BF16),
        tok(GLA_VAL_DIM), tok(GLA_KEY_DIM), tok(2 * D_MODEL),
    ]
    out_specs = [
        row(SWA_HEADS * SWA_HEAD_DIM), row(kw), col(kw),
        row(GLA_KEY_DIM), row(GLA_KEY_DIM), col(GLA_VAL_DIM),
        row(GLA_VAL_DIM), row(GLA_KEY_DIM), row(2 * D_MODEL),
    ]
    cast_w = (w_attn_o, w_gla_o, w_mix_o)
    slab_rows = D_MODEL // steps
    group = SWA_HEADS // SWA_KV_HEADS
    assert slab_rows == SWA_HEAD_DIM and steps == SWA_HEADS and slab_rows % 16 == 0
    slab = pl.BlockSpec((slab_rows, D_MODEL), lambda i: (i, 0))
    head_slab = pl.BlockSpec((slab_rows, D_MODEL),
                             lambda i: (i // SWA_KV_HEADS + group * (i % SWA_KV_HEADS), 0))
    return pl.pallas_call(
        functools.partial(_in_proj_kernel, len(cast_w)),
        grid=(steps,),
        in_specs=[row(D_MODEL)] + [_const_spec(c.shape) for c in consts]
        + [head_slab, slab, slab],
        out_specs=out_specs + [slab] * len(cast_w),
        out_shape=out_shape + [jax.ShapeDtypeStruct(w.shape, BF16) for w in cast_w],
        compiler_params=_params(("arbitrary",)),
        name="in_proj",
    )(x2d, *consts, *cast_w)


def _swa_stages(sink_ref, q_ref, k_ref, kp_ref, vt_ref, vtp_ref, seq_start, write):
    blk = SWA_BLOCK
    win = blk + CHUNK
    low = lax.broadcasted_iota(jnp.int32, (CHUNK, LANES), 1) < SWA_HEAD_DIM
    low_row = lax.broadcasted_iota(jnp.int32, (1, LANES), 1) < SWA_HEAD_DIM
    prev_bias = jnp.where(seq_start, -1e30, 0.0).astype(F32)
    k_all = jnp.concatenate([kp_ref[...], k_ref[...]], axis=0)
    vt_all = jnp.concatenate([vtp_ref[...], vt_ref[...]], axis=1)
    zeros_chunk = jnp.zeros((CHUNK, LANES), BF16)
    n_pairs = SWA_HEADS // SWA_KV_HEADS
    n_blocks = q_ref.shape[0] // blk
    n_ch = blk // CHUNK
    live = {}

    def scores(p):
        res = []
        for u in range(n_blocks):
            qsel = []
            for qh in range(n_ch):
                r0 = u * blk + qh * CHUNK
                qp = q_ref[r0:r0 + CHUNK, p * LANES:(p + 1) * LANES]
                zero = jnp.zeros_like(qp)
                qsel += [jnp.where(low, qp, zero), jnp.where(low, zero, qp)]
            res.append(_dot_nt(k_all[u * blk:(u + 2) * blk], jnp.concatenate(qsel, axis=0)))
        live["s", p] = res

    def softmax_values(p):
        sink_row = jnp.where(low_row, sink_ref[p], sink_ref[p + n_pairs]) * LOG2E
        res = []
        for u, s_blk in enumerate(live.pop(("s", p))):
            probs, inv = [], []
            for qh in range(n_ch):
                st = s_blk[qh * CHUNK:qh * CHUNK + win, qh * LANES:(qh + 1) * LANES]
                if u == 0:
                    n_prev = blk - qh * CHUNK
                    st = jnp.concatenate([st[:n_prev] + prev_bias, st[n_prev:]], axis=0)
                m = jnp.maximum(jnp.max(st, axis=0, keepdims=True), sink_row)
                e = jnp.exp2(st - m)
                inv.append(1.0 / (jnp.sum(e, axis=0, keepdims=True) + jnp.exp2(sink_row - m)))
                probs.append(jnp.concatenate([zeros_chunk] * qh + [e.astype(BF16)]
                                             + [zeros_chunk] * (n_ch - 1 - qh), axis=0))
            ot = _dot(vt_all[:, u * blk:(u + 2) * blk], jnp.concatenate(probs, axis=1))
            res.append([ot[:, qh * LANES:(qh + 1) * LANES] * inv[qh] for qh in range(n_ch)])
        live["o", p] = res

    def store(p):
        for u, ots in enumerate(live.pop(("o", p))):
            for qh, ot in enumerate(ots):
                r0 = u * blk + qh * CHUNK
                o2 = ot.T
                write(slice(r0, r0 + CHUNK), slice(p * LANES, (p + 1) * LANES),
                      jnp.where(low, o2[:CHUNK], o2[CHUNK:]).astype(BF16))

    return [(functools.partial(scores, p), functools.partial(softmax_values, p),
             functools.partial(store, p)) for p in range(n_pairs)]


def _gla_stages(q_ref, k_ref, gk_ref, vt_ref, g_ref, st_ref, seq_start, write):
    pair = 2 * CHUNK
    heads = range(GLA_HEADS)
    ks = [slice(h * GLA_DK, (h + 1) * GLA_DK) for h in heads]
    vs = [slice(h * GLA_DV, (h + 1) * GLA_DV) for h in heads]
    r = lax.broadcasted_iota(jnp.int32, (pair, pair), 0)
    c = lax.broadcasted_iota(jnp.int32, (pair, pair), 1)
    tri = ((c <= r) & ((r // CHUNK) == (c // CHUNK))).astype(BF16)
    first = lax.broadcasted_iota(jnp.int32, (pair, GLA_DK), 0) < CHUNK
    eps_scaled = EPS * GLA_DK
    n_pairs = q_ref.shape[0] // pair
    sts = [jnp.where(seq_start, jnp.zeros((GLA_DV, GLA_DK), F32), st_ref[h]) for h in heads]
    live = {}

    def decay(i):
        sl = slice(i * pair, (i + 1) * pair)
        b_all = _dot(tri, gk_ref[sl, :])
        res = []
        for h in heads:
            b = b_all[:, ks[h]]
            b_end0 = b[CHUNK - 1:CHUNK, :]
            b_end1 = b[pair - 1:pair, :]
            k_dec = (k_ref[sl, ks[h]].astype(F32)
                     * jnp.exp(jnp.where(first, b_end0, b_end1) - b)).astype(BF16)
            zero = jnp.zeros_like(k_dec)
            kd = jnp.concatenate([jnp.where(first, k_dec, zero), jnp.where(first, zero, k_dec)],
                                 axis=1)
            res.append((jnp.exp(b_end0), jnp.exp(b_end1), kd))
        live["d", i] = res

    def key_values(i):
        sl = slice(i * pair, (i + 1) * pair)
        live["kv", i] = [(a0, a1, _dot(vt_ref[vs[h], sl], kd))
                         for h, (a0, a1, kd) in zip(heads, live.pop(("d", i)))]

    def state_and_out(i):
        sl = slice(i * pair, (i + 1) * pair)
        dk = live.pop(("kv", i))
        outs = []
        for h in heads:
            a0, a1, kv = dk[h]
            st0 = sts[h] * a0 + kv[:, :GLA_DK]
            st1 = st0 * a1 + kv[:, GLA_DK:]
            sts[h] = st1
            q = q_ref[sl, ks[h]]
            outs.append(jnp.concatenate([_dot_nt(q[:CHUNK], st0.astype(BF16)),
                                         _dot_nt(q[CHUNK:], st1.astype(BF16))], axis=0))
        for h in heads:
            o = outs[h]
            inv = lax.rsqrt(jnp.mean(o * o, axis=-1, keepdims=True) + eps_scaled)
            write(sl, vs[h], (o * inv).astype(BF16) * g_ref[sl, vs[h]])

    def save_state():
        for h in heads:
            st_ref[h] = sts[h]

    stages = [functools.partial(decay, 0)]
    for i in range(n_pairs):
        if i + 1 < n_pairs:
            stages.append(functools.partial(decay, i + 1))
        stages.append(functools.partial(key_values, i))
        if i > 0:
            stages.append(functools.partial(state_and_out, i - 1))
    stages += [functools.partial(state_and_out, n_pairs - 1), save_state]
    return stages


def _mix_stages(x_ref, gate_ref, wa_ref, wg_ref, wm_ref, oa_ref, og_ref, mg_ref, o_ref):
    pieces = [slice(c * MIX_CHUNK, (c + 1) * MIX_CHUNK) for c in range(D_MODEL // MIX_CHUNK)]

    def merge(cs):
        ya = _dot(oa_ref[...], wa_ref[:, cs])
        yg = _dot(og_ref[...], wg_ref[:, cs])
        cs_b = slice(D_MODEL + cs.start, D_MODEL + cs.stop)
        mg_ref[:, cs] = (gate_ref[:, cs].astype(F32) * ya
                         + gate_ref[:, cs_b].astype(F32) * yg).astype(BF16)

    def project(cs):
        o_ref[:, cs] = x_ref[:, cs] + _dot(mg_ref[...], wm_ref[:, cs])

    return ([functools.partial(merge, cs) for cs in pieces]
            + [functools.partial(project, cs) for cs in pieces])


def _interleave(major, minor):
    out, done = [], 0
    for n, stage in enumerate(major):
        out.append(stage)
        want = (n + 1) * len(minor) // len(major)
        out.extend(minor[done:want])
        done = want
    return out


def _run_lookahead(i, n_tiles, lead, lag, merge, init=None):
    @pl.when(i == 0)
    def _():
        if init is not None:
            init()
        for stage in merge(lead(), []):
            stage()

    @pl.when(jnp.logical_and(i > 0, i < n_tiles))
    def _():
        for stage in merge(lead(), lag()):
            stage()

    @pl.when(i == n_tiles)
    def _():
        for stage in merge([], lag()):
            stage()


def _attn_mix_kernel(tiles_per_seq, n_cast,
                     sink_ref, q_ref, k_ref, kp_ref, vt_ref, vtp_ref,
                     qg_ref, kg_ref, gk_ref, vgt_ref, gg_ref,
                     x_ref, gate_ref, wa_ref, wg_ref, wm_ref, *rest):
    cast_src, (o_ref, *cast_dst) = rest[:n_cast], rest[n_cast:2 * n_cast + 1]
    oa_scr, og_scr, mg_scr, st_ref = rest[2 * n_cast + 1:]
    i = pl.program_id(0)
    n_tiles = pl.num_programs(0) - 1
    seq_start = (i % tiles_per_seq) == 0
    slot_w = i % 2
    slot_r = 1 - slot_w

    def write_oa(rows, cols, val):
        oa_scr[slot_w, rows, cols] = val

    def write_og(rows, cols, val):
        og_scr[slot_w, rows, cols] = val

    def mixers():
        swa = _swa_stages(sink_ref, q_ref, k_ref, kp_ref, vt_ref, vtp_ref, seq_start, write_oa)
        gla = _gla_stages(qg_ref, kg_ref, gk_ref, vgt_ref, gg_ref, st_ref, seq_start, write_og)
        n = len(swa)
        order = [swa[0][0]]
        for p in range(n):
            if p + 1 < n:
                order.append(swa[p + 1][0])
            order.append(swa[p][1])
            if p > 0:
                order.append(swa[p - 1][2])
        order.append(swa[n - 1][2])
        return _interleave(order, gla) + [convert_weights]

    def projections():
        return _mix_stages(x_ref, gate_ref, wa_ref, wg_ref, wm_ref,
                           oa_scr.at[slot_r], og_scr.at[slot_r], mg_scr, o_ref)

    def merge(lead, lag):
        return _interleave(lead, lag) if lead else lag

    def convert_weights():
        for src, dst in zip(cast_src, cast_dst):
            dst[...] = src[...].astype(BF16)

    def init():
        st_ref[...] = jnp.zeros_like(st_ref)

    _run_lookahead(i, n_tiles, mixers, projections, merge, init)


def _attn_mix(sinks, qa, ka, vat, qg, kg, gk, vgt, gg, x2d, gates, wa, wg, wm, cast_w, seq):
    t = x2d.shape[0]
    tm = TM_ATTN
    n_tiles = t // tm
    blocks = tm // SWA_BLOCK
    kw = ka.shape[1]
    cur = lambda i: jnp.minimum(i, n_tiles - 1)
    prv = lambda i: jnp.maximum(i - 1, 0)
    prev_block = lambda i: jnp.maximum(cur(i) * blocks - 1, 0)
    tok = lambda n: pl.BlockSpec((tm, n), lambda i: (cur(i), 0))
    col = lambda n: pl.BlockSpec((n, tm), lambda i: (0, cur(i)))
    old = lambda n: pl.BlockSpec((tm, n), lambda i: (prv(i), 0))

    def slab(w):
        n_slabs = max(n for n in range(1, n_tiles + 1) if w.shape[0] % (16 * n) == 0)
        return pl.BlockSpec((w.shape[0] // n_slabs, w.shape[1]),
                            lambda i: (jnp.minimum(i, n_slabs - 1), 0))

    cast_specs = [slab(w) for w in cast_w]
    return pl.pallas_call(
        functools.partial(_attn_mix_kernel, seq // tm, len(cast_w)),
        grid=(n_tiles + 1,),
        in_specs=[
            pl.BlockSpec(memory_space=pltpu.SMEM),
            tok(qa.shape[1]), tok(kw),
            pl.BlockSpec((SWA_BLOCK, kw), lambda i: (prev_block(i), 0)),
            col(kw),
            pl.BlockSpec((kw, SWA_BLOCK), lambda i: (0, prev_block(i))),
            tok(GLA_KEY_DIM), tok(GLA_KEY_DIM), tok(GLA_KEY_DIM), col(GLA_VAL_DIM),
            tok(GLA_VAL_DIM),
            old(D_MODEL), old(2 * D_MODEL),
            _const_spec(wa.shape), _const_spec(wg.shape), _const_spec(wm.shape),
        ] + cast_specs,
        out_specs=[old(D_MODEL)] + cast_specs,
        out_shape=[jax.ShapeDtypeStruct((t, D_MODEL), F32)]
        + [jax.ShapeDtypeStruct(w.shape, BF16) for w in cast_w],
        scratch_shapes=[
            pltpu.VMEM((2, tm, SWA_HEADS * SWA_HEAD_DIM), BF16),
            pltpu.VMEM((2, tm, GLA_VAL_DIM), BF16),
            pltpu.VMEM((tm, D_MODEL), BF16),
            pltpu.VMEM((GLA_HEADS, GLA_DV, GLA_DK), F32),
        ],
        compiler_params=_params(("arbitrary",)),
        name="attn_mix",
    )(sinks, qa, ka, ka, vat, vat, qg, kg, gk, vgt, gg, x2d, gates, wa, wg, wm, *cast_w)


def _mem_kv_kernel(m_ref, nw_ref, w_ref, k_ref, v_ref):
    mn = _rms(m_ref[0], nw_ref[...]).astype(BF16)
    kv = _dot(mn, w_ref[...])
    k = kv[:, :MEM_WIDTH].astype(BF16)
    v = kv[:, MEM_WIDTH:].astype(BF16)
    head = lax.broadcasted_iota(jnp.int32, (N_MEM, MEM_WIDTH), 1) // MEM_HEAD_DIM
    zero = jnp.zeros_like(k)
    for h in range(MEM_HEADS):
        k_ref[0, h] = jnp.where(head == h, k, zero)
        v_ref[0, h] = jnp.where(head == h, v, zero)


def _mem_kv(mem, nw, w):
    batch = mem.shape[0]
    out = jax.ShapeDtypeStruct((batch, MEM_HEADS, N_MEM, MEM_WIDTH), BF16)
    spec = pl.BlockSpec((1, MEM_HEADS, N_MEM, MEM_WIDTH), lambda b: (b, 0, 0, 0))
    return pl.pallas_call(
        _mem_kv_kernel,
        grid=(batch,),
        in_specs=[pl.BlockSpec((1, N_MEM, D_MODEL), lambda b: (b, 0, 0)),
                  _const_spec(nw.shape), _const_spec(w.shape)],
        out_specs=(spec, spec),
        out_shape=(out, out),
        compiler_params=_params(("arbitrary",)),
        name="mem_kv",
    )(mem, nw, w)


def _xattn_stages(x_ref, k_ref, v_ref, nq_ref, wq_ref, wo_ref, nf_ref, x2_ref, hf_ref):
    live = {}

    def queries():
        hq = _rms(x_ref[...], nq_ref[...]).astype(BF16)
        live["q"] = (_dot(hq, wq_ref[...]) * (MEM_HEAD_DIM ** -0.5)).astype(BF16)
        live["o"] = jnp.zeros((x_ref.shape[0], MEM_WIDTH), F32)

    def head(h):
        s = _dot_nt(live["q"], k_ref[0, h])
        e = jnp.exp(s - jnp.max(s, axis=-1, keepdims=True))
        inv = 1.0 / jnp.sum(e, axis=-1, keepdims=True)
        live["o"] = live["o"] + _dot(e.astype(BF16), v_ref[0, h]) * inv

    def project():
        x2 = x_ref[...] + _dot(live.pop("o").astype(BF16), wo_ref[...])
        x2_ref[...] = x2
        hf_ref[...] = _rms(x2, nf_ref[...]).astype(BF16)

    return [queries] + [functools.partial(head, h) for h in range(MEM_HEADS)] + [project]


def _ffn_stages(x2_ref, hf_ref, wg_ref, wu_ref, wd_ref, nfin_ref, act_ref, o_ref):
    live = {"ss": jnp.zeros((o_ref.shape[0], 1), F32)}

    def hidden(cs):
        hf = hf_ref[...]
        g = _dot(hf, wg_ref[:, cs])
        u = _dot(hf, wu_ref[:, cs])
        act_ref[:, cs] = (g * jax.nn.sigmoid(g) * u).astype(BF16)

    def down(cs):
        y = x2_ref[:, cs] + _dot(act_ref[...], wd_ref[:, cs])
        live["ss"] = live["ss"] + jnp.sum(y * y, axis=-1, keepdims=True)
        o_ref[:, cs] = y

    def final_norm():
        inv = lax.rsqrt(live["ss"] * (1.0 / D_MODEL) + EPS)
        o_ref[...] = o_ref[...] * inv * nfin_ref[...]

    hid = [slice(c * FF_CHUNK, (c + 1) * FF_CHUNK) for c in range(D_FF // FF_CHUNK)]
    out = [slice(c * MIX_CHUNK, (c + 1) * MIX_CHUNK) for c in range(D_MODEL // MIX_CHUNK)]
    return ([functools.partial(hidden, cs) for cs in hid]
            + [functools.partial(down, cs) for cs in out] + [final_norm])


def _tail_kernel(x_ref, k_ref, v_ref, nq_ref, wq_ref, wo_ref, nf_ref, wg_ref, wu_ref, wd_ref,
                 nfin_ref, o_ref, x2_scr, hf_scr, act_ref):
    i = pl.program_id(0)
    n_tiles = pl.num_programs(0) - 1
    slot_w = i % 2
    slot_r = 1 - slot_w

    def xattn():
        return _xattn_stages(x_ref, k_ref, v_ref, nq_ref, wq_ref, wo_ref, nf_ref,
                             x2_scr.at[slot_w], hf_scr.at[slot_w])

    def ffn():
        return _ffn_stages(x2_scr.at[slot_r], hf_scr.at[slot_r], wg_ref, wu_ref, wd_ref,
                           nfin_ref, act_ref, o_ref)

    def merge(lead, lag):
        if not lag:
            return lead
        n_hidden = D_FF // FF_CHUNK
        n_first = n_hidden - len(lead)
        return lag[:n_first] + _interleave(lag[n_first:n_hidden], lead) + lag[n_hidden:]

    _run_lookahead(i, n_tiles, xattn, ffn, merge)


def _tail(x2d, kext, vext, nq, wq, wo, nf, wg, wu, wd, nfin, seq):
    tm = TM_TAIL
    t = x2d.shape[0]
    n_tiles = t // tm
    tiles_per_seq = seq // tm
    cur = lambda i: jnp.minimum(i, n_tiles - 1)
    prv = lambda i: jnp.maximum(i - 1, 0)
    mem_spec = pl.BlockSpec((1, MEM_HEADS, N_MEM, MEM_WIDTH),
                            lambda i: (cur(i) // tiles_per_seq, 0, 0, 0))
    consts = (nq, wq, wo, nf, wg, wu, wd, nfin)
    return pl.pallas_call(
        _tail_kernel,
        grid=(n_tiles + 1,),
        in_specs=[pl.BlockSpec((tm, D_MODEL), lambda i: (cur(i), 0)), mem_spec, mem_spec]
        + [_const_spec(c.shape) for c in consts],
        out_specs=pl.BlockSpec((tm, D_MODEL), lambda i: (prv(i), 0)),
        out_shape=jax.ShapeDtypeStruct((t, D_MODEL), F32),
        scratch_shapes=[pltpu.VMEM((2, tm, D_MODEL), F32), pltpu.VMEM((2, tm, D_MODEL), BF16),
                        pltpu.VMEM((tm, D_FF), BF16)],
        compiler_params=_params(("arbitrary",)),
        name="tail",
    )(x2d, kext, vext, *consts)


def _layer(x2d, mem, batch, seq, norm_mix_w, w_in, b_gate, attn_sinks, gla_gate_w2, gla_gate_b,
           gla_norm_w, w_attn_o, w_gla_o, w_mix_o, norm_mem_q_w, norm_mem_kv_w, w_mem_q,
           w_mem_kv, w_mem_o, norm_ffn_w, w_ffn_gate, w_ffn_up, w_ffn_down, out_norm_w):
    w2_p = jnp.pad(gla_gate_w2, ((0, LANES - GLA_GATE_RANK), (0, 0)))
    bf = lambda a: a.astype(BF16)
    r2 = lambda a: a.reshape(1, -1)

    qa, ka, vat, qg, kg, vgt, gg, gk, gates, wa16, wgo16, wmo16 = _in_proj(
        x2d, r2(norm_mix_w), bf(w_in.T), bf(w2_p), r2(gla_gate_b), r2(b_gate),
        r2(jnp.tile(gla_norm_w, GLA_HEADS)), w_attn_o, w_gla_o, w_mix_o)
    x1, wg16, wu16, wd16, wq16, wo16, wkv16 = _attn_mix(
        attn_sinks, qa, ka, vat, qg, kg, gk, vgt, gg, x2d, gates, wa16, wgo16, wmo16,
        (w_ffn_gate, w_ffn_up, w_ffn_down, w_mem_q, w_mem_o, w_mem_kv), seq)
    kext, vext = _mem_kv(mem, r2(norm_mem_kv_w), wkv16)
    return _tail(x1, kext, vext, r2(norm_mem_q_w), wq16, wo16, r2(norm_ffn_w),
                 wg16, wu16, wd16, r2(out_norm_w), seq)


def kernel(x, mem, norm_mix_w, w_in, b_gate, attn_sinks, gla_gate_w2, gla_gate_b, gla_norm_w,
           w_attn_o, w_gla_o, w_mix_o, norm_mem_q_w, norm_mem_kv_w, w_mem_q, w_mem_kv, w_mem_o,
           norm_ffn_w, w_ffn_gate, w_ffn_up, w_ffn_down, norm_final_w):
    batch, seq, d = x.shape
    depth = w_in.shape[0]
    assert depth == 1 and d == D_MODEL
    assert seq % TM_PROJ == 0 and seq % TM_ATTN == 0 and seq % TM_TAIL == 0
    out = _layer(x.reshape(batch * seq, d), mem, batch, seq, norm_mix_w[0], w_in[0], b_gate[0],
                 attn_sinks[0], gla_gate_w2[0], gla_gate_b[0], gla_norm_w[0], w_attn_o[0],
                 w_gla_o[0], w_mix_o[0], norm_mem_q_w[0], norm_mem_kv_w[0], w_mem_q[0],
                 w_mem_kv[0], w_mem_o[0], norm_ffn_w[0], w_ffn_gate[0], w_ffn_up[0],
                 w_ffn_down[0], norm_final_w)
    return out.reshape(batch, seq, d)
```

```python
import functools

import jax
import jax.numpy as jnp
from jax import lax
from jax.experimental import pallas as pl
from jax.experimental.pallas import tpu as pltpu

D_MODEL = 1024
CHUNK = 64
N_MEM = 256
EPS = 1e-6

SWA_HEADS = 16
SWA_KV_HEADS = 2
SWA_HEAD_DIM = 64
SWA_BLOCK = 128

GLA_HEADS = 4
GLA_KEY_DIM = D_MODEL // 2
GLA_VAL_DIM = D_MODEL
GLA_DK = GLA_KEY_DIM // GLA_HEADS
GLA_DV = GLA_VAL_DIM // GLA_HEADS
GLA_GATE_RANK = 16
GLA_GATE_NORM = 16.0

MEM_HEADS = 4
MEM_HEAD_DIM = 64
MEM_WIDTH = MEM_HEADS * MEM_HEAD_DIM

D_FF = -(-(8 * D_MODEL) // (3 * 256)) * 256

IN_SIZES = (SWA_HEADS * SWA_HEAD_DIM, SWA_KV_HEADS * SWA_HEAD_DIM, SWA_KV_HEADS * SWA_HEAD_DIM,
            GLA_KEY_DIM, GLA_KEY_DIM, GLA_VAL_DIM, GLA_VAL_DIM, GLA_GATE_RANK, 2 * D_MODEL)
IN_OFFSETS = tuple(sum(IN_SIZES[:i]) for i in range(len(IN_SIZES) + 1))

LANES = 128
VMEM_LIMIT = 56 * 1024 * 1024

LOG2E = 1.4426950408889634

TM_PROJ = 1024
W_IN_CHUNK = 128
W_IN_RING = 6
PROJ_M = 512
PROJ_N = 256
TM_ATTN = 512
TM_TAIL = 512
MIX_CHUNK = 256
FF_CHUNK = 256

BF16 = jnp.bfloat16
F32 = jnp.float32


def _rms(x, w):
    return x * lax.rsqrt(jnp.mean(x * x, axis=-1, keepdims=True) + EPS) * w


def _dot(a, b):
    return jnp.dot(a, b, preferred_element_type=F32)


def _dot_nt(a, b):
    return lax.dot_general(a, b, (((1,), (1,)), ((), ())), preferred_element_type=F32)


def _const_spec(shape):
    zeros = (0,) * len(shape)
    return pl.BlockSpec(shape, lambda *_: zeros, pipeline_mode=pl.Buffered(1))


def _params(semantics):
    return pltpu.CompilerParams(dimension_semantics=semantics, vmem_limit_bytes=VMEM_LIMIT)


def _run_streaming_weights(pieces, src_hbm, dst_ref, stage_ref, sem_ref):
    depth, chunk = stage_ref.shape[0], stage_ref.shape[1]
    total = src_hbm.shape[0]
    first_use = []
    for _, needs in pieces:
        for row0, n in needs:
            for k in range(row0 // chunk, (row0 + n - 1) // chunk + 1):
                if k not in first_use:
                    first_use.append(k)

    def copy(j):
        r0 = first_use[j] * chunk
        n = min(chunk, total - r0)
        return pltpu.make_async_copy(src_hbm.at[pl.ds(r0, n), :],
                                     stage_ref.at[j % depth, pl.ds(0, n), :],
                                     sem_ref.at[j % depth])

    for j in range(min(depth, len(first_use))):
        copy(j).start()
    done = 0
    for emit, needs in pieces:
        last = max([first_use.index(k) for row0, n in needs
                    for k in range(row0 // chunk, (row0 + n - 1) // chunk + 1)] + [-1])
        while done <= last:
            r0 = first_use[done] * chunk
            n = min(chunk, total - r0)
            copy(done).wait()
            dst_ref[r0:r0 + n, :] = stage_ref[done % depth, 0:n, :].astype(BF16)
            if done + depth < len(first_use):
                copy(done + depth).start()
            done += 1
        emit()
    assert done == len(first_use)


def _in_proj_kernel(n_cast, x_ref, nw_ref, wt_hbm, w2_ref, gb_ref, bg_ref, gn_ref, *rest):
    cast_src, cast_dst = rest[:n_cast], rest[len(rest) - n_cast - 3:len(rest) - 3]
    (qa_ref, ka_ref, vat_ref, qg_ref, kg_ref, vgt_ref, gg_ref, gk_ref,
     gate_ref) = rest[n_cast:len(rest) - n_cast - 3]
    wt_ref, stage_ref, sem_ref = rest[len(rest) - 3:]
    for src, dst in zip(cast_src, cast_dst):
        dst[...] = src[...].astype(BF16)
    o_qa, o_ka, o_va, o_qg, o_kg, o_vg, o_gg, o_alr, o_gate, o_end = IN_OFFSETS
    hd = SWA_HEAD_DIM
    tm = x_ref.shape[0]
    row_halves = [slice(m0, m0 + PROJ_M) for m0 in range(0, tm, PROJ_M)]
    h = {rows.start: _rms(x_ref[rows, :], nw_ref[...]).astype(BF16) for rows in row_halves}

    def project(rows, weight_ranges, n_cols, store):
        def piece(c0):
            w = [wt_ref[r0:r0 + n, :] for r0, n in weight_ranges(c0)]
            w = w[0] if len(w) == 1 else jnp.concatenate(w, axis=0)
            store(rows, c0, _dot_nt(h[rows.start], w))
        return [(functools.partial(piece, c0), weight_ranges(c0))
                for c0 in range(0, n_cols, PROJ_N)]

    def project_t(rows, row0, n_rows, dst_ref):
        def piece(t0, r0, nr):
            dst_ref[r0:r0 + nr, rows.start + t0:rows.start + t0 + PROJ_N] = _dot_nt(
                wt_ref[row0 + r0:row0 + r0 + nr, :], h[rows.start][t0:t0 + PROJ_N]).astype(BF16)
        return [(functools.partial(piece, t0, r0, min(PROJ_M, n_rows - r0)),
                 [(row0 + r0, min(PROJ_M, n_rows - r0))])
                for t0 in range(0, PROJ_M, PROJ_N) for r0 in range(0, n_rows, PROJ_M)]

    def from_wt(row0):
        return lambda c0: [(row0 + c0, PROJ_N)]

    def plain(dst_ref):
        def store(rows, c0, y):
            dst_ref[rows, c0:c0 + PROJ_N] = y.astype(BF16)
        return store

    def store_gates(lo):
        def store(rows, c0, y):
            cs = slice(lo + c0, lo + c0 + PROJ_N)
            gate_ref[rows, cs] = jax.nn.sigmoid(y + bg_ref[:, cs]).astype(BF16)
        return store

    def store_q(rows, c0, y):
        qa_ref[rows, c0:c0 + PROJ_N] = (y * (hd ** -0.5 * LOG2E)).astype(BF16)

    def store_swish(rows, c0, y):
        gg_ref[rows, c0:c0 + PROJ_N] = (y * jax.nn.sigmoid(y)
                                        * gn_ref[:, c0:c0 + PROJ_N]).astype(BF16)

    a_lr = {}

    def store_keys(rows, c0, y):
        kw = SWA_KV_HEADS * hd
        ka_ref[rows, :] = y[:, :kw].astype(BF16)
        a_lr[rows.start] = y[:, kw:].astype(BF16)

    def log_decay(rows, zs):
        z = _dot(a_lr[rows.start], w2_ref[:, zs]) + gb_ref[:, zs]
        log_sig = jnp.minimum(z, 0.0) - jnp.log(1.0 + jnp.exp(-jnp.abs(z)))
        gk_ref[rows, zs] = (log_sig * (1.0 / GLA_GATE_NORM)).astype(BF16)

    group = SWA_HEADS // SWA_KV_HEADS

    def query_rows(c0):
        tiles = range(c0 // LANES, (c0 + PROJ_N) // LANES)
        return [(o_qa + (p + j * group) * hd, hd) for p in tiles for j in range(SWA_KV_HEADS)]

    key_alr_rows = lambda c0: [(o_ka, o_va - o_ka), (o_alr, LANES)]
    half = (o_end - o_gate) // 2
    pieces = []
    for rows in row_halves:
        heavy = (project(rows, from_wt(o_gate), half, store_gates(0))
                 + project(rows, from_wt(o_gate + half), half, store_gates(half))
                 + project(rows, from_wt(o_gg), o_alr - o_gg, store_swish))
        light = (project(rows, query_rows, o_ka - o_qa, store_q)
                 + project_t(rows, o_vg, o_gg - o_vg, vgt_ref)
                 + project(rows, from_wt(o_qg), o_kg - o_qg, plain(qg_ref)))
        decay = [(functools.partial(log_decay, rows, slice(z0, z0 + PROJ_N)), [])
                 for z0 in range(0, GLA_KEY_DIM, PROJ_N)]
        rest = (project(rows, from_wt(o_kg), o_vg - o_kg, plain(kg_ref))
                + project_t(rows, o_va, o_qg - o_va, vat_ref))
        pieces += (_interleave(heavy, light) + project(rows, key_alr_rows, PROJ_N, store_keys)
                   + _interleave(rest, decay))

    @pl.when(pl.program_id(0) == 0)
    def _():
        _run_streaming_weights(pieces, wt_hbm, wt_ref, stage_ref, sem_ref)

    @pl.when(pl.program_id(0) > 0)
    def _():
        for emit, _ in pieces:
            emit()


def _in_proj(x2d, nw, wt, w2, gb, bg, gn, w_attn_o, w_gla_o, w_mix_o):
    t = x2d.shape[0]
    tm = TM_PROJ
    steps = t // tm
    row = lambda n: pl.BlockSpec((tm, n), lambda i: (i, 0))
    col = lambda n: pl.BlockSpec((n, tm), lambda i: (0, i))
    consts = (nw, wt, w2, gb, bg, gn)
    const_specs = [pl.BlockSpec(memory_space=pl.ANY) if c is wt else _const_spec(c.shape)
                   for c in consts]
    kw = SWA_KV_HEADS * SWA_HEAD_DIM
    tok = lambda n: jax.ShapeDtypeStruct((t, n), BF16)
    out_shape = [
        tok(SWA_HEADS * SWA_HEAD_DIM), tok(kw), jax.ShapeDtypeStruct((kw, t), BF16),
        tok(GLA_KEY_DIM), tok(GLA_KEY_DIM), jax.ShapeDtypeStruct((GLA_VAL_DIM, t), BF16),
        tok(GLA_VAL_DIM), tok(GLA_KEY_DIM), tok(2 * D_MODEL),
    ]
    out_specs = [
        row(SWA_HEADS * SWA_HEAD_DIM), row(kw), col(kw),
        row(GLA_KEY_DIM), row(GLA_KEY_DIM), col(GLA_VAL_DIM),
        row(GLA_VAL_DIM), row(GLA_KEY_DIM), row(2 * D_MODEL),
    ]
    cast_w = (w_attn_o, w_gla_o, w_mix_o)
    slab_rows = D_MODEL // steps
    group = SWA_HEADS // SWA_KV_HEADS
    assert slab_rows == SWA_HEAD_DIM and steps == SWA_HEADS and slab_rows % 16 == 0
    slab = pl.BlockSpec((slab_rows, D_MODEL), lambda i: (i, 0))
    head_slab = pl.BlockSpec((slab_rows, D_MODEL),
                             lambda i: (i // SWA_KV_HEADS + group * (i % SWA_KV_HEADS), 0))
    return pl.pallas_call(
        functools.partial(_in_proj_kernel, len(cast_w)),
        grid=(steps,),
        in_specs=[row(D_MODEL)] + const_specs + [head_slab, slab, slab],
        out_specs=out_specs + [slab] * len(cast_w),
        out_shape=out_shape + [jax.ShapeDtypeStruct(w.shape, BF16) for w in cast_w],
        scratch_shapes=[pltpu.VMEM(wt.shape, BF16),
                        pltpu.VMEM((W_IN_RING, W_IN_CHUNK, D_MODEL), F32),
                        pltpu.SemaphoreType.DMA((W_IN_RING,))],
        compiler_params=_params(("arbitrary",)),
        name="in_proj",
    )(x2d, *consts, *cast_w)


def _swa_stages(sink_ref, q_ref, k_ref, kp_ref, vt_ref, vtp_ref, seq_start, write):
    blk = SWA_BLOCK
    win = blk + CHUNK
    low = lax.broadcasted_iota(jnp.int32, (CHUNK, LANES), 1) < SWA_HEAD_DIM
    low_row = lax.broadcasted_iota(jnp.int32, (1, LANES), 1) < SWA_HEAD_DIM
    prev_bias = jnp.where(seq_start, -1e30, 0.0).astype(F32)
    k_all = jnp.concatenate([kp_ref[...], k_ref[...]], axis=0)
    vt_all = jnp.concatenate([vtp_ref[...], vt_ref[...]], axis=1)
    zeros_chunk = jnp.zeros((CHUNK, LANES), BF16)
    n_pairs = SWA_HEADS // SWA_KV_HEADS
    n_blocks = q_ref.shape[0] // blk
    n_ch = blk // CHUNK
    live = {}

    def scores(p):
        res = []
        for u in range(n_blocks):
            qsel = []
            for qh in range(n_ch):
                r0 = u * blk + qh * CHUNK
                qp = q_ref[r0:r0 + CHUNK, p * LANES:(p + 1) * LANES]
                zero = jnp.zeros_like(qp)
                qsel += [jnp.where(low, qp, zero), jnp.where(low, zero, qp)]
            res.append(_dot_nt(k_all[u * blk:(u + 2) * blk], jnp.concatenate(qsel, axis=0)))
        live["s", p] = res

    def softmax_values(p):
        sink_row = jnp.where(low_row, sink_ref[p], sink_ref[p + n_pairs]) * LOG2E
        res = []
        for u, s_blk in enumerate(live.pop(("s", p))):
            probs, inv = [], []
            for qh in range(n_ch):
                st = s_blk[qh * CHUNK:qh * CHUNK + win, qh * LANES:(qh + 1) * LANES]
                if u == 0:
                    n_prev = blk - qh * CHUNK
                    st = jnp.concatenate([st[:n_prev] + prev_bias, st[n_prev:]], axis=0)
                m = jnp.maximum(jnp.max(st, axis=0, keepdims=True), sink_row)
                e = jnp.exp2(st - m)
                inv.append(1.0 / (jnp.sum(e, axis=0, keepdims=True) + jnp.exp2(sink_row - m)))
                probs.append(jnp.concatenate([zeros_chunk] * qh + [e.astype(BF16)]
                                             + [zeros_chunk] * (n_ch - 1 - qh), axis=0))
            ot = _dot(vt_all[:, u * blk:(u + 2) * blk], jnp.concatenate(probs, axis=1))
            res.append([ot[:, qh * LANES:(qh + 1) * LANES] * inv[qh] for qh in range(n_ch)])
        live["o", p] = res

    def store(p):
        for u, ots in enumerate(live.pop(("o", p))):
            for qh, ot in enumerate(ots):
                r0 = u * blk + qh * CHUNK
                o2 = ot.T
                write(slice(r0, r0 + CHUNK), slice(p * LANES, (p + 1) * LANES),
                      jnp.where(low, o2[:CHUNK], o2[CHUNK:]).astype(BF16))

    return [(functools.partial(scores, p), functools.partial(softmax_values, p),
             functools.partial(store, p)) for p in range(n_pairs)]


def _gla_stages(q_ref, k_ref, gk_ref, vt_ref, g_ref, st_ref, seq_start, write):
    pair = 2 * CHUNK
    heads = range(GLA_HEADS)
    ks = [slice(h * GLA_DK, (h + 1) * GLA_DK) for h in heads]
    vs = [slice(h * GLA_DV, (h + 1) * GLA_DV) for h in heads]
    r = lax.broadcasted_iota(jnp.int32, (pair, pair), 0)
    c = lax.broadcasted_iota(jnp.int32, (pair, pair), 1)
    tri = ((c <= r) & ((r // CHUNK) == (c // CHUNK))).astype(BF16)
    first = lax.broadcasted_iota(jnp.int32, (pair, GLA_DK), 0) < CHUNK
    eps_scaled = EPS * GLA_DK
    n_pairs = q_ref.shape[0] // pair
    sts = [jnp.where(seq_start, jnp.zeros((GLA_DV, GLA_DK), F32), st_ref[h]) for h in heads]
    live = {}

    def decay(i):
        sl = slice(i * pair, (i + 1) * pair)
        b_all = _dot(tri, gk_ref[sl, :])
        res = []
        for h in heads:
            b = b_all[:, ks[h]]
            b_end0 = b[CHUNK - 1:CHUNK, :]
            b_end1 = b[pair - 1:pair, :]
            k_dec = (k_ref[sl, ks[h]].astype(F32)
                     * jnp.exp(jnp.where(first, b_end0, b_end1) - b)).astype(BF16)
            zero = jnp.zeros_like(k_dec)
            kd = jnp.concatenate([jnp.where(first, k_dec, zero), jnp.where(first, zero, k_dec)],
                                 axis=1)
            res.append((jnp.exp(b_end0), jnp.exp(b_end1), kd))
        live["d", i] = res

    def key_values(i):
        sl = slice(i * pair, (i + 1) * pair)
        live["kv", i] = [(a0, a1, _dot(vt_ref[vs[h], sl], kd))
                         for h, (a0, a1, kd) in zip(heads, live.pop(("d", i)))]

    def state_and_out(i):
        sl = slice(i * pair, (i + 1) * pair)
        dk = live.pop(("kv", i))
        outs = []
        for h in heads:
            a0, a1, kv = dk[h]
            st0 = sts[h] * a0 + kv[:, :GLA_DK]
            st1 = st0 * a1 + kv[:, GLA_DK:]
            sts[h] = st1
            q = q_ref[sl, ks[h]]
            outs.append(jnp.concatenate([_dot_nt(q[:CHUNK], st0.astype(BF16)),
                                         _dot_nt(q[CHUNK:], st1.astype(BF16))], axis=0))
        for h in heads:
            o = outs[h]
            inv = lax.rsqrt(jnp.mean(o * o, axis=-1, keepdims=True) + eps_scaled)
            write(sl, vs[h], (o * inv).astype(BF16) * g_ref[sl, vs[h]])

    def save_state():
        for h in heads:
            st_ref[h] = sts[h]

    stages = [functools.partial(decay, 0)]
    for i in range(n_pairs):
        if i + 1 < n_pairs:
            stages.append(functools.partial(decay, i + 1))
        stages.append(functools.partial(key_values, i))
        if i > 0:
            stages.append(functools.partial(state_and_out, i - 1))
    stages += [functools.partial(state_and_out, n_pairs - 1), save_state]
    return stages


def _mix_stages(x_ref, gate_ref, wa_ref, wg_ref, wm_ref, oa_ref, og_ref, mg_ref, o_ref):
    pieces = [slice(c * MIX_CHUNK, (c + 1) * MIX_CHUNK) for c in range(D_MODEL // MIX_CHUNK)]

    def merge(cs):
        ya = _dot(oa_ref[...], wa_ref[:, cs])
        yg = _dot(og_ref[...], wg_ref[:, cs])
        cs_b = slice(D_MODEL + cs.start, D_MODEL + cs.stop)
        mg_ref[:, cs] = (gate_ref[:, cs].astype(F32) * ya
                         + gate_ref[:, cs_b].astype(F32) * yg).astype(BF16)

    def project(cs):
        o_ref[:, cs] = x_ref[:, cs] + _dot(mg_ref[...], wm_ref[:, cs])

    return ([functools.partial(merge, cs) for cs in pieces]
            + [functools.partial(project, cs) for cs in pieces])


def _interleave(major, minor):
    out, done = [], 0
    for n, stage in enumerate(major):
        out.append(stage)
        want = (n + 1) * len(minor) // len(major)
        out.extend(minor[done:want])
        done = want
    return out


def _run_lookahead(i, n_tiles, lead, lag, merge, init=None):
    @pl.when(i == 0)
    def _():
        if init is not None:
            init()
        for stage in merge(lead(), []):
            stage()

    @pl.when(jnp.logical_and(i > 0, i < n_tiles))
    def _():
        for stage in merge(lead(), lag()):
            stage()

    @pl.when(i == n_tiles)
    def _():
        for stage in merge([], lag()):
            stage()


def _attn_mix_kernel(tiles_per_seq, n_cast,
                     sink_ref, q_ref, k_ref, kp_ref, vt_ref, vtp_ref,
                     qg_ref, kg_ref, gk_ref, vgt_ref, gg_ref,
                     x_ref, gate_ref, wa_ref, wg_ref, wm_ref, *rest):
    cast_src, (o_ref, *cast_dst) = rest[:n_cast], rest[n_cast:2 * n_cast + 1]
    oa_scr, og_scr, mg_scr, st_ref = rest[2 * n_cast + 1:]
    i = pl.program_id(0)
    n_tiles = pl.num_programs(0) - 1
    seq_start = (i % tiles_per_seq) == 0
    slot_w = i % 2
    slot_r = 1 - slot_w

    def write_oa(rows, cols, val):
        oa_scr[slot_w, rows, cols] = val

    def write_og(rows, cols, val):
        og_scr[slot_w, rows, cols] = val

    def mixers():
        swa = _swa_stages(sink_ref, q_ref, k_ref, kp_ref, vt_ref, vtp_ref, seq_start, write_oa)
        gla = _gla_stages(qg_ref, kg_ref, gk_ref, vgt_ref, gg_ref, st_ref, seq_start, write_og)
        n = len(swa)
        order = [swa[0][0]]
        for p in range(n):
            if p + 1 < n:
                order.append(swa[p + 1][0])
            order.append(swa[p][1])
            if p > 0:
                order.append(swa[p - 1][2])
        order.append(swa[n - 1][2])
        return _interleave(order, gla) + [convert_weights]

    def projections():
        return _mix_stages(x_ref, gate_ref, wa_ref, wg_ref, wm_ref,
                           oa_scr.at[slot_r], og_scr.at[slot_r], mg_scr, o_ref)

    def merge(lead, lag):
        return _interleave(lead, lag) if lead else lag

    def convert_weights():
        for src, dst in zip(cast_src, cast_dst):
            dst[...] = src[...].astype(BF16)

    def init():
        st_ref[...] = jnp.zeros_like(st_ref)

    _run_lookahead(i, n_tiles, mixers, projections, merge, init)


def _attn_mix(sinks, qa, ka, vat, qg, kg, gk, vgt, gg, x2d, gates, wa, wg, wm, cast_w, seq):
    t = x2d.shape[0]
    tm = TM_ATTN
    n_tiles = t // tm
    blocks = tm // SWA_BLOCK
    kw = ka.shape[1]
    cur = lambda i: jnp.minimum(i, n_tiles - 1)
    prv = lambda i: jnp.maximum(i - 1, 0)
    prev_block = lambda i: jnp.maximum(cur(i) * blocks - 1, 0)
    tok = lambda n: pl.BlockSpec((tm, n), lambda i: (cur(i), 0))
    col = lambda n: pl.BlockSpec((n, tm), lambda i: (0, cur(i)))
    old = lambda n: pl.BlockSpec((tm, n), lambda i: (prv(i), 0))

    def slab(w):
        n_slabs = max(n for n in range(1, n_tiles + 1) if w.shape[0] % (16 * n) == 0)
        return pl.BlockSpec((w.shape[0] // n_slabs, w.shape[1]),
                            lambda i: (jnp.minimum(i, n_slabs - 1), 0))

    cast_specs = [slab(w) for w in cast_w]
    return pl.pallas_call(
        functools.partial(_attn_mix_kernel, seq // tm, len(cast_w)),
        grid=(n_tiles + 1,),
        in_specs=[
            pl.BlockSpec(memory_space=pltpu.SMEM),
            tok(qa.shape[1]), tok(kw),
            pl.BlockSpec((SWA_BLOCK, kw), lambda i: (prev_block(i), 0)),
            col(kw),
            pl.BlockSpec((kw, SWA_BLOCK), lambda i: (0, prev_block(i))),
            tok(GLA_KEY_DIM), tok(GLA_KEY_DIM), tok(GLA_KEY_DIM), col(GLA_VAL_DIM),
            tok(GLA_VAL_DIM),
            old(D_MODEL), old(2 * D_MODEL),
            _const_spec(wa.shape), _const_spec(wg.shape), _const_spec(wm.shape),
        ] + cast_specs,
        out_specs=[old(D_MODEL)] + cast_specs,
        out_shape=[jax.ShapeDtypeStruct((t, D_MODEL), F32)]
        + [jax.ShapeDtypeStruct(w.shape, BF16) for w in cast_w],
        scratch_shapes=[
            pltpu.VMEM((2, tm, SWA_HEADS * SWA_HEAD_DIM), BF16),
            pltpu.VMEM((2, tm, GLA_VAL_DIM), BF16),
            pltpu.VMEM((tm, D_MODEL), BF16),
            pltpu.VMEM((GLA_HEADS, GLA_DV, GLA_DK), F32),
        ],
        compiler_params=_params(("arbitrary",)),
        name="attn_mix",
    )(sinks, qa, ka, ka, vat, vat, qg, kg, gk, vgt, gg, x2d, gates, wa, wg, wm, *cast_w)


def _mem_kv_kernel(m_ref, nw_ref, w_ref, k_ref, v_ref):
    mn = _rms(m_ref[0], nw_ref[...]).astype(BF16)
    kv = _dot(mn, w_ref[...])
    k = kv[:, :MEM_WIDTH].astype(BF16)
    v = kv[:, MEM_WIDTH:].astype(BF16)
    head = lax.broadcasted_iota(jnp.int32, (N_MEM, MEM_WIDTH), 1) // MEM_HEAD_DIM
    zero = jnp.zeros_like(k)
    for h in range(MEM_HEADS):
        k_ref[0, h] = jnp.where(head == h, k, zero)
        v_ref[0, h] = jnp.where(head == h, v, zero)


def _mem_kv(mem, nw, w):
    batch = mem.shape[0]
    out = jax.ShapeDtypeStruct((batch, MEM_HEADS, N_MEM, MEM_WIDTH), BF16)
    spec = pl.BlockSpec((1, MEM_HEADS, N_MEM, MEM_WIDTH), lambda b: (b, 0, 0, 0))
    return pl.pallas_call(
        _mem_kv_kernel,
        grid=(batch,),
        in_specs=[pl.BlockSpec((1, N_MEM, D_MODEL), lambda b: (b, 0, 0)),
                  _const_spec(nw.shape), _const_spec(w.shape)],
        out_specs=(spec, spec),
        out_shape=(out, out),
        compiler_params=_params(("arbitrary",)),
        name="mem_kv",
    )(mem, nw, w)


def _xattn_stages(x_ref, k_ref, v_ref, nq_ref, wq_ref, wo_ref, nf_ref, x2_ref, hf_ref):
    live = {}

    def queries():
        hq = _rms(x_ref[...], nq_ref[...]).astype(BF16)
        live["q"] = (_dot(hq, wq_ref[...]) * (MEM_HEAD_DIM ** -0.5)).astype(BF16)
        live["o"] = jnp.zeros((x_ref.shape[0], MEM_WIDTH), F32)

    def head(h):
        s = _dot_nt(live["q"], k_ref[0, h])
        e = jnp.exp(s - jnp.max(s, axis=-1, keepdims=True))
        inv = 1.0 / jnp.sum(e, axis=-1, keepdims=True)
        live["o"] = live["o"] + _dot(e.astype(BF16), v_ref[0, h]) * inv

    def project():
        o = live.pop("o").astype(BF16)
        for c in range(D_MODEL // MIX_CHUNK):
            cs = slice(c * MIX_CHUNK, (c + 1) * MIX_CHUNK)
            x2_ref[:, cs] = x_ref[:, cs] + _dot(o, wo_ref[:, cs])
        hf_ref[...] = _rms(x2_ref[...], nf_ref[...]).astype(BF16)

    return [queries] + [functools.partial(head, h) for h in range(MEM_HEADS)] + [project]


def _ffn_stages(x2_ref, hf_ref, wg_ref, wu_ref, wd_ref, nfin_ref, act_ref, o_ref):
    live = {"ss": jnp.zeros((o_ref.shape[0], 1), F32)}

    def hidden(cs):
        hf = hf_ref[...]
        g = _dot(hf, wg_ref[:, cs])
        u = _dot(hf, wu_ref[:, cs])
        act_ref[:, cs] = (g * jax.nn.sigmoid(g) * u).astype(BF16)

    def down(cs):
        y = x2_ref[:, cs] + _dot(act_ref[...], wd_ref[:, cs])
        live["ss"] = live["ss"] + jnp.sum(y * y, axis=-1, keepdims=True)
        o_ref[:, cs] = y

    def final_norm():
        inv = lax.rsqrt(live["ss"] * (1.0 / D_MODEL) + EPS)
        o_ref[...] = o_ref[...] * inv * nfin_ref[...]

    hid = [slice(c * FF_CHUNK, (c + 1) * FF_CHUNK) for c in range(D_FF // FF_CHUNK)]
    out = [slice(c * MIX_CHUNK, (c + 1) * MIX_CHUNK) for c in range(D_MODEL // MIX_CHUNK)]
    return ([functools.partial(hidden, cs) for cs in hid]
            + [functools.partial(down, cs) for cs in out] + [final_norm])


def _tail_kernel(x_ref, k_ref, v_ref, nq_ref, wq_ref, wo_ref, nf_ref, wg_ref, wu_ref, wd_ref,
                 nfin_ref, o_ref, x2_scr, hf_scr, act_ref):
    i = pl.program_id(0)
    n_tiles = pl.num_programs(0) - 1
    slot_w = i % 2
    slot_r = 1 - slot_w

    def xattn():
        return _xattn_stages(x_ref, k_ref, v_ref, nq_ref, wq_ref, wo_ref, nf_ref,
                             x2_scr.at[slot_w], hf_scr.at[slot_w])

    def ffn():
        return _ffn_stages(x2_scr.at[slot_r], hf_scr.at[slot_r], wg_ref, wu_ref, wd_ref,
                           nfin_ref, act_ref, o_ref)

    def merge(lead, lag):
        if not lag:
            return lead
        n_hidden = D_FF // FF_CHUNK
        n_first = n_hidden - len(lead)
        return lag[:n_first] + _interleave(lag[n_first:n_hidden], lead) + lag[n_hidden:]

    _run_lookahead(i, n_tiles, xattn, ffn, merge)


def _tail(x2d, kext, vext, nq, wq, wo, nf, wg, wu, wd, nfin, seq):
    tm = TM_TAIL
    t = x2d.shape[0]
    n_tiles = t // tm
    tiles_per_seq = seq // tm
    cur = lambda i: jnp.minimum(i, n_tiles - 1)
    prv = lambda i: jnp.maximum(i - 1, 0)
    mem_spec = pl.BlockSpec((1, MEM_HEADS, N_MEM, MEM_WIDTH),
                            lambda i: (cur(i) // tiles_per_seq, 0, 0, 0))
    consts = (nq, wq, wo, nf, wg, wu, wd, nfin)
    return pl.pallas_call(
        _tail_kernel,
        grid=(n_tiles + 1,),
        in_specs=[pl.BlockSpec((tm, D_MODEL), lambda i: (cur(i), 0)), mem_spec, mem_spec]
        + [_const_spec(c.shape) for c in consts],
        out_specs=pl.BlockSpec((tm, D_MODEL), lambda i: (prv(i), 0)),
        out_shape=jax.ShapeDtypeStruct((t, D_MODEL), F32),
        scratch_shapes=[pltpu.VMEM((2, tm, D_MODEL), F32), pltpu.VMEM((2, tm, D_MODEL), BF16),
                        pltpu.VMEM((tm, D_FF), BF16)],
        compiler_params=_params(("arbitrary",)),
        name="tail",
    )(x2d, kext, vext, *consts)


def _layer(x2d, mem, batch, seq, norm_mix_w, w_in, b_gate, attn_sinks, gla_gate_w2, gla_gate_b,
           gla_norm_w, w_attn_o, w_gla_o, w_mix_o, norm_mem_q_w, norm_mem_kv_w, w_mem_q,
           w_mem_kv, w_mem_o, norm_ffn_w, w_ffn_gate, w_ffn_up, w_ffn_down, out_norm_w):
    w2_p = jnp.pad(gla_gate_w2, ((0, LANES - GLA_GATE_RANK), (0, 0)))
    bf = lambda a: a.astype(BF16)
    r2 = lambda a: a.reshape(1, -1)

    qa, ka, vat, qg, kg, vgt, gg, gk, gates, wa16, wgo16, wmo16 = _in_proj(
        x2d, r2(norm_mix_w), w_in.T, bf(w2_p), r2(gla_gate_b), r2(b_gate),
        r2(jnp.tile(gla_norm_w, GLA_HEADS)), w_attn_o, w_gla_o, w_mix_o)
    x1, wg16, wu16, wd16, wq16, wo16, wkv16 = _attn_mix(
        attn_sinks, qa, ka, vat, qg, kg, gk, vgt, gg, x2d, gates, wa16, wgo16, wmo16,
        (w_ffn_gate, w_ffn_up, w_ffn_down, w_mem_q, w_mem_o, w_mem_kv), seq)
    kext, vext = _mem_kv(mem, r2(norm_mem_kv_w), wkv16)
    return _tail(x1, kext, vext, r2(norm_mem_q_w), wq16, wo16, r2(norm_ffn_w),
                 wg16, wu16, wd16, r2(out_norm_w), seq)


def kernel(x, mem, norm_mix_w, w_in, b_gate, attn_sinks, gla_gate_w2, gla_gate_b, gla_norm_w,
           w_attn_o, w_gla_o, w_mix_o, norm_mem_q_w, norm_mem_kv_w, w_mem_q, w_mem_kv, w_mem_o,
           norm_ffn_w, w_ffn_gate, w_ffn_up, w_ffn_down, norm_final_w):
    batch, seq, d = x.shape
    depth = w_in.shape[0]
    assert depth == 1 and d == D_MODEL
    assert seq % TM_PROJ == 0 and seq % TM_ATTN == 0 and seq % TM_TAIL == 0
    out = _layer(x.reshape(batch * seq, d), mem, batch, seq, norm_mix_w[0], w_in[0], b_gate[0],
                 attn_sinks[0], gla_gate_w2[0], gla_gate_b[0], gla_norm_w[0], w_attn_o[0],
                 w_gla_o[0], w_mix_o[0], norm_mem_q_w[0], norm_mem_kv_w[0], w_mem_q[0],
                 w_mem_kv[0], w_mem_o[0], norm_ffn_w[0], w_ffn_gate[0], w_ffn_up[0],
                 w_ffn_down[0], norm_final_w)
    return out.reshape(batch, seq, d)
```

```python
import functools

import jax
import jax.numpy as jnp
from jax import lax
from jax.experimental import pallas as pl
from jax.experimental.pallas import tpu as pltpu

D_MODEL = 1024
CHUNK = 64
N_MEM = 256
EPS = 1e-6

SWA_HEADS = 16
SWA_KV_HEADS = 2
SWA_HEAD_DIM = 64
SWA_BLOCK = 128

GLA_HEADS = 4
GLA_KEY_DIM = D_MODEL // 2
GLA_VAL_DIM = D_MODEL
GLA_DK = GLA_KEY_DIM // GLA_HEADS
GLA_DV = GLA_VAL_DIM // GLA_HEADS
GLA_GATE_RANK = 16
GLA_GATE_NORM = 16.0

MEM_HEADS = 4
MEM_HEAD_DIM = 64
MEM_WIDTH = MEM_HEADS * MEM_HEAD_DIM

D_FF = -(-(8 * D_MODEL) // (3 * 256)) * 256

IN_SIZES = (SWA_HEADS * SWA_HEAD_DIM, SWA_KV_HEADS * SWA_HEAD_DIM, SWA_KV_HEADS * SWA_HEAD_DIM,
            GLA_KEY_DIM, GLA_KEY_DIM, GLA_VAL_DIM, GLA_VAL_DIM, GLA_GATE_RANK, 2 * D_MODEL)
IN_OFFSETS = tuple(sum(IN_SIZES[:i]) for i in range(len(IN_SIZES) + 1))

LANES = 128
VMEM_LIMIT = 56 * 1024 * 1024

LOG2E = 1.4426950408889634

TM_PROJ = 1024
PROJ_M = 512
PROJ_N = 256
TM_ATTN = 512
TM_TAIL = 512
MIX_CHUNK = 256
FF_CHUNK = 256

BF16 = jnp.bfloat16
F32 = jnp.float32


def _rms(x, w):
    return x * lax.rsqrt(jnp.mean(x * x, axis=-1, keepdims=True) + EPS) * w


def _dot(a, b):
    return jnp.dot(a, b, preferred_element_type=F32)


def _dot_nt(a, b):
    return lax.dot_general(a, b, (((1,), (1,)), ((), ())), preferred_element_type=F32)


def _const_spec(shape):
    zeros = (0,) * len(shape)
    return pl.BlockSpec(shape, lambda *_: zeros, pipeline_mode=pl.Buffered(1))


def _params(semantics):
    return pltpu.CompilerParams(dimension_semantics=semantics, vmem_limit_bytes=VMEM_LIMIT)


def _in_proj_kernel(n_cast, x_ref, nw_ref, wt_ref, w2_ref, gb_ref, bg_ref, gn_ref, *rest):
    cast_src, cast_dst = rest[:n_cast], rest[len(rest) - n_cast:]
    (qa_ref, ka_ref, vat_ref, qg_ref, kg_ref, vgt_ref, gg_ref, gk_ref,
     gate_ref) = rest[n_cast:len(rest) - n_cast]
    for src, dst in zip(cast_src, cast_dst):
        dst[...] = src[...].astype(BF16)
    o_qa, o_ka, o_va, o_qg, o_kg, o_vg, o_gg, o_alr, o_gate, o_end = IN_OFFSETS
    hd = SWA_HEAD_DIM
    tm = x_ref.shape[0]
    row_halves = [slice(m0, m0 + PROJ_M) for m0 in range(0, tm, PROJ_M)]
    h = {rows.start: _rms(x_ref[rows, :], nw_ref[...]).astype(BF16) for rows in row_halves}

    def project(rows, weight_ranges, n_cols, store):
        def piece(c0):
            w = [wt_ref[r0:r0 + n, :] for r0, n in weight_ranges(c0)]
            w = w[0] if len(w) == 1 else jnp.concatenate(w, axis=0)
            store(rows, c0, _dot_nt(h[rows.start], w))
        return [functools.partial(piece, c0) for c0 in range(0, n_cols, PROJ_N)]

    def project_t(rows, row0, n_rows, dst_ref):
        def piece(t0, r0, nr):
            dst_ref[r0:r0 + nr, rows.start + t0:rows.start + t0 + PROJ_N] = _dot_nt(
                wt_ref[row0 + r0:row0 + r0 + nr, :], h[rows.start][t0:t0 + PROJ_N]).astype(BF16)
        return [functools.partial(piece, t0, r0, min(PROJ_M, n_rows - r0))
                for t0 in range(0, PROJ_M, PROJ_N) for r0 in range(0, n_rows, PROJ_M)]

    def from_wt(row0):
        return lambda c0: [(row0 + c0, PROJ_N)]

    def plain(dst_ref):
        def store(rows, c0, y):
            dst_ref[rows, c0:c0 + PROJ_N] = y.astype(BF16)
        return store

    def store_gates(lo):
        def store(rows, c0, y):
            cs = slice(lo + c0, lo + c0 + PROJ_N)
            gate_ref[rows, cs] = jax.nn.sigmoid(y + bg_ref[:, cs]).astype(BF16)
        return store

    def store_q(rows, c0, y):
        qa_ref[rows, c0:c0 + PROJ_N] = (y * (hd ** -0.5 * LOG2E)).astype(BF16)

    def store_swish(rows, c0, y):
        gg_ref[rows, c0:c0 + PROJ_N] = (y * jax.nn.sigmoid(y)
                                        * gn_ref[:, c0:c0 + PROJ_N]).astype(BF16)

    a_lr = {}

    def store_keys(rows, c0, y):
        kw = SWA_KV_HEADS * hd
        ka_ref[rows, :] = y[:, :kw].astype(BF16)
        a_lr[rows.start] = y[:, kw:].astype(BF16)

    def log_decay(rows, zs):
        z = _dot(a_lr[rows.start], w2_ref[:, zs]) + gb_ref[:, zs]
        log_sig = jnp.minimum(z, 0.0) - jnp.log(1.0 + jnp.exp(-jnp.abs(z)))
        gk_ref[rows, zs] = (log_sig * (1.0 / GLA_GATE_NORM)).astype(BF16)

    group = SWA_HEADS // SWA_KV_HEADS

    def query_rows(c0):
        tiles = range(c0 // LANES, (c0 + PROJ_N) // LANES)
        return [(o_qa + (p + j * group) * hd, hd) for p in tiles for j in range(SWA_KV_HEADS)]

    key_alr_rows = lambda c0: [(o_ka, o_va - o_ka), (o_alr, LANES)]
    half = (o_end - o_gate) // 2
    pieces = []
    for rows in row_halves:
        heavy = (project(rows, from_wt(o_gate), half, store_gates(0))
                 + project(rows, from_wt(o_gate + half), half, store_gates(half))
                 + project(rows, from_wt(o_gg), o_alr - o_gg, store_swish))
        light = (project(rows, query_rows, o_ka - o_qa, store_q)
                 + project_t(rows, o_vg, o_gg - o_vg, vgt_ref)
                 + project(rows, from_wt(o_qg), o_kg - o_qg, plain(qg_ref)))
        decay = [functools.partial(log_decay, rows, slice(z0, z0 + PROJ_N))
                 for z0 in range(0, GLA_KEY_DIM, PROJ_N)]
        rest = (project(rows, from_wt(o_kg), o_vg - o_kg, plain(kg_ref))
                + project_t(rows, o_va, o_qg - o_va, vat_ref))
        pieces += (_interleave(heavy, light) + project(rows, key_alr_rows, PROJ_N, store_keys)
                   + _interleave(rest, decay))

    for emit in pieces:
        emit()


def _in_proj(x2d, nw, wt, w2, gb, bg, gn, w_attn_o, w_gla_o, w_mix_o):
    t = x2d.shape[0]
    tm = TM_PROJ
    steps = t // tm
    row = lambda n: pl.BlockSpec((tm, n), lambda i: (i, 0))
    col = lambda n: pl.BlockSpec((n, tm), lambda i: (0, i))
    consts = (nw, wt, w2, gb, bg, gn)
    kw = SWA_KV_HEADS * SWA_HEAD_DIM
    tok = lambda n: jax.ShapeDtypeStruct((t, n), BF16)
    out_shape = [
        tok(SWA_HEADS * SWA_HEAD_DIM), tok(kw), jax.ShapeDtypeStruct((kw, t), BF16),
        tok(GLA_KEY_DIM), tok(GLA_KEY_DIM), jax.ShapeDtypeStruct((GLA_VAL_DIM, t), BF16),
        tok(GLA_VAL_DIM), tok(GLA_KEY_DIM), tok(2 * D_MODEL),
    ]
    out_specs = [
        row(SWA_HEADS * SWA_HEAD_DIM), row(kw), col(kw),
        row(GLA_KEY_DIM), row(GLA_KEY_DIM), col(GLA_VAL_DIM),
        row(GLA_VAL_DIM), row(GLA_KEY_DIM), row(2 * D_MODEL),
    ]
    cast_w = (w_attn_o, w_gla_o, w_mix_o)
    slab_rows = D_MODEL // steps
    group = SWA_HEADS // SWA_KV_HEADS
    assert slab_rows == SWA_HEAD_DIM and steps == SWA_HEADS and slab_rows % 16 == 0
    slab = pl.BlockSpec((slab_rows, D_MODEL), lambda i: (i, 0))
    head_slab = pl.BlockSpec((slab_rows, D_MODEL),
                             lambda i: (i // SWA_KV_HEADS + group * (i % SWA_KV_HEADS), 0))
    return pl.pallas_call(
        functools.partial(_in_proj_kernel, len(cast_w)),
        grid=(steps,),
        in_specs=[row(D_MODEL)] + [_const_spec(c.shape) for c in consts]
        + [head_slab, slab, slab],
        out_specs=out_specs + [slab] * len(cast_w),
        out_shape=out_shape + [jax.ShapeDtypeStruct(w.shape, BF16) for w in cast_w],
        compiler_params=_params(("arbitrary",)),
        name="in_proj",
    )(x2d, *consts, *cast_w)


def _swa_stages(sink_ref, q_ref, k_ref, kp_ref, vt_ref, vtp_ref, seq_start, write):
    blk = SWA_BLOCK
    win = blk + CHUNK
    low = lax.broadcasted_iota(jnp.int32, (CHUNK, LANES), 1) < SWA_HEAD_DIM
    low_row = lax.broadcasted_iota(jnp.int32, (1, LANES), 1) < SWA_HEAD_DIM
    prev_bias = jnp.where(seq_start, -1e30, 0.0).astype(F32)
    k_all = jnp.concatenate([kp_ref[...], k_ref[...]], axis=0)
    vt_all = jnp.concatenate([vtp_ref[...], vt_ref[...]], axis=1)
    zeros_chunk = jnp.zeros((CHUNK, LANES), BF16)
    n_pairs = SWA_HEADS // SWA_KV_HEADS
    n_blocks = q_ref.shape[0] // blk
    n_ch = blk // CHUNK
    live = {}

    def scores(p):
        res = []
        for u in range(n_blocks):
            qsel = []
            for qh in range(n_ch):
                r0 = u * blk + qh * CHUNK
                qp = q_ref[r0:r0 + CHUNK, p * LANES:(p + 1) * LANES]
                zero = jnp.zeros_like(qp)
                qsel += [jnp.where(low, qp, zero), jnp.where(low, zero, qp)]
            res.append(_dot_nt(k_all[u * blk:(u + 2) * blk], jnp.concatenate(qsel, axis=0)))
        live["s", p] = res

    def softmax_values(p):
        sink_row = jnp.where(low_row, sink_ref[p], sink_ref[p + n_pairs]) * LOG2E
        res = []
        for u, s_blk in enumerate(live.pop(("s", p))):
            probs, inv = [], []
            for qh in range(n_ch):
                st = s_blk[qh * CHUNK:qh * CHUNK + win, qh * LANES:(qh + 1) * LANES]
                if u == 0:
                    n_prev = blk - qh * CHUNK
                    st = jnp.concatenate([st[:n_prev] + prev_bias, st[n_prev:]], axis=0)
                m = jnp.maximum(jnp.max(st, axis=0, keepdims=True), sink_row)
                e = jnp.exp2(st - m)
                inv.append(1.0 / (jnp.sum(e, axis=0, keepdims=True) + jnp.exp2(sink_row - m)))
                probs.append(jnp.concatenate([zeros_chunk] * qh + [e.astype(BF16)]
                                             + [zeros_chunk] * (n_ch - 1 - qh), axis=0))
            ot = _dot(vt_all[:, u * blk:(u + 2) * blk], jnp.concatenate(probs, axis=1))
            res.append([ot[:, qh * LANES:(qh + 1) * LANES] * inv[qh] for qh in range(n_ch)])
        live["o", p] = res

    def store(p):
        for u, ots in enumerate(live.pop(("o", p))):
            for qh, ot in enumerate(ots):
                r0 = u * blk + qh * CHUNK
                o2 = ot.T
                write(slice(r0, r0 + CHUNK), slice(p * LANES, (p + 1) * LANES),
                      jnp.where(low, o2[:CHUNK], o2[CHUNK:]).astype(BF16))

    return [(functools.partial(scores, p), functools.partial(softmax_values, p),
             functools.partial(store, p)) for p in range(n_pairs)]


def _gla_stages(q_ref, k_ref, gk_ref, vt_ref, g_ref, st_ref, seq_start, write):
    pair = 2 * CHUNK
    heads = range(GLA_HEADS)
    ks = [slice(h * GLA_DK, (h + 1) * GLA_DK) for h in heads]
    vs = [slice(h * GLA_DV, (h + 1) * GLA_DV) for h in heads]
    r = lax.broadcasted_iota(jnp.int32, (pair, pair), 0)
    c = lax.broadcasted_iota(jnp.int32, (pair, pair), 1)
    tri = ((c <= r) & ((r // CHUNK) == (c // CHUNK))).astype(BF16)
    first = lax.broadcasted_iota(jnp.int32, (pair, GLA_DK), 0) < CHUNK
    eps_scaled = EPS * GLA_DK
    n_pairs = q_ref.shape[0] // pair
    sts = [jnp.where(seq_start, jnp.zeros((GLA_DV, GLA_DK), F32), st_ref[h]) for h in heads]
    live = {}

    def decay(i):
        sl = slice(i * pair, (i + 1) * pair)
        b_all = _dot(tri, gk_ref[sl, :])
        res = []
        for h in heads:
            b = b_all[:, ks[h]]
            b_end0 = b[CHUNK - 1:CHUNK, :]
            b_end1 = b[pair - 1:pair, :]
            k_dec = (k_ref[sl, ks[h]].astype(F32)
                     * jnp.exp(jnp.where(first, b_end0, b_end1) - b)).astype(BF16)
            zero = jnp.zeros_like(k_dec)
            kd = jnp.concatenate([jnp.where(first, k_dec, zero), jnp.where(first, zero, k_dec)],
                                 axis=1)
            res.append((jnp.exp(b_end0), jnp.exp(b_end1), kd))
        live["d", i] = res

    def key_values(i):
        sl = slice(i * pair, (i + 1) * pair)
        live["kv", i] = [(a0, a1, _dot(vt_ref[vs[h], sl], kd))
                         for h, (a0, a1, kd) in zip(heads, live.pop(("d", i)))]

    def state_and_out(i):
        sl = slice(i * pair, (i + 1) * pair)
        dk = live.pop(("kv", i))
        outs = []
        for h in heads:
            a0, a1, kv = dk[h]
            st0 = sts[h] * a0 + kv[:, :GLA_DK]
            st1 = st0 * a1 + kv[:, GLA_DK:]
            sts[h] = st1
            q = q_ref[sl, ks[h]]
            outs.append(jnp.concatenate([_dot_nt(q[:CHUNK], st0.astype(BF16)),
                                         _dot_nt(q[CHUNK:], st1.astype(BF16))], axis=0))
        for h in heads:
            o = outs[h]
            inv = lax.rsqrt(jnp.mean(o * o, axis=-1, keepdims=True) + eps_scaled)
            write(sl, vs[h], (o * inv).astype(BF16) * g_ref[sl, vs[h]])

    def save_state():
        for h in heads:
            st_ref[h] = sts[h]

    stages = [functools.partial(decay, 0)]
    for i in range(n_pairs):
        if i + 1 < n_pairs:
            stages.append(functools.partial(decay, i + 1))
        stages.append(functools.partial(key_values, i))
        if i > 0:
            stages.append(functools.partial(state_and_out, i - 1))
    stages += [functools.partial(state_and_out, n_pairs - 1), save_state]
    return stages


def _mix_stages(x_ref, gate_ref, wa_ref, wg_ref, wm_ref, oa_ref, og_ref, mg_ref, o_ref):
    pieces = [slice(c * MIX_CHUNK, (c + 1) * MIX_CHUNK) for c in range(D_MODEL // MIX_CHUNK)]

    def merge(cs):
        ya = _dot(oa_ref[...], wa_ref[:, cs])
        yg = _dot(og_ref[...], wg_ref[:, cs])
        cs_b = slice(D_MODEL + cs.start, D_MODEL + cs.stop)
        mg_ref[:, cs] = (gate_ref[:, cs].astype(F32) * ya
                         + gate_ref[:, cs_b].astype(F32) * yg).astype(BF16)

    def project(cs):
        o_ref[:, cs] = x_ref[:, cs] + _dot(mg_ref[...], wm_ref[:, cs])

    return ([functools.partial(merge, cs) for cs in pieces]
            + [functools.partial(project, cs) for cs in pieces])


def _interleave(major, minor):
    out, done = [], 0
    for n, stage in enumerate(major):
        out.append(stage)
        want = (n + 1) * len(minor) // len(major)
        out.extend(minor[done:want])
        done = want
    return out


def _run_lookahead(i, n_tiles, lead, lag, merge, init=None):
    @pl.when(i == 0)
    def _():
        if init is not None:
            init()
        for stage in merge(lead(), []):
            stage()

    @pl.when(jnp.logical_and(i > 0, i < n_tiles))
    def _():
        for stage in merge(lead(), lag()):
            stage()

    @pl.when(i == n_tiles)
    def _():
        for stage in merge([], lag()):
            stage()


def _attn_mix_kernel(tiles_per_seq, n_cast,
                     sink_ref, q_ref, k_ref, kp_ref, vt_ref, vtp_ref,
                     qg_ref, kg_ref, gk_ref, vgt_ref, gg_ref,
                     x_ref, gate_ref, wa_ref, wg_ref, wm_ref, *rest):
    cast_src, (o_ref, *cast_dst) = rest[:n_cast], rest[n_cast:2 * n_cast + 1]
    oa_scr, og_scr, mg_scr, st_ref = rest[2 * n_cast + 1:]
    i = pl.program_id(0)
    n_tiles = pl.num_programs(0) - 1
    seq_start = (i % tiles_per_seq) == 0
    slot_w = i % 2
    slot_r = 1 - slot_w

    def write_oa(rows, cols, val):
        oa_scr[slot_w, rows, cols] = val

    def write_og(rows, cols, val):
        og_scr[slot_w, rows, cols] = val

    def mixers():
        swa = _swa_stages(sink_ref, q_ref, k_ref, kp_ref, vt_ref, vtp_ref, seq_start, write_oa)
        gla = _gla_stages(qg_ref, kg_ref, gk_ref, vgt_ref, gg_ref, st_ref, seq_start, write_og)
        n = len(swa)
        order = [swa[0][0]]
        for p in range(n):
            if p + 1 < n:
                order.append(swa[p + 1][0])
            order.append(swa[p][1])
            if p > 0:
                order.append(swa[p - 1][2])
        order.append(swa[n - 1][2])
        return _interleave(order, gla) + [convert_weights]

    def projections():
        return _mix_stages(x_ref, gate_ref, wa_ref, wg_ref, wm_ref,
                           oa_scr.at[slot_r], og_scr.at[slot_r], mg_scr, o_ref)

    def merge(lead, lag):
        return _interleave(lead, lag) if lead else lag

    def convert_weights():
        for src, dst in zip(cast_src, cast_dst):
            dst[...] = src[...].astype(BF16)

    def init():
        st_ref[...] = jnp.zeros_like(st_ref)

    _run_lookahead(i, n_tiles, mixers, projections, merge, init)


def _attn_mix(sinks, qa, ka, vat, qg, kg, gk, vgt, gg, x2d, gates, wa, wg, wm, cast_w, seq):
    t = x2d.shape[0]
    tm = TM_ATTN
    n_tiles = t // tm
    blocks = tm // SWA_BLOCK
    kw = ka.shape[1]
    cur = lambda i: jnp.minimum(i, n_tiles - 1)
    prv = lambda i: jnp.maximum(i - 1, 0)
    prev_block = lambda i: jnp.maximum(cur(i) * blocks - 1, 0)
    tok = lambda n: pl.BlockSpec((tm, n), lambda i: (cur(i), 0))
    col = lambda n: pl.BlockSpec((n, tm), lambda i: (0, cur(i)))
    old = lambda n: pl.BlockSpec((tm, n), lambda i: (prv(i), 0))

    def slab(w):
        n_slabs = max(n for n in range(1, n_tiles + 1) if w.shape[0] % (16 * n) == 0)
        return pl.BlockSpec((w.shape[0] // n_slabs, w.shape[1]),
                            lambda i: (jnp.minimum(i, n_slabs - 1), 0))

    cast_specs = [slab(w) for w in cast_w]
    return pl.pallas_call(
        functools.partial(_attn_mix_kernel, seq // tm, len(cast_w)),
        grid=(n_tiles + 1,),
        in_specs=[
            pl.BlockSpec(memory_space=pltpu.SMEM),
            tok(qa.shape[1]), tok(kw),
            pl.BlockSpec((SWA_BLOCK, kw), lambda i: (prev_block(i), 0)),
            col(kw),
            pl.BlockSpec((kw, SWA_BLOCK), lambda i: (0, prev_block(i))),
            tok(GLA_KEY_DIM), tok(GLA_KEY_DIM), tok(GLA_KEY_DIM), col(GLA_VAL_DIM),
            tok(GLA_VAL_DIM),
            old(D_MODEL), old(2 * D_MODEL),
            _const_spec(wa.shape), _const_spec(wg.shape), _const_spec(wm.shape),
        ] + cast_specs,
        out_specs=[old(D_MODEL)] + cast_specs,
        out_shape=[jax.ShapeDtypeStruct((t, D_MODEL), F32)]
        + [jax.ShapeDtypeStruct(w.shape, BF16) for w in cast_w],
        scratch_shapes=[
            pltpu.VMEM((2, tm, SWA_HEADS * SWA_HEAD_DIM), BF16),
            pltpu.VMEM((2, tm, GLA_VAL_DIM), BF16),
            pltpu.VMEM((tm, D_MODEL), BF16),
            pltpu.VMEM((GLA_HEADS, GLA_DV, GLA_DK), F32),
        ],
        compiler_params=_params(("arbitrary",)),
        name="attn_mix",
    )(sinks, qa, ka, ka, vat, vat, qg, kg, gk, vgt, gg, x2d, gates, wa, wg, wm, *cast_w)


def _mem_kv_kernel(m_ref, nw_ref, w_ref, k_ref, v_ref):
    mn = _rms(m_ref[0], nw_ref[...]).astype(BF16)
    kv = _dot(mn, w_ref[...])
    k = kv[:, :MEM_WIDTH].astype(BF16)
    v = kv[:, MEM_WIDTH:].astype(BF16)
    head = lax.broadcasted_iota(jnp.int32, (N_MEM, MEM_WIDTH), 1) // MEM_HEAD_DIM
    zero = jnp.zeros_like(k)
    for h in range(MEM_HEADS):
        k_ref[0, h] = jnp.where(head == h, k, zero)
        v_ref[0, h] = jnp.where(head == h, v, zero)


def _mem_kv(mem, nw, w):
    batch = mem.shape[0]
    out = jax.ShapeDtypeStruct((batch, MEM_HEADS, N_MEM, MEM_WIDTH), BF16)
    spec = pl.BlockSpec((1, MEM_HEADS, N_MEM, MEM_WIDTH), lambda b: (b, 0, 0, 0))
    return pl.pallas_call(
        _mem_kv_kernel,
        grid=(batch,),
        in_specs=[pl.BlockSpec((1, N_MEM, D_MODEL), lambda b: (b, 0, 0)),
                  _const_spec(nw.shape), _const_spec(w.shape)],
        out_specs=(spec, spec),
        out_shape=(out, out),
        compiler_params=_params(("arbitrary",)),
        name="mem_kv",
    )(mem, nw, w)


def _xattn_stages(x_ref, k_ref, v_ref, nq_ref, wq_ref, wo_ref, nf_ref, x2_ref, hf_ref):
    live = {}

    def queries():
        hq = _rms(x_ref[...], nq_ref[...]).astype(BF16)
        live["q"] = (_dot(hq, wq_ref[...]) * (MEM_HEAD_DIM ** -0.5)).astype(BF16)
        live["o"] = jnp.zeros((x_ref.shape[0], MEM_WIDTH), F32)

    def head(h):
        s = _dot_nt(live["q"], k_ref[0, h])
        e = jnp.exp(s - jnp.max(s, axis=-1, keepdims=True))
        inv = 1.0 / jnp.sum(e, axis=-1, keepdims=True)
        live["o"] = live["o"] + _dot(e.astype(BF16), v_ref[0, h]) * inv

    def project():
        o = live.pop("o").astype(BF16)
        for c in range(D_MODEL // MIX_CHUNK):
            cs = slice(c * MIX_CHUNK, (c + 1) * MIX_CHUNK)
            x2_ref[:, cs] = x_ref[:, cs] + _dot(o, wo_ref[:, cs])
        hf_ref[...] = _rms(x2_ref[...], nf_ref[...]).astype(BF16)

    return [queries] + [functools.partial(head, h) for h in range(MEM_HEADS)] + [project]


def _ffn_stages(x2_ref, hf_ref, wg_ref, wu_ref, wd_ref, nfin_ref, act_ref, o_ref):
    live = {"ss": jnp.zeros((o_ref.shape[0], 1), F32)}

    def hidden(cs):
        hf = hf_ref[...]
        g = _dot(hf, wg_ref[:, cs])
        u = _dot(hf, wu_ref[:, cs])
        act_ref[:, cs] = (g * jax.nn.sigmoid(g) * u).astype(BF16)

    def down(cs):
        y = x2_ref[:, cs] + _dot(act_ref[...], wd_ref[:, cs])
        live["ss"] = live["ss"] + jnp.sum(y * y, axis=-1, keepdims=True)
        o_ref[:, cs] = y

    def final_norm():
        inv = lax.rsqrt(live["ss"] * (1.0 / D_MODEL) + EPS)
        o_ref[...] = o_ref[...] * inv * nfin_ref[...]

    hid = [slice(c * FF_CHUNK, (c + 1) * FF_CHUNK) for c in range(D_FF // FF_CHUNK)]
    out = [slice(c * MIX_CHUNK, (c + 1) * MIX_CHUNK) for c in range(D_MODEL // MIX_CHUNK)]
    return ([functools.partial(hidden, cs) for cs in hid]
            + [functools.partial(down, cs) for cs in out] + [final_norm])


def _tail_kernel(x_ref, k_ref, v_ref, nq_ref, wq_ref, wo_ref, nf_ref, wg_ref, wu_ref, wd_ref,
                 nfin_ref, o_ref, x2_scr, hf_scr, act_ref):
    i = pl.program_id(0)
    n_tiles = pl.num_programs(0) - 1
    slot_w = i % 2
    slot_r = 1 - slot_w

    def xattn():
        return _xattn_stages(x_ref, k_ref, v_ref, nq_ref, wq_ref, wo_ref, nf_ref,
                             x2_scr.at[slot_w], hf_scr.at[slot_w])

    def ffn():
        return _ffn_stages(x2_scr.at[slot_r], hf_scr.at[slot_r], wg_ref, wu_ref, wd_ref,
                           nfin_ref, act_ref, o_ref)

    def merge(lead, lag):
        if not lag:
            return lead
        n_hidden = D_FF // FF_CHUNK
        n_first = n_hidden - len(lead)
        return lag[:n_first] + _interleave(lag[n_first:n_hidden], lead) + lag[n_hidden:]

    _run_lookahead(i, n_tiles, xattn, ffn, merge)


def _tail(x2d, kext, vext, nq, wq, wo, nf, wg, wu, wd, nfin, seq):
    tm = TM_TAIL
    t = x2d.shape[0]
    n_tiles = t // tm
    tiles_per_seq = seq // tm
    cur = lambda i: jnp.minimum(i, n_tiles - 1)
    prv = lambda i: jnp.maximum(i - 1, 0)
    mem_spec = pl.BlockSpec((1, MEM_HEADS, N_MEM, MEM_WIDTH),
                            lambda i: (cur(i) // tiles_per_seq, 0, 0, 0))
    consts = (nq, wq, wo, nf, wg, wu, wd, nfin)
    return pl.pallas_call(
        _tail_kernel,
        grid=(n_tiles + 1,),
        in_specs=[pl.BlockSpec((tm, D_MODEL), lambda i: (cur(i), 0)), mem_spec, mem_spec]
        + [_const_spec(c.shape) for c in consts],
        out_specs=pl.BlockSpec((tm, D_MODEL), lambda i: (prv(i), 0)),
        out_shape=jax.ShapeDtypeStruct((t, D_MODEL), F32),
        scratch_shapes=[pltpu.VMEM((2, tm, D_MODEL), F32), pltpu.VMEM((2, tm, D_MODEL), BF16),
                        pltpu.VMEM((tm, D_FF), BF16)],
        compiler_params=_params(("arbitrary",)),
        name="tail",
    )(x2d, kext, vext, *consts)


def _layer(x2d, mem, batch, seq, norm_mix_w, w_in, b_gate, attn_sinks, gla_gate_w2, gla_gate_b,
           gla_norm_w, w_attn_o, w_gla_o, w_mix_o, norm_mem_q_w, norm_mem_kv_w, w_mem_q,
           w_mem_kv, w_mem_o, norm_ffn_w, w_ffn_gate, w_ffn_up, w_ffn_down, out_norm_w):
    w2_p = jnp.pad(gla_gate_w2, ((0, LANES - GLA_GATE_RANK), (0, 0)))
    bf = lambda a: a.astype(BF16)
    r2 = lambda a: a.reshape(1, -1)

    qa, ka, vat, qg, kg, vgt, gg, gk, gates, wa16, wgo16, wmo16 = _in_proj(
        x2d, r2(norm_mix_w), bf(w_in.T), bf(w2_p), r2(gla_gate_b), r2(b_gate),
        r2(jnp.tile(gla_norm_w, GLA_HEADS)), w_attn_o, w_gla_o, w_mix_o)
    x1, wg16, wu16, wd16, wq16, wo16, wkv16 = _attn_mix(
        attn_sinks, qa, ka, vat, qg, kg, gk, vgt, gg, x2d, gates, wa16, wgo16, wmo16,
        (w_ffn_gate, w_ffn_up, w_ffn_down, w_mem_q, w_mem_o, w_mem_kv), seq)
    kext, vext = _mem_kv(mem, r2(norm_mem_kv_w), wkv16)
    return _tail(x1, kext, vext, r2(norm_mem_q_w), wq16, wo16, r2(norm_ffn_w),
                 wg16, wu16, wd16, r2(out_norm_w), seq)


def kernel(x, mem, norm_mix_w, w_in, b_gate, attn_sinks, gla_gate_w2, gla_gate_b, gla_norm_w,
           w_attn_o, w_gla_o, w_mix_o, norm_mem_q_w, norm_mem_kv_w, w_mem_q, w_mem_kv, w_mem_o,
           norm_ffn_w, w_ffn_gate, w_ffn_up, w_ffn_down, norm_final_w):
    batch, seq, d = x.shape
    depth = w_in.shape[0]
    assert depth == 1 and d == D_MODEL
    assert seq % TM_PROJ == 0 and seq % TM_ATTN == 0 and seq % TM_TAIL == 0
    out = _layer(x.reshape(batch * seq, d), mem, batch, seq, norm_mix_w[0], w_in[0], b_gate[0],
                 attn_sinks[0], gla_gate_w2[0], gla_gate_b[0], gla_norm_w[0], w_attn_o[0],
                 w_gla_o[0], w_mix_o[0], norm_mem_q_w[0], norm_mem_kv_w[0], w_mem_q[0],
                 w_mem_kv[0], w_mem_o[0], norm_ffn_w[0], w_ffn_gate[0], w_ffn_up[0],
                 w_ffn_down[0], norm_final_w)
    return out.reshape(batch, seq, d)
```

```python
import functools

import jax
import jax.numpy as jnp
from jax import lax
from jax.experimental import pallas as pl
from jax.experimental.pallas import tpu as pltpu

D_MODEL = 1024
CHUNK = 64
N_MEM = 256
EPS = 1e-6

SWA_HEADS = 16
SWA_KV_HEADS = 2
SWA_HEAD_DIM = 64
SWA_BLOCK = 128

GLA_HEADS = 4
GLA_KEY_DIM = D_MODEL // 2
GLA_VAL_DIM = D_MODEL
GLA_DK = GLA_KEY_DIM // GLA_HEADS
GLA_DV = GLA_VAL_DIM // GLA_HEADS
GLA_GATE_RANK = 16
GLA_GATE_NORM = 16.0

MEM_HEADS = 4
MEM_HEAD_DIM = 64
MEM_WIDTH = MEM_HEADS * MEM_HEAD_DIM

D_FF = -(-(8 * D_MODEL) // (3 * 256)) * 256

IN_SIZES = (SWA_HEADS * SWA_HEAD_DIM, SWA_KV_HEADS * SWA_HEAD_DIM, SWA_KV_HEADS * SWA_HEAD_DIM,
            GLA_KEY_DIM, GLA_KEY_DIM, GLA_VAL_DIM, GLA_VAL_DIM, GLA_GATE_RANK, 2 * D_MODEL)
IN_OFFSETS = tuple(sum(IN_SIZES[:i]) for i in range(len(IN_SIZES) + 1))

LANES = 128
VMEM_LIMIT = 56 * 1024 * 1024

LOG2E = 1.4426950408889634

TM_PROJ = 1024
PROJ_M = 512
PROJ_N = 256
TM_ATTN = 512
TM_TAIL = 512
MIX_CHUNK = 256
FF_CHUNK = 256

BF16 = jnp.bfloat16
F32 = jnp.float32


def _rms(x, w):
    return x * lax.rsqrt(jnp.mean(x * x, axis=-1, keepdims=True) + EPS) * w


def _dot(a, b):
    return jnp.dot(a, b, preferred_element_type=F32)


def _dot_nt(a, b):
    return lax.dot_general(a, b, (((1,), (1,)), ((), ())), preferred_element_type=F32)


def _const_spec(shape):
    zeros = (0,) * len(shape)
    return pl.BlockSpec(shape, lambda *_: zeros, pipeline_mode=pl.Buffered(1))


def _params(semantics):
    return pltpu.CompilerParams(dimension_semantics=semantics, vmem_limit_bytes=VMEM_LIMIT)


def _in_proj_kernel(n_cast, x_ref, nw_ref, wt_ref, w2_ref, gb_ref, bg_ref, gn_ref, *rest):
    cast_src, cast_dst = rest[:n_cast], rest[len(rest) - n_cast:]
    (qa_ref, ka_ref, vat_ref, qg_ref, kg_ref, vgt_ref, gg_ref, gk_ref,
     gate_ref) = rest[n_cast:len(rest) - n_cast]
    for src, dst in zip(cast_src, cast_dst):
        dst[...] = src[...].astype(BF16)
    o_qa, o_ka, o_va, o_qg, o_kg, o_vg, o_gg, o_alr, o_gate, o_end = IN_OFFSETS
    hd = SWA_HEAD_DIM
    tm = x_ref.shape[0]
    row_halves = [slice(m0, m0 + PROJ_M) for m0 in range(0, tm, PROJ_M)]
    h = {rows.start: _rms(x_ref[rows, :], nw_ref[...]).astype(BF16) for rows in row_halves}

    def project(rows, weight_rows, n_cols, store):
        def piece(c0):
            store(rows, c0, _dot_nt(h[rows.start], weight_rows(c0)))
        return [functools.partial(piece, c0) for c0 in range(0, n_cols, PROJ_N)]

    def project_t(rows, row0, n_rows, dst_ref):
        def piece(t0, r0, nr):
            dst_ref[r0:r0 + nr, rows.start + t0:rows.start + t0 + PROJ_N] = _dot_nt(
                wt_ref[row0 + r0:row0 + r0 + nr, :], h[rows.start][t0:t0 + PROJ_N]).astype(BF16)
        return [functools.partial(piece, t0, r0, min(PROJ_M, n_rows - r0))
                for t0 in range(0, PROJ_M, PROJ_N) for r0 in range(0, n_rows, PROJ_M)]

    def from_wt(row0):
        return lambda c0: wt_ref[row0 + c0:row0 + c0 + PROJ_N, :]

    def plain(dst_ref):
        def store(rows, c0, y):
            dst_ref[rows, c0:c0 + PROJ_N] = y.astype(BF16)
        return store

    def store_gates(lo):
        def store(rows, c0, y):
            cs = slice(lo + c0, lo + c0 + PROJ_N)
            gate_ref[rows, cs] = jax.nn.sigmoid(y + bg_ref[:, cs]).astype(BF16)
        return store

    def store_q(rows, c0, y):
        qa_ref[rows, c0:c0 + PROJ_N] = (y * (hd ** -0.5 * LOG2E)).astype(BF16)

    def store_swish(rows, c0, y):
        gg_ref[rows, c0:c0 + PROJ_N] = (y * jax.nn.sigmoid(y)
                                        * gn_ref[:, c0:c0 + PROJ_N]).astype(BF16)

    a_lr = {}

    def store_keys(rows, c0, y):
        kw = SWA_KV_HEADS * hd
        ka_ref[rows, :] = y[:, :kw].astype(BF16)
        a_lr[rows.start] = y[:, kw:].astype(BF16)

    def log_decay(rows, zs):
        z = _dot(a_lr[rows.start], w2_ref[:, zs]) + gb_ref[:, zs]
        log_sig = jnp.minimum(z, 0.0) - jnp.log(1.0 + jnp.exp(-jnp.abs(z)))
        gk_ref[rows, zs] = (log_sig * (1.0 / GLA_GATE_NORM)).astype(BF16)

    group = SWA_HEADS // SWA_KV_HEADS
    wq = jnp.concatenate(
        [wt_ref[o_qa + (p + j * group) * hd:o_qa + (p + j * group + 1) * hd, :]
         for p in range(group) for j in range(SWA_KV_HEADS)], axis=0)
    w_k_alr = jnp.concatenate([wt_ref[o_ka:o_va, :], wt_ref[o_alr:o_alr + LANES, :]], axis=0)
    half = (o_end - o_gate) // 2
    def both_halves(pieces_of):
        return [piece for same in zip(*[pieces_of(rows) for rows in row_halves]) for piece in same]

    heavy = both_halves(lambda rows: (
        project(rows, from_wt(o_gate), half, store_gates(0))
        + project(rows, from_wt(o_gate + half), half, store_gates(half))
        + project(rows, from_wt(o_gg), o_alr - o_gg, store_swish)))
    light = both_halves(lambda rows: (
        project(rows, lambda c0: wq[c0:c0 + PROJ_N], o_ka - o_qa, store_q)
        + project_t(rows, o_vg, o_gg - o_vg, vgt_ref)
        + project(rows, from_wt(o_qg), o_kg - o_qg, plain(qg_ref))))
    keys = both_halves(lambda rows: project(rows, lambda c0: w_k_alr, PROJ_N, store_keys))
    decay = both_halves(lambda rows: [functools.partial(log_decay, rows, slice(z0, z0 + PROJ_N))
                                      for z0 in range(0, GLA_KEY_DIM, PROJ_N)])
    rest = both_halves(lambda rows: (
        project(rows, from_wt(o_kg), o_vg - o_kg, plain(kg_ref))
        + project_t(rows, o_va, o_qg - o_va, vat_ref)))
    for piece in _interleave(heavy, light) + keys + _interleave(rest, decay):
        piece()


def _in_proj(x2d, nw, wt, w2, gb, bg, gn, w_attn_o, w_gla_o, w_mix_o):
    t = x2d.shape[0]
    tm = TM_PROJ
    steps = t // tm
    row = lambda n: pl.BlockSpec((tm, n), lambda i: (i, 0))
    col = lambda n: pl.BlockSpec((n, tm), lambda i: (0, i))
    consts = (nw, wt, w2, gb, bg, gn)
    kw = SWA_KV_HEADS * SWA_HEAD_DIM
    tok = lambda n: jax.ShapeDtypeStruct((t, n), BF16)
    out_shape = [
        tok(SWA_HEADS * SWA_HEAD_DIM), tok(kw), jax.ShapeDtypeStruct((kw, t), BF16),
        tok(GLA_KEY_DIM), tok(GLA_KEY_DIM), jax.ShapeDtypeStruct((GLA_VAL_DIM, t), BF16),
        tok(GLA_VAL_DIM), tok(GLA_KEY_DIM), tok(2 * D_MODEL),
    ]
    out_specs = [
        row(SWA_HEADS * SWA_HEAD_DIM), row(kw), col(kw),
        row(GLA_KEY_DIM), row(GLA_KEY_DIM), col(GLA_VAL_DIM),
        row(GLA_VAL_DIM), row(GLA_KEY_DIM), row(2 * D_MODEL),
    ]
    cast_w = (w_attn_o, w_gla_o, w_mix_o)
    slab_rows = D_MODEL // steps
    group = SWA_HEADS // SWA_KV_HEADS
    assert slab_rows == SWA_HEAD_DIM and steps == SWA_HEADS and slab_rows % 16 == 0
    slab = pl.BlockSpec((slab_rows, D_MODEL), lambda i: (i, 0))
    head_slab = pl.BlockSpec((slab_rows, D_MODEL),
                             lambda i: (i // SWA_KV_HEADS + group * (i % SWA_KV_HEADS), 0))
    return pl.pallas_call(
        functools.partial(_in_proj_kernel, len(cast_w)),
        grid=(steps,),
        in_specs=[row(D_MODEL)] + [_const_spec(c.shape) for c in consts]
        + [head_slab, slab, slab],
        out_specs=out_specs + [slab] * len(cast_w),
        out_shape=out_shape + [jax.ShapeDtypeStruct(w.shape, BF16) for w in cast_w],
        compiler_params=_params(("arbitrary",)),
        name="in_proj",
    )(x2d, *consts, *cast_w)


def _swa_stages(sink_ref, q_ref, k_ref, kp_ref, vt_ref, vtp_ref, seq_start, write):
    blk = SWA_BLOCK
    win = blk + CHUNK
    low = lax.broadcasted_iota(jnp.int32, (CHUNK, LANES), 1) < SWA_HEAD_DIM
    low_row = lax.broadcasted_iota(jnp.int32, (1, LANES), 1) < SWA_HEAD_DIM
    prev_bias = jnp.where(seq_start, -1e30, 0.0).astype(F32)
    k_all = jnp.concatenate([kp_ref[...], k_ref[...]], axis=0)
    vt_all = jnp.concatenate([vtp_ref[...], vt_ref[...]], axis=1)
    zeros_chunk = jnp.zeros((CHUNK, LANES), BF16)
    n_pairs = SWA_HEADS // SWA_KV_HEADS
    n_blocks = q_ref.shape[0] // blk
    n_ch = blk // CHUNK
    live = {}

    def scores(p):
        res = []
        for u in range(n_blocks):
            qsel = []
            for qh in range(n_ch):
                r0 = u * blk + qh * CHUNK
                qp = q_ref[r0:r0 + CHUNK, p * LANES:(p + 1) * LANES]
                zero = jnp.zeros_like(qp)
                qsel += [jnp.where(low, qp, zero), jnp.where(low, zero, qp)]
            res.append(_dot_nt(k_all[u * blk:(u + 2) * blk], jnp.concatenate(qsel, axis=0)))
        live["s", p] = res

    def softmax_values(p):
        sink_row = jnp.where(low_row, sink_ref[p], sink_ref[p + n_pairs]) * LOG2E
        res = []
        for u, s_blk in enumerate(live.pop(("s", p))):
            probs, inv = [], []
            for qh in range(n_ch):
                st = s_blk[qh * CHUNK:qh * CHUNK + win, qh * LANES:(qh + 1) * LANES]
                if u == 0:
                    n_prev = blk - qh * CHUNK
                    st = jnp.concatenate([st[:n_prev] + prev_bias, st[n_prev:]], axis=0)
                m = jnp.maximum(jnp.max(st, axis=0, keepdims=True), sink_row)
                e = jnp.exp2(st - m)
                inv.append(1.0 / (jnp.sum(e, axis=0, keepdims=True) + jnp.exp2(sink_row - m)))
                probs.append(jnp.concatenate([zeros_chunk] * qh + [e.astype(BF16)]
                                             + [zeros_chunk] * (n_ch - 1 - qh), axis=0))
            ot = _dot(vt_all[:, u * blk:(u + 2) * blk], jnp.concatenate(probs, axis=1))
            res.append([ot[:, qh * LANES:(qh + 1) * LANES] * inv[qh] for qh in range(n_ch)])
        live["o", p] = res

    def store(p):
        for u, ots in enumerate(live.pop(("o", p))):
            for qh, ot in enumerate(ots):
                r0 = u * blk + qh * CHUNK
                o2 = ot.T
                write(slice(r0, r0 + CHUNK), slice(p * LANES, (p + 1) * LANES),
                      jnp.where(low, o2[:CHUNK], o2[CHUNK:]).astype(BF16))

    return [(functools.partial(scores, p), functools.partial(softmax_values, p),
             functools.partial(store, p)) for p in range(n_pairs)]


def _gla_stages(q_ref, k_ref, gk_ref, vt_ref, g_ref, st_ref, seq_start, write):
    pair = 2 * CHUNK
    heads = range(GLA_HEADS)
    ks = [slice(h * GLA_DK, (h + 1) * GLA_DK) for h in heads]
    vs = [slice(h * GLA_DV, (h + 1) * GLA_DV) for h in heads]
    r = lax.broadcasted_iota(jnp.int32, (pair, pair), 0)
    c = lax.broadcasted_iota(jnp.int32, (pair, pair), 1)
    tri = ((c <= r) & ((r // CHUNK) == (c // CHUNK))).astype(BF16)
    first = lax.broadcasted_iota(jnp.int32, (pair, GLA_DK), 0) < CHUNK
    eps_scaled = EPS * GLA_DK
    n_pairs = q_ref.shape[0] // pair
    sts = [jnp.where(seq_start, jnp.zeros((GLA_DV, GLA_DK), F32), st_ref[h]) for h in heads]
    live = {}

    def decay(i):
        sl = slice(i * pair, (i + 1) * pair)
        b_all = _dot(tri, gk_ref[sl, :])
        res = []
        for h in heads:
            b = b_all[:, ks[h]]
            b_end0 = b[CHUNK - 1:CHUNK, :]
            b_end1 = b[pair - 1:pair, :]
            k_dec = (k_ref[sl, ks[h]].astype(F32)
                     * jnp.exp(jnp.where(first, b_end0, b_end1) - b)).astype(BF16)
            zero = jnp.zeros_like(k_dec)
            kd = jnp.concatenate([jnp.where(first, k_dec, zero), jnp.where(first, zero, k_dec)],
                                 axis=1)
            res.append((jnp.exp(b_end0), jnp.exp(b_end1), kd))
        live["d", i] = res

    def key_values(i):
        sl = slice(i * pair, (i + 1) * pair)
        live["kv", i] = [(a0, a1, _dot(vt_ref[vs[h], sl], kd))
                         for h, (a0, a1, kd) in zip(heads, live.pop(("d", i)))]

    def state_and_out(i):
        sl = slice(i * pair, (i + 1) * pair)
        dk = live.pop(("kv", i))
        outs = []
        for h in heads:
            a0, a1, kv = dk[h]
            st0 = sts[h] * a0 + kv[:, :GLA_DK]
            st1 = st0 * a1 + kv[:, GLA_DK:]
            sts[h] = st1
            q = q_ref[sl, ks[h]]
            outs.append(jnp.concatenate([_dot_nt(q[:CHUNK], st0.astype(BF16)),
                                         _dot_nt(q[CHUNK:], st1.astype(BF16))], axis=0))
        for h in heads:
            o = outs[h]
            inv = lax.rsqrt(jnp.mean(o * o, axis=-1, keepdims=True) + eps_scaled)
            write(sl, vs[h], (o * inv).astype(BF16) * g_ref[sl, vs[h]])

    def save_state():
        for h in heads:
            st_ref[h] = sts[h]

    stages = [functools.partial(decay, 0)]
    for i in range(n_pairs):
        if i + 1 < n_pairs:
            stages.append(functools.partial(decay, i + 1))
        stages.append(functools.partial(key_values, i))
        if i > 0:
            stages.append(functools.partial(state_and_out, i - 1))
    stages += [functools.partial(state_and_out, n_pairs - 1), save_state]
    return stages


def _mix_stages(x_ref, gate_ref, wa_ref, wg_ref, wm_ref, oa_ref, og_ref, mg_ref, o_ref):
    pieces = [slice(c * MIX_CHUNK, (c + 1) * MIX_CHUNK) for c in range(D_MODEL // MIX_CHUNK)]

    def merge(cs):
        ya = _dot(oa_ref[...], wa_ref[:, cs])
        yg = _dot(og_ref[...], wg_ref[:, cs])
        cs_b = slice(D_MODEL + cs.start, D_MODEL + cs.stop)
        mg_ref[:, cs] = (gate_ref[:, cs].astype(F32) * ya
                         + gate_ref[:, cs_b].astype(F32) * yg).astype(BF16)

    def project(cs):
        o_ref[:, cs] = x_ref[:, cs] + _dot(mg_ref[...], wm_ref[:, cs])

    return ([functools.partial(merge, cs) for cs in pieces]
            + [functools.partial(project, cs) for cs in pieces])


def _interleave(major, minor):
    out, done = [], 0
    for n, stage in enumerate(major):
        out.append(stage)
        want = (n + 1) * len(minor) // len(major)
        out.extend(minor[done:want])
        done = want
    return out


def _run_lookahead(i, n_tiles, lead, lag, merge, init=None):
    @pl.when(i == 0)
    def _():
        if init is not None:
            init()
        for stage in merge(lead(), []):
            stage()

    @pl.when(jnp.logical_and(i > 0, i < n_tiles))
    def _():
        for stage in merge(lead(), lag()):
            stage()

    @pl.when(i == n_tiles)
    def _():
        for stage in merge([], lag()):
            stage()


def _attn_mix_kernel(tiles_per_seq, n_cast,
                     sink_ref, q_ref, k_ref, kp_ref, vt_ref, vtp_ref,
                     qg_ref, kg_ref, gk_ref, vgt_ref, gg_ref,
                     x_ref, gate_ref, wa_ref, wg_ref, wm_ref, *rest):
    cast_src, (o_ref, *cast_dst) = rest[:n_cast], rest[n_cast:2 * n_cast + 1]
    oa_scr, og_scr, mg_scr, st_ref = rest[2 * n_cast + 1:]
    i = pl.program_id(0)
    n_tiles = pl.num_programs(0) - 1
    seq_start = (i % tiles_per_seq) == 0
    slot_w = i % 2
    slot_r = 1 - slot_w

    def write_oa(rows, cols, val):
        oa_scr[slot_w, rows, cols] = val

    def write_og(rows, cols, val):
        og_scr[slot_w, rows, cols] = val

    def mixers():
        swa = _swa_stages(sink_ref, q_ref, k_ref, kp_ref, vt_ref, vtp_ref, seq_start, write_oa)
        gla = _gla_stages(qg_ref, kg_ref, gk_ref, vgt_ref, gg_ref, st_ref, seq_start, write_og)
        n = len(swa)
        order = [swa[0][0]]
        for p in range(n):
            if p + 1 < n:
                order.append(swa[p + 1][0])
            order.append(swa[p][1])
            if p > 0:
                order.append(swa[p - 1][2])
        order.append(swa[n - 1][2])
        return _interleave(order, gla) + [convert_weights]

    def projections():
        return _mix_stages(x_ref, gate_ref, wa_ref, wg_ref, wm_ref,
                           oa_scr.at[slot_r], og_scr.at[slot_r], mg_scr, o_ref)

    def merge(lead, lag):
        return _interleave(lead, lag) if lead else lag

    def convert_weights():
        for src, dst in zip(cast_src, cast_dst):
            dst[...] = src[...].astype(BF16)

    def init():
        st_ref[...] = jnp.zeros_like(st_ref)

    _run_lookahead(i, n_tiles, mixers, projections, merge, init)


def _attn_mix(sinks, qa, ka, vat, qg, kg, gk, vgt, gg, x2d, gates, wa, wg, wm, cast_w, seq):
    t = x2d.shape[0]
    tm = TM_ATTN
    n_tiles = t // tm
    blocks = tm // SWA_BLOCK
    kw = ka.shape[1]
    cur = lambda i: jnp.minimum(i, n_tiles - 1)
    prv = lambda i: jnp.maximum(i - 1, 0)
    prev_block = lambda i: jnp.maximum(cur(i) * blocks - 1, 0)
    tok = lambda n: pl.BlockSpec((tm, n), lambda i: (cur(i), 0))
    col = lambda n: pl.BlockSpec((n, tm), lambda i: (0, cur(i)))
    old = lambda n: pl.BlockSpec((tm, n), lambda i: (prv(i), 0))

    def slab(w):
        n_slabs = max(n for n in range(1, n_tiles + 1) if w.shape[0] % (16 * n) == 0)
        return pl.BlockSpec((w.shape[0] // n_slabs, w.shape[1]),
                            lambda i: (jnp.minimum(i, n_slabs - 1), 0))

    cast_specs = [slab(w) for w in cast_w]
    return pl.pallas_call(
        functools.partial(_attn_mix_kernel, seq // tm, len(cast_w)),
        grid=(n_tiles + 1,),
        in_specs=[
            pl.BlockSpec(memory_space=pltpu.SMEM),
            tok(qa.shape[1]), tok(kw),
            pl.BlockSpec((SWA_BLOCK, kw), lambda i: (prev_block(i), 0)),
            col(kw),
            pl.BlockSpec((kw, SWA_BLOCK), lambda i: (0, prev_block(i))),
            tok(GLA_KEY_DIM), tok(GLA_KEY_DIM), tok(GLA_KEY_DIM), col(GLA_VAL_DIM),
            tok(GLA_VAL_DIM),
            old(D_MODEL), old(2 * D_MODEL),
            _const_spec(wa.shape), _const_spec(wg.shape), _const_spec(wm.shape),
        ] + cast_specs,
        out_specs=[old(D_MODEL)] + cast_specs,
        out_shape=[jax.ShapeDtypeStruct((t, D_MODEL), F32)]
        + [jax.ShapeDtypeStruct(w.shape, BF16) for w in cast_w],
        scratch_shapes=[
            pltpu.VMEM((2, tm, SWA_HEADS * SWA_HEAD_DIM), BF16),
            pltpu.VMEM((2, tm, GLA_VAL_DIM), BF16),
            pltpu.VMEM((tm, D_MODEL), BF16),
            pltpu.VMEM((GLA_HEADS, GLA_DV, GLA_DK), F32),
        ],
        compiler_params=_params(("arbitrary",)),
        name="attn_mix",
    )(sinks, qa, ka, ka, vat, vat, qg, kg, gk, vgt, gg, x2d, gates, wa, wg, wm, *cast_w)


def _mem_kv_kernel(m_ref, nw_ref, w_ref, k_ref, v_ref):
    mn = _rms(m_ref[0], nw_ref[...]).astype(BF16)
    kv = _dot(mn, w_ref[...])
    k = kv[:, :MEM_WIDTH].astype(BF16)
    v = kv[:, MEM_WIDTH:].astype(BF16)
    head = lax.broadcasted_iota(jnp.int32, (N_MEM, MEM_WIDTH), 1) // MEM_HEAD_DIM
    zero = jnp.zeros_like(k)
    for h in range(MEM_HEADS):
        k_ref[0, h] = jnp.where(head == h, k, zero)
        v_ref[0, h] = jnp.where(head == h, v, zero)


def _mem_kv(mem, nw, w):
    batch = mem.shape[0]
    out = jax.ShapeDtypeStruct((batch, MEM_HEADS, N_MEM, MEM_WIDTH), BF16)
    spec = pl.BlockSpec((1, MEM_HEADS, N_MEM, MEM_WIDTH), lambda b: (b, 0, 0, 0))
    return pl.pallas_call(
        _mem_kv_kernel,
        grid=(batch,),
        in_specs=[pl.BlockSpec((1, N_MEM, D_MODEL), lambda b: (b, 0, 0)),
                  _const_spec(nw.shape), _const_spec(w.shape)],
        out_specs=(spec, spec),
        out_shape=(out, out),
        compiler_params=_params(("arbitrary",)),
        name="mem_kv",
    )(mem, nw, w)


def _xattn_stages(x_ref, k_ref, v_ref, nq_ref, wq_ref, wo_ref, nf_ref, x2_ref, hf_ref):
    live = {}

    def queries():
        hq = _rms(x_ref[...], nq_ref[...]).astype(BF16)
        live["q"] = (_dot(hq, wq_ref[...]) * (MEM_HEAD_DIM ** -0.5)).astype(BF16)
        live["o"] = jnp.zeros((x_ref.shape[0], MEM_WIDTH), F32)

    def head(h):
        s = _dot_nt(live["q"], k_ref[0, h])
        e = jnp.exp(s - jnp.max(s, axis=-1, keepdims=True))
        inv = 1.0 / jnp.sum(e, axis=-1, keepdims=True)
        live["o"] = live["o"] + _dot(e.astype(BF16), v_ref[0, h]) * inv

    def project():
        o = live.pop("o").astype(BF16)
        for c in range(D_MODEL // MIX_CHUNK):
            cs = slice(c * MIX_CHUNK, (c + 1) * MIX_CHUNK)
            x2_ref[:, cs] = x_ref[:, cs] + _dot(o, wo_ref[:, cs])
        hf_ref[...] = _rms(x2_ref[...], nf_ref[...]).astype(BF16)

    return [queries] + [functools.partial(head, h) for h in range(MEM_HEADS)] + [project]


def _ffn_stages(x2_ref, hf_ref, wg_ref, wu_ref, wd_ref, nfin_ref, act_ref, o_ref):
    live = {"ss": jnp.zeros((o_ref.shape[0], 1), F32)}

    def hidden(cs):
        hf = hf_ref[...]
        g = _dot(hf, wg_ref[:, cs])
        u = _dot(hf, wu_ref[:, cs])
        act_ref[:, cs] = (g * jax.nn.sigmoid(g) * u).astype(BF16)

    def down(cs):
        y = x2_ref[:, cs] + _dot(act_ref[...], wd_ref[:, cs])
        live["ss"] = live["ss"] + jnp.sum(y * y, axis=-1, keepdims=True)
        o_ref[:, cs] = y

    def final_norm():
        inv = lax.rsqrt(live["ss"] * (1.0 / D_MODEL) + EPS)
        o_ref[...] = o_ref[...] * inv * nfin_ref[...]

    hid = [slice(c * FF_CHUNK, (c + 1) * FF_CHUNK) for c in range(D_FF // FF_CHUNK)]
    out = [slice(c * MIX_CHUNK, (c + 1) * MIX_CHUNK) for c in range(D_MODEL // MIX_CHUNK)]
    return ([functools.partial(hidden, cs) for cs in hid]
            + [functools.partial(down, cs) for cs in out] + [final_norm])


def _tail_kernel(x_ref, k_ref, v_ref, nq_ref, wq_ref, wo_ref, nf_ref, wg_ref, wu_ref, wd_ref,
                 nfin_ref, o_ref, x2_scr, hf_scr, act_ref):
    i = pl.program_id(0)
    n_tiles = pl.num_programs(0) - 1
    slot_w = i % 2
    slot_r = 1 - slot_w

    def xattn():
        return _xattn_stages(x_ref, k_ref, v_ref, nq_ref, wq_ref, wo_ref, nf_ref,
                             x2_scr.at[slot_w], hf_scr.at[slot_w])

    def ffn():
        return _ffn_stages(x2_scr.at[slot_r], hf_scr.at[slot_r], wg_ref, wu_ref, wd_ref,
                           nfin_ref, act_ref, o_ref)

    def merge(lead, lag):
        if not lag:
            return lead
        n_hidden = D_FF // FF_CHUNK
        n_first = n_hidden - len(lead)
        return lag[:n_first] + _interleave(lag[n_first:n_hidden], lead) + lag[n_hidden:]

    _run_lookahead(i, n_tiles, xattn, ffn, merge)


def _tail(x2d, kext, vext, nq, wq, wo, nf, wg, wu, wd, nfin, seq):
    tm = TM_TAIL
    t = x2d.shape[0]
    n_tiles = t // tm
    tiles_per_seq = seq // tm
    cur = lambda i: jnp.minimum(i, n_tiles - 1)
    prv = lambda i: jnp.maximum(i - 1, 0)
    mem_spec = pl.BlockSpec((1, MEM_HEADS, N_MEM, MEM_WIDTH),
                            lambda i: (cur(i) // tiles_per_seq, 0, 0, 0))
    consts = (nq, wq, wo, nf, wg, wu, wd, nfin)
    return pl.pallas_call(
        _tail_kernel,
        grid=(n_tiles + 1,),
        in_specs=[pl.BlockSpec((tm, D_MODEL), lambda i: (cur(i), 0)), mem_spec, mem_spec]
        + [_const_spec(c.shape) for c in consts],
        out_specs=pl.BlockSpec((tm, D_MODEL), lambda i: (prv(i), 0)),
        out_shape=jax.ShapeDtypeStruct((t, D_MODEL), F32),
        scratch_shapes=[pltpu.VMEM((2, tm, D_MODEL), F32), pltpu.VMEM((2, tm, D_MODEL), BF16),
                        pltpu.VMEM((tm, D_FF), BF16)],
        compiler_params=_params(("arbitrary",)),
        name="tail",
    )(x2d, kext, vext, *consts)


def _layer(x2d, mem, batch, seq, norm_mix_w, w_in, b_gate, attn_sinks, gla_gate_w2, gla_gate_b,
           gla_norm_w, w_attn_o, w_gla_o, w_mix_o, norm_mem_q_w, norm_mem_kv_w, w_mem_q,
           w_mem_kv, w_mem_o, norm_ffn_w, w_ffn_gate, w_ffn_up, w_ffn_down, out_norm_w):
    w2_p = jnp.pad(gla_gate_w2, ((0, LANES - GLA_GATE_RANK), (0, 0)))
    bf = lambda a: a.astype(BF16)
    r2 = lambda a: a.reshape(1, -1)

    qa, ka, vat, qg, kg, vgt, gg, gk, gates, wa16, wgo16, wmo16 = _in_proj(
        x2d, r2(norm_mix_w), bf(w_in.T), bf(w2_p), r2(gla_gate_b), r2(b_gate),
        r2(jnp.tile(gla_norm_w, GLA_HEADS)), w_attn_o, w_gla_o, w_mix_o)
    x1, wg16, wu16, wd16, wq16, wo16, wkv16 = _attn_mix(
        attn_sinks, qa, ka, vat, qg, kg, gk, vgt, gg, x2d, gates, wa16, wgo16, wmo16,
        (w_ffn_gate, w_ffn_up, w_ffn_down, w_mem_q, w_mem_o, w_mem_kv), seq)
    kext, vext = _mem_kv(mem, r2(norm_mem_kv_w), wkv16)
    return _tail(x1, kext, vext, r2(norm_mem_q_w), wq16, wo16, r2(norm_ffn_w),
                 wg16, wu16, wd16, r2(out_norm_w), seq)


def kernel(x, mem, norm_mix_w, w_in, b_gate, attn_sinks, gla_gate_w2, gla_gate_b, gla_norm_w,
           w_attn_o, w_gla_o, w_mix_o, norm_mem_q_w, norm_mem_kv_w, w_mem_q, w_mem_kv, w_mem_o,
           norm_ffn_w, w_ffn_gate, w_ffn_up, w_ffn_down, norm_final_w):
    batch, seq, d = x.shape
    depth = w_in.shape[0]
    assert depth == 1 and d == D_MODEL
    assert seq % TM_PROJ == 0 and seq % TM_ATTN == 0 and seq % TM_TAIL == 0
    out = _layer(x.reshape(batch * seq, d), mem, batch, seq, norm_mix_w[0], w_in[0], b_gate[0],
                 attn_sinks[0], gla_gate_w2[0], gla_gate_b[0], gla_norm_w[0], w_attn_o[0],
                 w_gla_o[0], w_mix_o[0], norm_mem_q_w[0], norm_mem_kv_w[0], w_mem_q[0],
                 w_mem_kv[0], w_mem_o[0], norm_ffn_w[0], w_ffn_gate[0], w_ffn_up[0],
                 w_ffn_down[0], norm_final_w)
    return out.reshape(batch, seq, d)
```

```python
import functools

import jax
import jax.numpy as jnp
from jax import lax
from jax.experimental import pallas as pl
from jax.experimental.pallas import tpu as pltpu

D_MODEL = 1024
CHUNK = 64
N_MEM = 256
EPS = 1e-6

SWA_HEADS = 16
SWA_KV_HEADS = 2
SWA_HEAD_DIM = 64
SWA_BLOCK = 128

GLA_HEADS = 4
GLA_KEY_DIM = D_MODEL // 2
GLA_VAL_DIM = D_MODEL
GLA_DK = GLA_KEY_DIM // GLA_HEADS
GLA_DV = GLA_VAL_DIM // GLA_HEADS
GLA_GATE_RANK = 16
GLA_GATE_NORM = 16.0

MEM_HEADS = 4
MEM_HEAD_DIM = 64
MEM_WIDTH = MEM_HEADS * MEM_HEAD_DIM

D_FF = -(-(8 * D_MODEL) // (3 * 256)) * 256

IN_SIZES = (SWA_HEADS * SWA_HEAD_DIM, SWA_KV_HEADS * SWA_HEAD_DIM, SWA_KV_HEADS * SWA_HEAD_DIM,
            GLA_KEY_DIM, GLA_KEY_DIM, GLA_VAL_DIM, GLA_VAL_DIM, GLA_GATE_RANK, 2 * D_MODEL)
IN_OFFSETS = tuple(sum(IN_SIZES[:i]) for i in range(len(IN_SIZES) + 1))

LANES = 128
VMEM_LIMIT = 56 * 1024 * 1024

LOG2E = 1.4426950408889634

TM_PROJ = 1024
PROJ_M = 512
PROJ_N = 256
TM_ATTN = 512
TM_TAIL = 512
MIX_CHUNK = 256
FF_CHUNK = 256

BF16 = jnp.bfloat16
F32 = jnp.float32


def _rms(x, w):
    return x * lax.rsqrt(jnp.mean(x * x, axis=-1, keepdims=True) + EPS) * w


def _dot(a, b):
    return jnp.dot(a, b, preferred_element_type=F32)


def _dot_nt(a, b):
    return lax.dot_general(a, b, (((1,), (1,)), ((), ())), preferred_element_type=F32)


def _const_spec(shape):
    zeros = (0,) * len(shape)
    return pl.BlockSpec(shape, lambda *_: zeros, pipeline_mode=pl.Buffered(1))


def _params(semantics):
    return pltpu.CompilerParams(dimension_semantics=semantics, vmem_limit_bytes=VMEM_LIMIT)


def _in_proj_kernel(n_cast, x_ref, nw_ref, wt_ref, w2_ref, gb_ref, bg_ref, gn_ref, *rest):
    cast_src, cast_dst = rest[:n_cast], rest[len(rest) - n_cast:]
    (qa_ref, ka_ref, vat_ref, qg_ref, kg_ref, vgt_ref, gg_ref, gk_ref,
     gate_ref) = rest[n_cast:len(rest) - n_cast]
    for src, dst in zip(cast_src, cast_dst):
        dst[...] = src[...].astype(BF16)
    o_qa, o_ka, o_va, o_qg, o_kg, o_vg, o_gg, o_alr, o_gate, o_end = IN_OFFSETS
    hd = SWA_HEAD_DIM
    tm = x_ref.shape[0]
    row_halves = [slice(m0, m0 + PROJ_M) for m0 in range(0, tm, PROJ_M)]
    h = {rows.start: _rms(x_ref[rows, :], nw_ref[...]).astype(BF16) for rows in row_halves}

    def project(rows, weight_rows, n_cols, store):
        def piece(c0):
            store(rows, c0, _dot_nt(h[rows.start], weight_rows(c0)))
        return [functools.partial(piece, c0) for c0 in range(0, n_cols, PROJ_N)]

    def project_t(rows, row0, n_rows, dst_ref):
        def piece(t0, r0, nr):
            dst_ref[r0:r0 + nr, rows.start + t0:rows.start + t0 + PROJ_N] = _dot_nt(
                wt_ref[row0 + r0:row0 + r0 + nr, :], h[rows.start][t0:t0 + PROJ_N]).astype(BF16)
        return [functools.partial(piece, t0, r0, min(PROJ_M, n_rows - r0))
                for t0 in range(0, PROJ_M, PROJ_N) for r0 in range(0, n_rows, PROJ_M)]

    def from_wt(row0):
        return lambda c0: wt_ref[row0 + c0:row0 + c0 + PROJ_N, :]

    def plain(dst_ref):
        def store(rows, c0, y):
            dst_ref[rows, c0:c0 + PROJ_N] = y.astype(BF16)
        return store

    def store_gates(lo):
        def store(rows, c0, y):
            cs = slice(lo + c0, lo + c0 + PROJ_N)
            gate_ref[rows, cs] = jax.nn.sigmoid(y + bg_ref[:, cs]).astype(BF16)
        return store

    def store_q(rows, c0, y):
        qa_ref[rows, c0:c0 + PROJ_N] = (y * (hd ** -0.5 * LOG2E)).astype(BF16)

    def store_swish(rows, c0, y):
        gg_ref[rows, c0:c0 + PROJ_N] = (y * jax.nn.sigmoid(y)
                                        * gn_ref[:, c0:c0 + PROJ_N]).astype(BF16)

    a_lr = {}

    def store_keys(rows, c0, y):
        kw = SWA_KV_HEADS * hd
        ka_ref[rows, :] = y[:, :kw].astype(BF16)
        a_lr[rows.start] = y[:, kw:].astype(BF16)

    def log_decay(rows, zs):
        z = _dot(a_lr[rows.start], w2_ref[:, zs]) + gb_ref[:, zs]
        log_sig = jnp.minimum(z, 0.0) - jnp.log(1.0 + jnp.exp(-jnp.abs(z)))
        gk_ref[rows, zs] = (log_sig * (1.0 / GLA_GATE_NORM)).astype(BF16)

    group = SWA_HEADS // SWA_KV_HEADS
    wq = jnp.concatenate(
        [wt_ref[o_qa + (p + j * group) * hd:o_qa + (p + j * group + 1) * hd, :]
         for p in range(group) for j in range(SWA_KV_HEADS)], axis=0)
    w_k_alr = jnp.concatenate([wt_ref[o_ka:o_va, :], wt_ref[o_alr:o_alr + LANES, :]], axis=0)
    half = (o_end - o_gate) // 2
    def both_halves(pieces_of):
        return [piece for same in zip(*[pieces_of(rows) for rows in row_halves]) for piece in same]

    heavy = both_halves(lambda rows: (
        project(rows, from_wt(o_gate), half, store_gates(0))
        + project(rows, from_wt(o_gate + half), half, store_gates(half))
        + project(rows, from_wt(o_gg), o_alr - o_gg, store_swish)))
    light = both_halves(lambda rows: (
        project(rows, lambda c0: wq[c0:c0 + PROJ_N], o_ka - o_qa, store_q)
        + project_t(rows, o_vg, o_gg - o_vg, vgt_ref)
        + project(rows, from_wt(o_qg), o_kg - o_qg, plain(qg_ref))))
    keys = both_halves(lambda rows: project(rows, lambda c0: w_k_alr, PROJ_N, store_keys))
    decay = both_halves(lambda rows: [functools.partial(log_decay, rows, slice(z0, z0 + PROJ_N))
                                      for z0 in range(0, GLA_KEY_DIM, PROJ_N)])
    rest = both_halves(lambda rows: (
        project(rows, from_wt(o_kg), o_vg - o_kg, plain(kg_ref))
        + project_t(rows, o_va, o_qg - o_va, vat_ref)))
    for piece in _interleave(heavy, light) + keys + _interleave(rest, decay):
        piece()


def _in_proj(x2d, nw, wt, w2, gb, bg, gn, w_attn_o, w_gla_o, w_mix_o):
    t = x2d.shape[0]
    tm = TM_PROJ
    steps = t // tm
    row = lambda n: pl.BlockSpec((tm, n), lambda i: (i, 0))
    col = lambda n: pl.BlockSpec((n, tm), lambda i: (0, i))
    consts = (nw, wt, w2, gb, bg, gn)
    kw = SWA_KV_HEADS * SWA_HEAD_DIM
    tok = lambda n: jax.ShapeDtypeStruct((t, n), BF16)
    out_shape = [
        tok(SWA_HEADS * SWA_HEAD_DIM), tok(kw), jax.ShapeDtypeStruct((kw, t), BF16),
        tok(GLA_KEY_DIM), tok(GLA_KEY_DIM), jax.ShapeDtypeStruct((GLA_VAL_DIM, t), BF16),
        tok(GLA_VAL_DIM), tok(GLA_KEY_DIM), tok(2 * D_MODEL),
    ]
    out_specs = [
        row(SWA_HEADS * SWA_HEAD_DIM), row(kw), col(kw),
        row(GLA_KEY_DIM), row(GLA_KEY_DIM), col(GLA_VAL_DIM),
        row(GLA_VAL_DIM), row(GLA_KEY_DIM), row(2 * D_MODEL),
    ]
    cast_w = (w_attn_o, w_gla_o, w_mix_o)
    slab_rows = D_MODEL // steps
    group = SWA_HEADS // SWA_KV_HEADS
    assert slab_rows == SWA_HEAD_DIM and steps == SWA_HEADS and slab_rows % 16 == 0
    slab = pl.BlockSpec((slab_rows, D_MODEL), lambda i: (i, 0))
    head_slab = pl.BlockSpec((slab_rows, D_MODEL),
                             lambda i: (i // SWA_KV_HEADS + group * (i % SWA_KV_HEADS), 0))
    return pl.pallas_call(
        functools.partial(_in_proj_kernel, len(cast_w)),
        grid=(steps,),
        in_specs=[row(D_MODEL)] + [_const_spec(c.shape) for c in consts]
        + [head_slab, slab, slab],
        out_specs=out_specs + [slab] * len(cast_w),
        out_shape=out_shape + [jax.ShapeDtypeStruct(w.shape, BF16) for w in cast_w],
        compiler_params=_params(("arbitrary",)),
        name="in_proj",
    )(x2d, *consts, *cast_w)


def _swa_stages(sink_ref, q_ref, k_ref, kp_ref, vt_ref, vtp_ref, seq_start, write):
    blk = SWA_BLOCK
    win = blk + CHUNK
    low = lax.broadcasted_iota(jnp.int32, (CHUNK, LANES), 1) < SWA_HEAD_DIM
    low_row = lax.broadcasted_iota(jnp.int32, (1, LANES), 1) < SWA_HEAD_DIM
    prev_bias = jnp.where(seq_start, -1e30, 0.0).astype(F32)
    k_all = jnp.concatenate([kp_ref[...], k_ref[...]], axis=0)
    vt_all = jnp.concatenate([vtp_ref[...], vt_ref[...]], axis=1)
    zeros_chunk = jnp.zeros((CHUNK, LANES), BF16)
    n_pairs = SWA_HEADS // SWA_KV_HEADS
    n_blocks = q_ref.shape[0] // blk
    n_ch = blk // CHUNK
    live = {}

    def scores(p):
        res = []
        for u in range(n_blocks):
            qsel = []
            for qh in range(n_ch):
                r0 = u * blk + qh * CHUNK
                qp = q_ref[r0:r0 + CHUNK, p * LANES:(p + 1) * LANES]
                zero = jnp.zeros_like(qp)
                qsel += [jnp.where(low, qp, zero), jnp.where(low, zero, qp)]
            res.append(_dot_nt(k_all[u * blk:(u + 2) * blk], jnp.concatenate(qsel, axis=0)))
        live["s", p] = res

    def softmax_values(p):
        sink_row = jnp.where(low_row, sink_ref[p], sink_ref[p + n_pairs]) * LOG2E
        res = []
        for u, s_blk in enumerate(live.pop(("s", p))):
            probs, inv = [], []
            for qh in range(n_ch):
                st = s_blk[qh * CHUNK:qh * CHUNK + win, qh * LANES:(qh + 1) * LANES]
                if u == 0:
                    n_prev = blk - qh * CHUNK
                    st = jnp.concatenate([st[:n_prev] + prev_bias, st[n_prev:]], axis=0)
                m = jnp.maximum(jnp.max(st, axis=0, keepdims=True), sink_row)
                e = jnp.exp2(st - m)
                inv.append(1.0 / (jnp.sum(e, axis=0, keepdims=True) + jnp.exp2(sink_row - m)))
                probs.append(jnp.concatenate([zeros_chunk] * qh + [e.astype(BF16)]
                                             + [zeros_chunk] * (n_ch - 1 - qh), axis=0))
            ot = _dot(vt_all[:, u * blk:(u + 2) * blk], jnp.concatenate(probs, axis=1))
            res.append([ot[:, qh * LANES:(qh + 1) * LANES] * inv[qh] for qh in range(n_ch)])
        live["o", p] = res

    def store(p):
        for u, ots in enumerate(live.pop(("o", p))):
            for qh, ot in enumerate(ots):
                r0 = u * blk + qh * CHUNK
                o2 = ot.T
                write(slice(r0, r0 + CHUNK), slice(p * LANES, (p + 1) * LANES),
                      jnp.where(low, o2[:CHUNK], o2[CHUNK:]).astype(BF16))

    return [(functools.partial(scores, p), functools.partial(softmax_values, p),
             functools.partial(store, p)) for p in range(n_pairs)]


def _gla_stages(q_ref, k_ref, gk_ref, vt_ref, g_ref, st_ref, seq_start, write):
    pair = 2 * CHUNK
    heads = range(GLA_HEADS)
    ks = [slice(h * GLA_DK, (h + 1) * GLA_DK) for h in heads]
    vs = [slice(h * GLA_DV, (h + 1) * GLA_DV) for h in heads]
    r = lax.broadcasted_iota(jnp.int32, (pair, pair), 0)
    c = lax.broadcasted_iota(jnp.int32, (pair, pair), 1)
    tri = ((c <= r) & ((r // CHUNK) == (c // CHUNK))).astype(BF16)
    first = lax.broadcasted_iota(jnp.int32, (pair, GLA_DK), 0) < CHUNK
    eps_scaled = EPS * GLA_DK
    n_pairs = q_ref.shape[0] // pair
    sts = [jnp.where(seq_start, jnp.zeros((GLA_DV, GLA_DK), F32), st_ref[h]) for h in heads]
    live = {}

    def decay(i):
        sl = slice(i * pair, (i + 1) * pair)
        b_all = _dot(tri, gk_ref[sl, :])
        res = []
        for h in heads:
            b = b_all[:, ks[h]]
            b_end0 = b[CHUNK - 1:CHUNK, :]
            b_end1 = b[pair - 1:pair, :]
            k_dec = (k_ref[sl, ks[h]].astype(F32)
                     * jnp.exp(jnp.where(first, b_end0, b_end1) - b)).astype(BF16)
            zero = jnp.zeros_like(k_dec)
            kd = jnp.concatenate([jnp.where(first, k_dec, zero), jnp.where(first, zero, k_dec)],
                                 axis=1)
            res.append((jnp.exp(b_end0), jnp.exp(b_end1), kd))
        live["d", i] = res

    def key_values(i):
        sl = slice(i * pair, (i + 1) * pair)
        live["kv", i] = [(a0, a1, _dot(vt_ref[vs[h], sl], kd))
                         for h, (a0, a1, kd) in zip(heads, live.pop(("d", i)))]

    def state_and_out(i):
        sl = slice(i * pair, (i + 1) * pair)
        dk = live.pop(("kv", i))
        outs = []
        for h in heads:
            a0, a1, kv = dk[h]
            st0 = sts[h] * a0 + kv[:, :GLA_DK]
            st1 = st0 * a1 + kv[:, GLA_DK:]
            sts[h] = st1
            q = q_ref[sl, ks[h]]
            outs.append(jnp.concatenate([_dot_nt(q[:CHUNK], st0.astype(BF16)),
                                         _dot_nt(q[CHUNK:], st1.astype(BF16))], axis=0))
        for h in heads:
            o = outs[h]
            inv = lax.rsqrt(jnp.mean(o * o, axis=-1, keepdims=True) + eps_scaled)
            write(sl, vs[h], (o * inv).astype(BF16) * g_ref[sl, vs[h]])

    def save_state():
        for h in heads:
            st_ref[h] = sts[h]

    stages = [functools.partial(decay, 0)]
    for i in range(n_pairs):
        if i + 1 < n_pairs:
            stages.append(functools.partial(decay, i + 1))
        stages.append(functools.partial(key_values, i))
        if i > 0:
            stages.append(functools.partial(state_and_out, i - 1))
    stages += [functools.partial(state_and_out, n_pairs - 1), save_state]
    return stages


def _mix_stages(x_ref, gate_ref, wa_ref, wg_ref, wm_ref, oa_ref, og_ref, mg_ref, o_ref):
    pieces = [slice(c * MIX_CHUNK, (c + 1) * MIX_CHUNK) for c in range(D_MODEL // MIX_CHUNK)]

    def merge(cs):
        yg = _dot(og_ref[...], wg_ref[:, cs])
        ya = _dot(oa_ref[...], wa_ref[:, cs])
        cs_b = slice(D_MODEL + cs.start, D_MODEL + cs.stop)
        mg_ref[:, cs] = (gate_ref[:, cs].astype(F32) * ya
                         + gate_ref[:, cs_b].astype(F32) * yg).astype(BF16)

    def project(cs):
        o_ref[:, cs] = x_ref[:, cs] + _dot(mg_ref[...], wm_ref[:, cs])

    return ([functools.partial(merge, cs) for cs in pieces]
            + [functools.partial(project, cs) for cs in pieces])


def _interleave(major, minor):
    out, done = [], 0
    for n, stage in enumerate(major):
        out.append(stage)
        want = (n + 1) * len(minor) // len(major)
        out.extend(minor[done:want])
        done = want
    return out


def _run_lookahead(i, n_tiles, lead, lag, merge, init=None):
    @pl.when(i == 0)
    def _():
        if init is not None:
            init()
        for stage in merge(lead(), []):
            stage()

    @pl.when(jnp.logical_and(i > 0, i < n_tiles))
    def _():
        for stage in merge(lead(), lag()):
            stage()

    @pl.when(i == n_tiles)
    def _():
        for stage in merge([], lag()):
            stage()


def _attn_mix_kernel(tiles_per_seq, n_cast,
                     sink_ref, q_ref, k_ref, kp_ref, vt_ref, vtp_ref,
                     qg_ref, kg_ref, gk_ref, vgt_ref, gg_ref,
                     x_ref, gate_ref, wa_ref, wg_ref, wm_ref, *rest):
    cast_src, (o_ref, *cast_dst) = rest[:n_cast], rest[n_cast:2 * n_cast + 1]
    oa_scr, og_scr, mg_scr, st_ref = rest[2 * n_cast + 1:]
    i = pl.program_id(0)
    n_tiles = pl.num_programs(0) - 1
    seq_start = (i % tiles_per_seq) == 0
    slot_w = i % 2
    slot_r = 1 - slot_w

    def write_oa(rows, cols, val):
        oa_scr[slot_w, rows, cols] = val

    def write_og(rows, cols, val):
        og_scr[slot_w, rows, cols] = val

    def mixers():
        swa = _swa_stages(sink_ref, q_ref, k_ref, kp_ref, vt_ref, vtp_ref, seq_start, write_oa)
        gla = _gla_stages(qg_ref, kg_ref, gk_ref, vgt_ref, gg_ref, st_ref, seq_start, write_og)
        n = len(swa)
        order = [swa[0][0]]
        for p in range(n):
            if p + 1 < n:
                order.append(swa[p + 1][0])
            order.append(swa[p][1])
            if p > 0:
                order.append(swa[p - 1][2])
        order.append(swa[n - 1][2])
        return _interleave(order, gla) + [convert_weights]

    def projections():
        return _mix_stages(x_ref, gate_ref, wa_ref, wg_ref, wm_ref,
                           oa_scr.at[slot_r], og_scr.at[slot_r], mg_scr, o_ref)

    def merge(lead, lag):
        return _interleave(lead, lag) if lead else lag

    def convert_weights():
        for src, dst in zip(cast_src, cast_dst):
            dst[...] = src[...].astype(BF16)

    def init():
        st_ref[...] = jnp.zeros_like(st_ref)

    _run_lookahead(i, n_tiles, mixers, projections, merge, init)


def _attn_mix(sinks, qa, ka, vat, qg, kg, gk, vgt, gg, x2d, gates, wa, wg, wm, cast_w, seq):
    t = x2d.shape[0]
    tm = TM_ATTN
    n_tiles = t // tm
    blocks = tm // SWA_BLOCK
    kw = ka.shape[1]
    cur = lambda i: jnp.minimum(i, n_tiles - 1)
    prv = lambda i: jnp.maximum(i - 1, 0)
    prev_block = lambda i: jnp.maximum(cur(i) * blocks - 1, 0)
    tok = lambda n: pl.BlockSpec((tm, n), lambda i: (cur(i), 0))
    col = lambda n: pl.BlockSpec((n, tm), lambda i: (0, cur(i)))
    old = lambda n: pl.BlockSpec((tm, n), lambda i: (prv(i), 0))

    def slab(w):
        n_slabs = max(n for n in range(1, n_tiles + 1) if w.shape[0] % (16 * n) == 0)
        return pl.BlockSpec((w.shape[0] // n_slabs, w.shape[1]),
                            lambda i: (jnp.minimum(i, n_slabs - 1), 0))

    cast_specs = [slab(w) for w in cast_w]
    return pl.pallas_call(
        functools.partial(_attn_mix_kernel, seq // tm, len(cast_w)),
        grid=(n_tiles + 1,),
        in_specs=[
            pl.BlockSpec(memory_space=pltpu.SMEM),
            tok(qa.shape[1]), tok(kw),
            pl.BlockSpec((SWA_BLOCK, kw), lambda i: (prev_block(i), 0)),
            col(kw),
            pl.BlockSpec((kw, SWA_BLOCK), lambda i: (0, prev_block(i))),
            tok(GLA_KEY_DIM), tok(GLA_KEY_DIM), tok(GLA_KEY_DIM), col(GLA_VAL_DIM),
            tok(GLA_VAL_DIM),
            old(D_MODEL), old(2 * D_MODEL),
            _const_spec(wa.shape), _const_spec(wg.shape), _const_spec(wm.shape),
        ] + cast_specs,
        out_specs=[old(D_MODEL)] + cast_specs,
        out_shape=[jax.ShapeDtypeStruct((t, D_MODEL), F32)]
        + [jax.ShapeDtypeStruct(w.shape, BF16) for w in cast_w],
        scratch_shapes=[
            pltpu.VMEM((2, tm, SWA_HEADS * SWA_HEAD_DIM), BF16),
            pltpu.VMEM((2, tm, GLA_VAL_DIM), BF16),
            pltpu.VMEM((tm, D_MODEL), BF16),
            pltpu.VMEM((GLA_HEADS, GLA_DV, GLA_DK), F32),
        ],
        compiler_params=_params(("arbitrary",)),
        name="attn_mix",
    )(sinks, qa, ka, ka, vat, vat, qg, kg, gk, vgt, gg, x2d, gates, wa, wg, wm, *cast_w)


def _mem_kv_kernel(m_ref, nw_ref, w_ref, k_ref, v_ref):
    mn = _rms(m_ref[0], nw_ref[...]).astype(BF16)
    kv = _dot(mn, w_ref[...])
    k = kv[:, :MEM_WIDTH].astype(BF16)
    v = kv[:, MEM_WIDTH:].astype(BF16)
    head = lax.broadcasted_iota(jnp.int32, (N_MEM, MEM_WIDTH), 1) // MEM_HEAD_DIM
    zero = jnp.zeros_like(k)
    for h in range(MEM_HEADS):
        k_ref[0, h] = jnp.where(head == h, k, zero)
        v_ref[0, h] = jnp.where(head == h, v, zero)


def _mem_kv(mem, nw, w):
    batch = mem.shape[0]
    out = jax.ShapeDtypeStruct((batch, MEM_HEADS, N_MEM, MEM_WIDTH), BF16)
    spec = pl.BlockSpec((1, MEM_HEADS, N_MEM, MEM_WIDTH), lambda b: (b, 0, 0, 0))
    return pl.pallas_call(
        _mem_kv_kernel,
        grid=(batch,),
        in_specs=[pl.BlockSpec((1, N_MEM, D_MODEL), lambda b: (b, 0, 0)),
                  _const_spec(nw.shape), _const_spec(w.shape)],
        out_specs=(spec, spec),
        out_shape=(out, out),
        compiler_params=_params(("arbitrary",)),
        name="mem_kv",
    )(mem, nw, w)


def _xattn_stages(x_ref, k_ref, v_ref, nq_ref, wq_ref, wo_ref, nf_ref, x2_ref, hf_ref):
    live = {}

    def queries():
        hq = _rms(x_ref[...], nq_ref[...]).astype(BF16)
        live["q"] = (_dot(hq, wq_ref[...]) * (MEM_HEAD_DIM ** -0.5)).astype(BF16)
        live["o"] = jnp.zeros((x_ref.shape[0], MEM_WIDTH), F32)

    def head(h):
        s = _dot_nt(live["q"], k_ref[0, h])
        e = jnp.exp(s - jnp.max(s, axis=-1, keepdims=True))
        inv = 1.0 / jnp.sum(e, axis=-1, keepdims=True)
        live["o"] = live["o"] + _dot(e.astype(BF16), v_ref[0, h]) * inv

    def project():
        o = live.pop("o").astype(BF16)
        for c in range(D_MODEL // MIX_CHUNK):
            cs = slice(c * MIX_CHUNK, (c + 1) * MIX_CHUNK)
            x2_ref[:, cs] = x_ref[:, cs] + _dot(o, wo_ref[:, cs])
        hf_ref[...] = _rms(x2_ref[...], nf_ref[...]).astype(BF16)

    return [queries] + [functools.partial(head, h) for h in range(MEM_HEADS)] + [project]


def _ffn_stages(x2_ref, hf_ref, wg_ref, wu_ref, wd_ref, nfin_ref, act_ref, o_ref):
    live = {"ss": jnp.zeros((o_ref.shape[0], 1), F32)}

    def hidden(cs):
        hf = hf_ref[...]
        g = _dot(hf, wg_ref[:, cs])
        u = _dot(hf, wu_ref[:, cs])
        act_ref[:, cs] = (g * jax.nn.sigmoid(g) * u).astype(BF16)

    def down(cs):
        y = x2_ref[:, cs] + _dot(act_ref[...], wd_ref[:, cs])
        live["ss"] = live["ss"] + jnp.sum(y * y, axis=-1, keepdims=True)
        o_ref[:, cs] = y

    def final_norm():
        inv = lax.rsqrt(live["ss"] * (1.0 / D_MODEL) + EPS)
        o_ref[...] = o_ref[...] * inv * nfin_ref[...]

    hid = [slice(c * FF_CHUNK, (c + 1) * FF_CHUNK) for c in range(D_FF // FF_CHUNK)]
    out = [slice(c * MIX_CHUNK, (c + 1) * MIX_CHUNK) for c in range(D_MODEL // MIX_CHUNK)]
    return ([functools.partial(hidden, cs) for cs in hid]
            + [functools.partial(down, cs) for cs in out] + [final_norm])


def _tail_kernel(x_ref, k_ref, v_ref, nq_ref, wq_ref, wo_ref, nf_ref, wg_ref, wu_ref, wd_ref,
                 nfin_ref, o_ref, x2_scr, hf_scr, act_ref):
    i = pl.program_id(0)
    n_tiles = pl.num_programs(0) - 1
    slot_w = i % 2
    slot_r = 1 - slot_w

    def xattn():
        return _xattn_stages(x_ref, k_ref, v_ref, nq_ref, wq_ref, wo_ref, nf_ref,
                             x2_scr.at[slot_w], hf_scr.at[slot_w])

    def ffn():
        return _ffn_stages(x2_scr.at[slot_r], hf_scr.at[slot_r], wg_ref, wu_ref, wd_ref,
                           nfin_ref, act_ref, o_ref)

    def merge(lead, lag):
        if not lag:
            return lead
        n_hidden = D_FF // FF_CHUNK
        n_first = n_hidden - len(lead)
        return lag[:n_first] + _interleave(lag[n_first:n_hidden], lead) + lag[n_hidden:]

    _run_lookahead(i, n_tiles, xattn, ffn, merge)


def _tail(x2d, kext, vext, nq, wq, wo, nf, wg, wu, wd, nfin, seq):
    tm = TM_TAIL
    t = x2d.shape[0]
    n_tiles = t // tm
    tiles_per_seq = seq // tm
    cur = lambda i: jnp.minimum(i, n_tiles - 1)
    prv = lambda i: jnp.maximum(i - 1, 0)
    mem_spec = pl.BlockSpec((1, MEM_HEADS, N_MEM, MEM_WIDTH),
                            lambda i: (cur(i) // tiles_per_seq, 0, 0, 0))
    consts = (nq, wq, wo, nf, wg, wu, wd, nfin)
    return pl.pallas_call(
        _tail_kernel,
        grid=(n_tiles + 1,),
        in_specs=[pl.BlockSpec((tm, D_MODEL), lambda i: (cur(i), 0)), mem_spec, mem_spec]
        + [_const_spec(c.shape) for c in consts],
        out_specs=pl.BlockSpec((tm, D_MODEL), lambda i: (prv(i), 0)),
        out_shape=jax.ShapeDtypeStruct((t, D_MODEL), F32),
        scratch_shapes=[pltpu.VMEM((2, tm, D_MODEL), F32), pltpu.VMEM((2, tm, D_MODEL), BF16),
                        pltpu.VMEM((tm, D_FF), BF16)],
        compiler_params=_params(("arbitrary",)),
        name="tail",
    )(x2d, kext, vext, *consts)


def _layer(x2d, mem, batch, seq, norm_mix_w, w_in, b_gate, attn_sinks, gla_gate_w2, gla_gate_b,
           gla_norm_w, w_attn_o, w_gla_o, w_mix_o, norm_mem_q_w, norm_mem_kv_w, w_mem_q,
           w_mem_kv, w_mem_o, norm_ffn_w, w_ffn_gate, w_ffn_up, w_ffn_down, out_norm_w):
    w2_p = jnp.pad(gla_gate_w2, ((0, LANES - GLA_GATE_RANK), (0, 0)))
    bf = lambda a: a.astype(BF16)
    r2 = lambda a: a.reshape(1, -1)

    qa, ka, vat, qg, kg, vgt, gg, gk, gates, wa16, wgo16, wmo16 = _in_proj(
        x2d, r2(norm_mix_w), bf(w_in.T), bf(w2_p), r2(gla_gate_b), r2(b_gate),
        r2(jnp.tile(gla_norm_w, GLA_HEADS)), w_attn_o, w_gla_o, w_mix_o)
    x1, wg16, wu16, wd16, wq16, wo16, wkv16 = _attn_mix(
        attn_sinks, qa, ka, vat, qg, kg, gk, vgt, gg, x2d, gates, wa16, wgo16, wmo16,
        (w_ffn_gate, w_ffn_up, w_ffn_down, w_mem_q, w_mem_o, w_mem_kv), seq)
    kext, vext = _mem_kv(mem, r2(norm_mem_kv_w), wkv16)
    return _tail(x1, kext, vext, r2(norm_mem_q_w), wq16, wo16, r2(norm_ffn_w),
                 wg16, wu16, wd16, r2(out_norm_w), seq)


def kernel(x, mem, norm_mix_w, w_in, b_gate, attn_sinks, gla_gate_w2, gla_gate_b, gla_norm_w,
           w_attn_o, w_gla_o, w_mix_o, norm_mem_q_w, norm_mem_kv_w, w_mem_q, w_mem_kv, w_mem_o,
           norm_ffn_w, w_ffn_gate, w_ffn_up, w_ffn_down, norm_final_w):
    batch, seq, d = x.shape
    depth = w_in.shape[0]
    assert depth == 1 and d == D_MODEL
    assert seq % TM_PROJ == 0 and seq % TM_ATTN == 0 and seq % TM_TAIL == 0
    out = _layer(x.reshape(batch * seq, d), mem, batch, seq, norm_mix_w[0], w_in[0], b_gate[0],
                 attn_sinks[0], gla_gate_w2[0], gla_gate_b[0], gla_norm_w[0], w_attn_o[0],
                 w_gla_o[0], w_mix_o[0], norm_mem_q_w[0], norm_mem_kv_w[0], w_mem_q[0],
                 w_mem_kv[0], w_mem_o[0], norm_ffn_w[0], w_ffn_gate[0], w_ffn_up[0],
                 w_ffn_down[0], norm_final_w)
    return out.reshape(batch, seq, d)
```

```python
import functools

import jax
import jax.numpy as jnp
from jax import lax
from jax.experimental import pallas as pl
from jax.experimental.pallas import tpu as pltpu

D_MODEL = 1024
CHUNK = 64
N_MEM = 256
EPS = 1e-6

SWA_HEADS = 16
SWA_KV_HEADS = 2
SWA_HEAD_DIM = 64
SWA_BLOCK = 128

GLA_HEADS = 4
GLA_KEY_DIM = D_MODEL // 2
GLA_VAL_DIM = D_MODEL
GLA_DK = GLA_KEY_DIM // GLA_HEADS
GLA_DV = GLA_VAL_DIM // GLA_HEADS
GLA_GATE_RANK = 16
GLA_GATE_NORM = 16.0

MEM_HEADS = 4
MEM_HEAD_DIM = 64
MEM_WIDTH = MEM_HEADS * MEM_HEAD_DIM

D_FF = -(-(8 * D_MODEL) // (3 * 256)) * 256

IN_SIZES = (SWA_HEADS * SWA_HEAD_DIM, SWA_KV_HEADS * SWA_HEAD_DIM, SWA_KV_HEADS * SWA_HEAD_DIM,
            GLA_KEY_DIM, GLA_KEY_DIM, GLA_VAL_DIM, GLA_VAL_DIM, GLA_GATE_RANK, 2 * D_MODEL)
IN_OFFSETS = tuple(sum(IN_SIZES[:i]) for i in range(len(IN_SIZES) + 1))

LANES = 128
VMEM_LIMIT = 56 * 1024 * 1024

LOG2E = 1.4426950408889634

TM_PROJ = 1024
PROJ_M = 512
PROJ_N = 256
TM_ATTN = 512
TM_TAIL = 512
MIX_CHUNK = 256
FF_CHUNK = 256

BF16 = jnp.bfloat16
F32 = jnp.float32


def _rms(x, w):
    return x * lax.rsqrt(jnp.mean(x * x, axis=-1, keepdims=True) + EPS) * w


def _dot(a, b):
    return jnp.dot(a, b, preferred_element_type=F32)


def _dot_nt(a, b):
    return lax.dot_general(a, b, (((1,), (1,)), ((), ())), preferred_element_type=F32)


def _const_spec(shape):
    zeros = (0,) * len(shape)
    return pl.BlockSpec(shape, lambda *_: zeros, pipeline_mode=pl.Buffered(1))


def _params(semantics):
    return pltpu.CompilerParams(dimension_semantics=semantics, vmem_limit_bytes=VMEM_LIMIT)


def _in_proj_kernel(n_cast, x_ref, nw_ref, wt_ref, w2_ref, gb_ref, bg_ref, gn_ref, *rest):
    cast_src, cast_dst = rest[:n_cast], rest[len(rest) - n_cast:]
    (qa_ref, ka_ref, vat_ref, qg_ref, kg_ref, vgt_ref, gg_ref, gk_ref,
     gate_ref) = rest[n_cast:len(rest) - n_cast]
    for src, dst in zip(cast_src, cast_dst):
        dst[...] = src[...].astype(BF16)
    o_qa, o_ka, o_va, o_qg, o_kg, o_vg, o_gg, o_alr, o_gate, o_end = IN_OFFSETS
    hd = SWA_HEAD_DIM
    tm = x_ref.shape[0]
    row_halves = [slice(m0, m0 + PROJ_M) for m0 in range(0, tm, PROJ_M)]
    h = {rows.start: _rms(x_ref[rows, :], nw_ref[...]).astype(BF16) for rows in row_halves}

    def project(rows, weight_rows, n_cols, store):
        def piece(c0):
            store(rows, c0, _dot_nt(h[rows.start], weight_rows(c0)))
        return [functools.partial(piece, c0) for c0 in range(0, n_cols, PROJ_N)]

    def project_t(rows, row0, n_rows, dst_ref):
        def piece(t0, r0, nr):
            dst_ref[r0:r0 + nr, rows.start + t0:rows.start + t0 + PROJ_N] = _dot_nt(
                wt_ref[row0 + r0:row0 + r0 + nr, :], h[rows.start][t0:t0 + PROJ_N]).astype(BF16)
        return [functools.partial(piece, t0, r0, min(PROJ_M, n_rows - r0))
                for t0 in range(0, PROJ_M, PROJ_N) for r0 in range(0, n_rows, PROJ_M)]

    def from_wt(row0):
        return lambda c0: wt_ref[row0 + c0:row0 + c0 + PROJ_N, :]

    def plain(dst_ref):
        def store(rows, c0, y):
            dst_ref[rows, c0:c0 + PROJ_N] = y.astype(BF16)
        return store

    def store_gates(lo):
        def store(rows, c0, y):
            cs = slice(lo + c0, lo + c0 + PROJ_N)
            gate_ref[rows, cs] = jax.nn.sigmoid(y + bg_ref[:, cs]).astype(BF16)
        return store

    def store_q(rows, c0, y):
        qa_ref[rows, c0:c0 + PROJ_N] = (y * (hd ** -0.5 * LOG2E)).astype(BF16)

    def store_swish(rows, c0, y):
        gg_ref[rows, c0:c0 + PROJ_N] = (y * jax.nn.sigmoid(y)
                                        * gn_ref[:, c0:c0 + PROJ_N]).astype(BF16)

    a_lr = {}

    def store_keys(rows, c0, y):
        kw = SWA_KV_HEADS * hd
        ka_ref[rows, :] = y[:, :kw].astype(BF16)
        a_lr[rows.start] = y[:, kw:].astype(BF16)

    def log_decay(rows, zs):
        z = _dot(a_lr[rows.start], w2_ref[:, zs]) + gb_ref[:, zs]
        log_sig = jnp.minimum(z, 0.0) - jnp.log(1.0 + jnp.exp(-jnp.abs(z)))
        gk_ref[rows, zs] = (log_sig * (1.0 / GLA_GATE_NORM)).astype(BF16)

    group = SWA_HEADS // SWA_KV_HEADS
    wq = jnp.concatenate(
        [wt_ref[o_qa + (p + j * group) * hd:o_qa + (p + j * group + 1) * hd, :]
         for p in range(group) for j in range(SWA_KV_HEADS)], axis=0)
    w_k_alr = jnp.concatenate([wt_ref[o_ka:o_va, :], wt_ref[o_alr:o_alr + LANES, :]], axis=0)
    half = (o_end - o_gate) // 2
    def both_halves(pieces_of):
        return [piece for same in zip(*[pieces_of(rows) for rows in row_halves]) for piece in same]

    heavy = both_halves(lambda rows: (
        project(rows, from_wt(o_gate), half, store_gates(0))
        + project(rows, from_wt(o_gate + half), half, store_gates(half))
        + project(rows, from_wt(o_gg), o_alr - o_gg, store_swish)))
    light = both_halves(lambda rows: (
        project(rows, lambda c0: wq[c0:c0 + PROJ_N], o_ka - o_qa, store_q)
        + project_t(rows, o_vg, o_gg - o_vg, vgt_ref)
        + project(rows, from_wt(o_qg), o_kg - o_qg, plain(qg_ref))))
    keys = both_halves(lambda rows: project(rows, lambda c0: w_k_alr, PROJ_N, store_keys))
    decay = both_halves(lambda rows: [functools.partial(log_decay, rows, slice(z0, z0 + PROJ_N))
                                      for z0 in range(0, GLA_KEY_DIM, PROJ_N)])
    rest = both_halves(lambda rows: (
        project(rows, from_wt(o_kg), o_vg - o_kg, plain(kg_ref))
        + project_t(rows, o_va, o_qg - o_va, vat_ref)))
    for piece in keys + _interleave(heavy, light + rest[:4]) + _interleave(rest[4:], decay):
        piece()


def _in_proj(x2d, nw, wt, w2, gb, bg, gn, w_attn_o, w_gla_o, w_mix_o):
    t = x2d.shape[0]
    tm = TM_PROJ
    steps = t // tm
    row = lambda n: pl.BlockSpec((tm, n), lambda i: (i, 0))
    col = lambda n: pl.BlockSpec((n, tm), lambda i: (0, i))
    consts = (nw, wt, w2, gb, bg, gn)
    kw = SWA_KV_HEADS * SWA_HEAD_DIM
    tok = lambda n: jax.ShapeDtypeStruct((t, n), BF16)
    out_shape = [
        tok(SWA_HEADS * SWA_HEAD_DIM), tok(kw), jax.ShapeDtypeStruct((kw, t), BF16),
        tok(GLA_KEY_DIM), tok(GLA_KEY_DIM), jax.ShapeDtypeStruct((GLA_VAL_DIM, t), BF16),
        tok(GLA_VAL_DIM), tok(GLA_KEY_DIM), tok(2 * D_MODEL),
    ]
    out_specs = [
        row(SWA_HEADS * SWA_HEAD_DIM), row(kw), col(kw),
        row(GLA_KEY_DIM), row(GLA_KEY_DIM), col(GLA_VAL_DIM),
        row(GLA_VAL_DIM), row(GLA_KEY_DIM), row(2 * D_MODEL),
    ]
    cast_w = (w_attn_o, w_gla_o, w_mix_o)
    slab_rows = D_MODEL // steps
    group = SWA_HEADS // SWA_KV_HEADS
    assert slab_rows == SWA_HEAD_DIM and steps == SWA_HEADS and slab_rows % 16 == 0
    slab = pl.BlockSpec((slab_rows, D_MODEL), lambda i: (i, 0))
    head_slab = pl.BlockSpec((slab_rows, D_MODEL),
                             lambda i: (i // SWA_KV_HEADS + group * (i % SWA_KV_HEADS), 0))
    return pl.pallas_call(
        functools.partial(_in_proj_kernel, len(cast_w)),
        grid=(steps,),
        in_specs=[row(D_MODEL)] + [_const_spec(c.shape) for c in consts]
        + [head_slab, slab, slab],
        out_specs=out_specs + [slab] * len(cast_w),
        out_shape=out_shape + [jax.ShapeDtypeStruct(w.shape, BF16) for w in cast_w],
        compiler_params=_params(("arbitrary",)),
        name="in_proj",
    )(x2d, *consts, *cast_w)


def _swa_stages(sink_ref, q_ref, k_ref, kp_ref, vt_ref, vtp_ref, seq_start, write):
    blk = SWA_BLOCK
    win = blk + CHUNK
    low = lax.broadcasted_iota(jnp.int32, (CHUNK, LANES), 1) < SWA_HEAD_DIM
    low_row = lax.broadcasted_iota(jnp.int32, (1, LANES), 1) < SWA_HEAD_DIM
    prev_bias = jnp.where(seq_start, -1e30, 0.0).astype(F32)
    k_all = jnp.concatenate([kp_ref[...], k_ref[...]], axis=0)
    vt_all = jnp.concatenate([vtp_ref[...], vt_ref[...]], axis=1)
    zeros_chunk = jnp.zeros((CHUNK, LANES), BF16)
    n_pairs = SWA_HEADS // SWA_KV_HEADS
    n_blocks = q_ref.shape[0] // blk
    n_ch = blk // CHUNK
    live = {}

    def scores(p):
        res = []
        for u in range(n_blocks):
            qsel = []
            for qh in range(n_ch):
                r0 = u * blk + qh * CHUNK
                qp = q_ref[r0:r0 + CHUNK, p * LANES:(p + 1) * LANES]
                zero = jnp.zeros_like(qp)
                qsel += [jnp.where(low, qp, zero), jnp.where(low, zero, qp)]
            res.append(_dot_nt(k_all[u * blk:(u + 2) * blk], jnp.concatenate(qsel, axis=0)))
        live["s", p] = res

    def softmax_values(p):
        sink_row = jnp.where(low_row, sink_ref[p], sink_ref[p + n_pairs]) * LOG2E
        res = []
        for u, s_blk in enumerate(live.pop(("s", p))):
            probs, inv = [], []
            for qh in range(n_ch):
                st = s_blk[qh * CHUNK:qh * CHUNK + win, qh * LANES:(qh + 1) * LANES]
                if u == 0:
                    n_prev = blk - qh * CHUNK
                    st = jnp.concatenate([st[:n_prev] + prev_bias, st[n_prev:]], axis=0)
                m = jnp.maximum(jnp.max(st, axis=0, keepdims=True), sink_row)
                e = jnp.exp2(st - m)
                inv.append(1.0 / (jnp.sum(e, axis=0, keepdims=True) + jnp.exp2(sink_row - m)))
                probs.append(jnp.concatenate([zeros_chunk] * qh + [e.astype(BF16)]
                                             + [zeros_chunk] * (n_ch - 1 - qh), axis=0))
            ot = _dot(vt_all[:, u * blk:(u + 2) * blk], jnp.concatenate(probs, axis=1))
            res.append([ot[:, qh * LANES:(qh + 1) * LANES] * inv[qh] for qh in range(n_ch)])
        live["o", p] = res

    def store(p):
        for u, ots in enumerate(live.pop(("o", p))):
            for qh, ot in enumerate(ots):
                r0 = u * blk + qh * CHUNK
                o2 = ot.T
                write(slice(r0, r0 + CHUNK), slice(p * LANES, (p + 1) * LANES),
                      jnp.where(low, o2[:CHUNK], o2[CHUNK:]).astype(BF16))

    return [(functools.partial(scores, p), functools.partial(softmax_values, p),
             functools.partial(store, p)) for p in range(n_pairs)]


def _gla_stages(q_ref, k_ref, gk_ref, vt_ref, g_ref, st_ref, seq_start, write):
    pair = 2 * CHUNK
    heads = range(GLA_HEADS)
    ks = [slice(h * GLA_DK, (h + 1) * GLA_DK) for h in heads]
    vs = [slice(h * GLA_DV, (h + 1) * GLA_DV) for h in heads]
    r = lax.broadcasted_iota(jnp.int32, (pair, pair), 0)
    c = lax.broadcasted_iota(jnp.int32, (pair, pair), 1)
    tri = ((c <= r) & ((r // CHUNK) == (c // CHUNK))).astype(BF16)
    first = lax.broadcasted_iota(jnp.int32, (pair, GLA_DK), 0) < CHUNK
    eps_scaled = EPS * GLA_DK
    n_pairs = q_ref.shape[0] // pair
    sts = [jnp.where(seq_start, jnp.zeros((GLA_DV, GLA_DK), F32), st_ref[h]) for h in heads]
    live = {}

    def decay(i):
        sl = slice(i * pair, (i + 1) * pair)
        b_all = _dot(tri, gk_ref[sl, :])
        res = []
        for h in heads:
            b = b_all[:, ks[h]]
            b_end0 = b[CHUNK - 1:CHUNK, :]
            b_end1 = b[pair - 1:pair, :]
            k_dec = (k_ref[sl, ks[h]].astype(F32)
                     * jnp.exp(jnp.where(first, b_end0, b_end1) - b)).astype(BF16)
            zero = jnp.zeros_like(k_dec)
            kd = jnp.concatenate([jnp.where(first, k_dec, zero), jnp.where(first, zero, k_dec)],
                                 axis=1)
            res.append((jnp.exp(b_end0), jnp.exp(b_end1), kd))
        live["d", i] = res

    def key_values(i):
        sl = slice(i * pair, (i + 1) * pair)
        live["kv", i] = [(a0, a1, _dot(vt_ref[vs[h], sl], kd))
                         for h, (a0, a1, kd) in zip(heads, live.pop(("d", i)))]

    def state_and_out(i):
        sl = slice(i * pair, (i + 1) * pair)
        dk = live.pop(("kv", i))
        outs = []
        for h in heads:
            a0, a1, kv = dk[h]
            st0 = sts[h] * a0 + kv[:, :GLA_DK]
            st1 = st0 * a1 + kv[:, GLA_DK:]
            sts[h] = st1
            q = q_ref[sl, ks[h]]
            outs.append(jnp.concatenate([_dot_nt(q[:CHUNK], st0.astype(BF16)),
                                         _dot_nt(q[CHUNK:], st1.astype(BF16))], axis=0))
        for h in heads:
            o = outs[h]
            inv = lax.rsqrt(jnp.mean(o * o, axis=-1, keepdims=True) + eps_scaled)
            write(sl, vs[h], (o * inv).astype(BF16) * g_ref[sl, vs[h]])

    def save_state():
        for h in heads:
            st_ref[h] = sts[h]

    stages = [functools.partial(decay, 0)]
    for i in range(n_pairs):
        if i + 1 < n_pairs:
            stages.append(functools.partial(decay, i + 1))
        stages.append(functools.partial(key_values, i))
        if i > 0:
            stages.append(functools.partial(state_and_out, i - 1))
    stages += [functools.partial(state_and_out, n_pairs - 1), save_state]
    return stages


def _mix_stages(x_ref, gate_ref, wa_ref, wg_ref, wm_ref, oa_ref, og_ref, mg_ref, o_ref):
    pieces = [slice(c * MIX_CHUNK, (c + 1) * MIX_CHUNK) for c in range(D_MODEL // MIX_CHUNK)]

    def merge(cs):
        yg = _dot(og_ref[...], wg_ref[:, cs])
        ya = _dot(oa_ref[...], wa_ref[:, cs])
        cs_b = slice(D_MODEL + cs.start, D_MODEL + cs.stop)
        mg_ref[:, cs] = (gate_ref[:, cs].astype(F32) * ya
                         + gate_ref[:, cs_b].astype(F32) * yg).astype(BF16)

    def project(cs):
        o_ref[:, cs] = x_ref[:, cs] + _dot(mg_ref[...], wm_ref[:, cs])

    return ([functools.partial(merge, cs) for cs in pieces]
            + [functools.partial(project, cs) for cs in pieces])


def _interleave(major, minor):
    out, done = [], 0
    for n, stage in enumerate(major):
        out.append(stage)
        want = (n + 1) * len(minor) // len(major)
        out.extend(minor[done:want])
        done = want
    return out


def _run_lookahead(i, n_tiles, lead, lag, merge, init=None):
    @pl.when(i == 0)
    def _():
        if init is not None:
            init()
        for stage in merge(lead(), []):
            stage()

    @pl.when(jnp.logical_and(i > 0, i < n_tiles))
    def _():
        for stage in merge(lead(), lag()):
            stage()

    @pl.when(i == n_tiles)
    def _():
        for stage in merge([], lag()):
            stage()


def _attn_mix_kernel(tiles_per_seq, n_cast,
                     sink_ref, q_ref, k_ref, kp_ref, vt_ref, vtp_ref,
                     qg_ref, kg_ref, gk_ref, vgt_ref, gg_ref,
                     x_ref, gate_ref, wa_ref, wg_ref, wm_ref, *rest):
    cast_src, (o_ref, *cast_dst) = rest[:n_cast], rest[n_cast:2 * n_cast + 1]
    oa_scr, og_scr, mg_scr, st_ref = rest[2 * n_cast + 1:]
    i = pl.program_id(0)
    n_tiles = pl.num_programs(0) - 1
    seq_start = (i % tiles_per_seq) == 0
    slot_w = i % 2
    slot_r = 1 - slot_w

    def write_oa(rows, cols, val):
        oa_scr[slot_w, rows, cols] = val

    def write_og(rows, cols, val):
        og_scr[slot_w, rows, cols] = val

    def mixers():
        swa = _swa_stages(sink_ref, q_ref, k_ref, kp_ref, vt_ref, vtp_ref, seq_start, write_oa)
        gla = _gla_stages(qg_ref, kg_ref, gk_ref, vgt_ref, gg_ref, st_ref, seq_start, write_og)
        n = len(swa)
        order = [swa[0][0]]
        for p in range(n):
            if p + 1 < n:
                order.append(swa[p + 1][0])
            order.append(swa[p][1])
            if p > 0:
                order.append(swa[p - 1][2])
        order.append(swa[n - 1][2])
        return _interleave(order, gla) + [convert_weights]

    def projections():
        return _mix_stages(x_ref, gate_ref, wa_ref, wg_ref, wm_ref,
                           oa_scr.at[slot_r], og_scr.at[slot_r], mg_scr, o_ref)

    def merge(lead, lag):
        return _interleave(lead, lag) if lead else lag

    def convert_weights():
        for src, dst in zip(cast_src, cast_dst):
            dst[...] = src[...].astype(BF16)

    def init():
        st_ref[...] = jnp.zeros_like(st_ref)

    _run_lookahead(i, n_tiles, mixers, projections, merge, init)


def _attn_mix(sinks, qa, ka, vat, qg, kg, gk, vgt, gg, x2d, gates, wa, wg, wm, cast_w, seq):
    t = x2d.shape[0]
    tm = TM_ATTN
    n_tiles = t // tm
    blocks = tm // SWA_BLOCK
    kw = ka.shape[1]
    cur = lambda i: jnp.minimum(i, n_tiles - 1)
    prv = lambda i: jnp.maximum(i - 1, 0)
    prev_block = lambda i: jnp.maximum(cur(i) * blocks - 1, 0)
    tok = lambda n: pl.BlockSpec((tm, n), lambda i: (cur(i), 0))
    col = lambda n: pl.BlockSpec((n, tm), lambda i: (0, cur(i)))
    old = lambda n: pl.BlockSpec((tm, n), lambda i: (prv(i), 0))

    def slab(w):
        n_slabs = max(n for n in range(1, n_tiles + 1) if w.shape[0] % (16 * n) == 0)
        return pl.BlockSpec((w.shape[0] // n_slabs, w.shape[1]),
                            lambda i: (jnp.minimum(i, n_slabs - 1), 0))

    cast_specs = [slab(w) for w in cast_w]
    return pl.pallas_call(
        functools.partial(_attn_mix_kernel, seq // tm, len(cast_w)),
        grid=(n_tiles + 1,),
        in_specs=[
            pl.BlockSpec(memory_space=pltpu.SMEM),
            tok(qa.shape[1]), tok(kw),
            pl.BlockSpec((SWA_BLOCK, kw), lambda i: (prev_block(i), 0)),
            col(kw),
            pl.BlockSpec((kw, SWA_BLOCK), lambda i: (0, prev_block(i))),
            tok(GLA_KEY_DIM), tok(GLA_KEY_DIM), tok(GLA_KEY_DIM), col(GLA_VAL_DIM),
            tok(GLA_VAL_DIM),
            old(D_MODEL), old(2 * D_MODEL),
            _const_spec(wa.shape), _const_spec(wg.shape), _const_spec(wm.shape),
        ] + cast_specs,
        out_specs=[old(D_MODEL)] + cast_specs,
        out_shape=[jax.ShapeDtypeStruct((t, D_MODEL), F32)]
        + [jax.ShapeDtypeStruct(w.shape, BF16) for w in cast_w],
        scratch_shapes=[
            pltpu.VMEM((2, tm, SWA_HEADS * SWA_HEAD_DIM), BF16),
            pltpu.VMEM((2, tm, GLA_VAL_DIM), BF16),
            pltpu.VMEM((tm, D_MODEL), BF16),
            pltpu.VMEM((GLA_HEADS, GLA_DV, GLA_DK), F32),
        ],
        compiler_params=_params(("arbitrary",)),
        name="attn_mix",
    )(sinks, qa, ka, ka, vat, vat, qg, kg, gk, vgt, gg, x2d, gates, wa, wg, wm, *cast_w)


def _mem_kv_kernel(m_ref, nw_ref, w_ref, k_ref, v_ref):
    mn = _rms(m_ref[0], nw_ref[...]).astype(BF16)
    kv = _dot(mn, w_ref[...])
    k = kv[:, :MEM_WIDTH].astype(BF16)
    v = kv[:, MEM_WIDTH:].astype(BF16)
    head = lax.broadcasted_iota(jnp.int32, (N_MEM, MEM_WIDTH), 1) // MEM_HEAD_DIM
    zero = jnp.zeros_like(k)
    for h in range(MEM_HEADS):
        k_ref[0, h] = jnp.where(head == h, k, zero)
        v_ref[0, h] = jnp.where(head == h, v, zero)


def _mem_kv(mem, nw, w):
    batch = mem.shape[0]
    out = jax.ShapeDtypeStruct((batch, MEM_HEADS, N_MEM, MEM_WIDTH), BF16)
    spec = pl.BlockSpec((1, MEM_HEADS, N_MEM, MEM_WIDTH), lambda b: (b, 0, 0, 0))
    return pl.pallas_call(
        _mem_kv_kernel,
        grid=(batch,),
        in_specs=[pl.BlockSpec((1, N_MEM, D_MODEL), lambda b: (b, 0, 0)),
                  _const_spec(nw.shape), _const_spec(w.shape)],
        out_specs=(spec, spec),
        out_shape=(out, out),
        compiler_params=_params(("arbitrary",)),
        name="mem_kv",
    )(mem, nw, w)


def _xattn_stages(x_ref, k_ref, v_ref, nq_ref, wq_ref, wo_ref, nf_ref, x2_ref, hf_ref):
    live = {}

    def queries():
        hq = _rms(x_ref[...], nq_ref[...]).astype(BF16)
        live["q"] = (_dot(hq, wq_ref[...]) * (MEM_HEAD_DIM ** -0.5)).astype(BF16)
        live["o"] = jnp.zeros((x_ref.shape[0], MEM_WIDTH), F32)

    def head(h):
        s = _dot_nt(live["q"], k_ref[0, h])
        e = jnp.exp(s - jnp.max(s, axis=-1, keepdims=True))
        inv = 1.0 / jnp.sum(e, axis=-1, keepdims=True)
        live["o"] = live["o"] + _dot(e.astype(BF16), v_ref[0, h]) * inv

    def project():
        o = live.pop("o").astype(BF16)
        for c in range(D_MODEL // MIX_CHUNK):
            cs = slice(c * MIX_CHUNK, (c + 1) * MIX_CHUNK)
            x2_ref[:, cs] = x_ref[:, cs] + _dot(o, wo_ref[:, cs])
        hf_ref[...] = _rms(x2_ref[...], nf_ref[...]).astype(BF16)

    return [queries] + [functools.partial(head, h) for h in range(MEM_HEADS)] + [project]


def _ffn_stages(x2_ref, hf_ref, wg_ref, wu_ref, wd_ref, nfin_ref, act_ref, o_ref):
    live = {"ss": jnp.zeros((o_ref.shape[0], 1), F32)}

    def hidden(cs):
        hf = hf_ref[...]
        g = _dot(hf, wg_ref[:, cs])
        u = _dot(hf, wu_ref[:, cs])
        act_ref[:, cs] = (g * jax.nn.sigmoid(g) * u).astype(BF16)

    def down(cs):
        y = x2_ref[:, cs] + _dot(act_ref[...], wd_ref[:, cs])
        live["ss"] = live["ss"] + jnp.sum(y * y, axis=-1, keepdims=True)
        o_ref[:, cs] = y

    def final_norm():
        inv = lax.rsqrt(live["ss"] * (1.0 / D_MODEL) + EPS)
        o_ref[...] = o_ref[...] * inv * nfin_ref[...]

    hid = [slice(c * FF_CHUNK, (c + 1) * FF_CHUNK) for c in range(D_FF // FF_CHUNK)]
    out = [slice(c * MIX_CHUNK, (c + 1) * MIX_CHUNK) for c in range(D_MODEL // MIX_CHUNK)]
    return ([functools.partial(hidden, cs) for cs in hid]
            + [functools.partial(down, cs) for cs in out] + [final_norm])


def _tail_kernel(x_ref, k_ref, v_ref, nq_ref, wq_ref, wo_ref, nf_ref, wg_ref, wu_ref, wd_ref,
                 nfin_ref, o_ref, x2_scr, hf_scr, act_ref):
    i = pl.program_id(0)
    n_tiles = pl.num_programs(0) - 1
    slot_w = i % 2
    slot_r = 1 - slot_w

    def xattn():
        return _xattn_stages(x_ref, k_ref, v_ref, nq_ref, wq_ref, wo_ref, nf_ref,
                             x2_scr.at[slot_w], hf_scr.at[slot_w])

    def ffn():
        return _ffn_stages(x2_scr.at[slot_r], hf_scr.at[slot_r], wg_ref, wu_ref, wd_ref,
                           nfin_ref, act_ref, o_ref)

    def merge(lead, lag):
        if not lag:
            return lead
        n_hidden = D_FF // FF_CHUNK
        n_first = n_hidden - len(lead)
        return lag[:n_first] + _interleave(lag[n_first:n_hidden], lead) + lag[n_hidden:]

    _run_lookahead(i, n_tiles, xattn, ffn, merge)


def _tail(x2d, kext, vext, nq, wq, wo, nf, wg, wu, wd, nfin, seq):
    tm = TM_TAIL
    t = x2d.shape[0]
    n_tiles = t // tm
    tiles_per_seq = seq // tm
    cur = lambda i: jnp.minimum(i, n_tiles - 1)
    prv = lambda i: jnp.maximum(i - 1, 0)
    mem_spec = pl.BlockSpec((1, MEM_HEADS, N_MEM, MEM_WIDTH),
                            lambda i: (cur(i) // tiles_per_seq, 0, 0, 0))
    consts = (nq, wq, wo, nf, wg, wu, wd, nfin)
    return pl.pallas_call(
        _tail_kernel,
        grid=(n_tiles + 1,),
        in_specs=[pl.BlockSpec((tm, D_MODEL), lambda i: (cur(i), 0)), mem_spec, mem_spec]
        + [_const_spec(c.shape) for c in consts],
        out_specs=pl.BlockSpec((tm, D_MODEL), lambda i: (prv(i), 0)),
        out_shape=jax.ShapeDtypeStruct((t, D_MODEL), F32),
        scratch_shapes=[pltpu.VMEM((2, tm, D_MODEL), F32), pltpu.VMEM((2, tm, D_MODEL), BF16),
                        pltpu.VMEM((tm, D_FF), BF16)],
        compiler_params=_params(("arbitrary",)),
        name="tail",
    )(x2d, kext, vext, *consts)


def _layer(x2d, mem, batch, seq, norm_mix_w, w_in, b_gate, attn_sinks, gla_gate_w2, gla_gate_b,
           gla_norm_w, w_attn_o, w_gla_o, w_mix_o, norm_mem_q_w, norm_mem_kv_w, w_mem_q,
           w_mem_kv, w_mem_o, norm_ffn_w, w_ffn_gate, w_ffn_up, w_ffn_down, out_norm_w):
    w2_p = jnp.pad(gla_gate_w2, ((0, LANES - GLA_GATE_RANK), (0, 0)))
    bf = lambda a: a.astype(BF16)
    r2 = lambda a: a.reshape(1, -1)

    qa, ka, vat, qg, kg, vgt, gg, gk, gates, wa16, wgo16, wmo16 = _in_proj(
        x2d, r2(norm_mix_w), bf(w_in.T), bf(w2_p), r2(gla_gate_b), r2(b_gate),
        r2(jnp.tile(gla_norm_w, GLA_HEADS)), w_attn_o, w_gla_o, w_mix_o)
    x1, wg16, wu16, wd16, wq16, wo16, wkv16 = _attn_mix(
        attn_sinks, qa, ka, vat, qg, kg, gk, vgt, gg, x2d, gates, wa16, wgo16, wmo16,
        (w_ffn_gate, w_ffn_up, w_ffn_down, w_mem_q, w_mem_o, w_mem_kv), seq)
    kext, vext = _mem_kv(mem, r2(norm_mem_kv_w), wkv16)
    return _tail(x1, kext, vext, r2(norm_mem_q_w), wq16, wo16, r2(norm_ffn_w),
                 wg16, wu16, wd16, r2(out_norm_w), seq)


def kernel(x, mem, norm_mix_w, w_in, b_gate, attn_sinks, gla_gate_w2, gla_gate_b, gla_norm_w,
           w_attn_o, w_gla_o, w_mix_o, norm_mem_q_w, norm_mem_kv_w, w_mem_q, w_mem_kv, w_mem_o,
           norm_ffn_w, w_ffn_gate, w_ffn_up, w_ffn_down, norm_final_w):
    batch, seq, d = x.shape
    depth = w_in.shape[0]
    assert depth == 1 and d == D_MODEL
    assert seq % TM_PROJ == 0 and seq % TM_ATTN == 0 and seq % TM_TAIL == 0
    out = _layer(x.reshape(batch * seq, d), mem, batch, seq, norm_mix_w[0], w_in[0], b_gate[0],
                 attn_sinks[0], gla_gate_w2[0], gla_gate_b[0], gla_norm_w[0], w_attn_o[0],
                 w_gla_o[0], w_mix_o[0], norm_mem_q_w[0], norm_mem_kv_w[0], w_mem_q[0],
                 w_mem_kv[0], w_mem_o[0], norm_ffn_w[0], w_ffn_gate[0], w_ffn_up[0],
                 w_ffn_down[0], norm_final_w)
    return out.reshape(batch, seq, d)
```
